```python
import math
import jax
import jax.numpy as jnp
from jax import lax
import numpy as np


D_MODEL = 1024
BATCH = 8
SEQ = 4096
DEPTH = 4

GRID_W = 64
CTX_LEN = 256
CHUNK = 64
EPS = 1e-6
N_MOD = 6
S5_WIDTH = D_MODEL // 4
S5_GROUP = 16
S5_GROUPS = S5_WIDTH // S5_GROUP
S5_STATE = 64
HG_HEADS = 6
HG_DK = 64
HG_DV = 64
HG_QK = HG_HEADS * HG_DK
HG_WIDTH = HG_HEADS * HG_DV
RET_HEADS = 6
RET_DK = 64
RET_DV = 64
RET_QK = RET_HEADS * RET_DK
RET_WIDTH = RET_HEADS * RET_DV
ROPE_BASE = 10000.0
N_BRANCH = 3
IN_SIZES = (S5_WIDTH, HG_QK, HG_QK, HG_QK, HG_WIDTH, HG_WIDTH, RET_QK, RET_QK, RET_WIDTH, RET_WIDTH, N_BRANCH * D_MODEL)
IN_WIDTH = sum(IN_SIZES)
N_GROUPS = 4
EXPERTS_PER_GROUP = 8
N_EXPERTS = N_GROUPS * EXPERTS_PER_GROUP
TOP_K = 2
EXPERT_HIDDEN = D_MODEL // 2
MOE_BLOCK = 128
F32 = jnp.float32

kernel_name = 'hybrid_s5_hgrn2_retention_hmoe_prefix_dit'


def rmsnorm(x, g):
    xf = x.astype(F32)
    y = xf * lax.rsqrt(jnp.mean(xf * xf, axis=-1, keepdims=True) + EPS)
    return (y * g.astype(F32)).astype(x.dtype)


def head_rmsnorm(o):
    return o * lax.rsqrt(jnp.mean(o * o, axis=-1, keepdims=True) + EPS)


def modulate(h, shift, scale):
    return h * (1.0 + scale) + shift


def axial_rope(rows):
    row = jnp.repeat(jnp.arange(rows, dtype=F32), GRID_W)
    col = jnp.broadcast_to(jnp.arange(GRID_W, dtype=F32)[None, :], (rows, GRID_W)).reshape(-1)
    n_freq = RET_DK // 4
    inv_freq = ROPE_BASE ** (-jnp.arange(n_freq, dtype=F32) / n_freq)
    ang = jnp.concatenate([row[:, None] * inv_freq, col[:, None] * inv_freq], axis=-1)
    return jnp.cos(ang), jnp.sin(ang)


def apply_rope(x, cos, sin):
    half = x.shape[-1] // 2
    x1, x2 = x[..., :half], x[..., half:]
    cs, sn = cos[None, :, None, :], sin[None, :, None, :]
    return jnp.concatenate([x1 * cs - x2 * sn, x1 * sn + x2 * cs], axis=-1)


def chunk_gated_recurrence(q, k, v, logf, h0, strict):
    B, L, H, K = q.shape
    V = v.shape[-1]
    n = L // CHUNK

    def blocks(a):
        return a.reshape(B, n, CHUNK, H, a.shape[-1]).transpose(1, 0, 3, 2, 4)

    idx = jnp.arange(CHUNK)
    mask = (idx[:, None] > idx[None, :]) if strict else (idx[:, None] >= idx[None, :])

    def step(h, inp):
        qi, ki, vi, fi = inp
        b = jnp.cumsum(fi, axis=2)
        o = jnp.einsum('bhtk,bhkv->bhtv', qi * jnp.exp(b), h)
        diff = b[:, :, :, None, :] - b[:, :, None, :, :]
        dec = jnp.exp(jnp.where(mask[:, :, None], diff, -jnp.inf))
        scores = jnp.einsum('bhtk,bhsk,bhtsk->bhts', qi, ki, dec)
        o = o + jnp.einsum('bhts,bhsv->bhtv', scores, vi)
        b_end = b[:, :, -1:, :]
        h = jnp.exp(b_end[:, :, 0, :])[..., None] * h + jnp.einsum('bhsk,bhsv->bhkv', ki * jnp.exp(b_end - b), vi)
        return h, o

    h_final, o = lax.scan(step, h0, (blocks(q), blocks(k), blocks(v), blocks(logf)))
    return o.transpose(1, 0, 3, 2, 4).reshape(B, L, H, V), h_final


def bidir_prefix_recurrence(q, k_fwd, k_bwd, v, logf_fwd, logf_bwd, n_ctx, strict_bwd):
    B, T, H, K = q.shape
    h0 = jnp.zeros((B, H, K, v.shape[-1]), F32)

    def run(k, logf, reverse, strict):
        h = h0
        outs = []
        for lo, hi in ((0, n_ctx), (n_ctx, T)):
            seg = [a[:, lo:hi] for a in (q, k, v, logf)]
            if reverse:
                seg = [jnp.flip(a, axis=1) for a in seg]
            o, h = chunk_gated_recurrence(seg[0], seg[1], seg[2], seg[3], h, strict)
            outs.append(jnp.flip(o, axis=1) if reverse else o)
        return jnp.concatenate(outs, axis=1)

    return run(k_fwd, logf_fwd, False, False) + run(k_bwd, logf_bwd, True, strict_bwd)


def s5_scan(a_re, a_im, b_re, b_im, h0_re, h0_im, reverse):
    end = -1 if reverse else 0
    b_re = b_re.at[:, end].add(a_re * h0_re - a_im * h0_im)
    b_im = b_im.at[:, end].add(a_re * h0_im + a_im * h0_re)
    shape = (1,) + b_re.shape[1:]
    A_re = jnp.broadcast_to(a_re, shape)
    A_im = jnp.broadcast_to(a_im, shape)

    def combine(e1, e2):
        a1r, a1i, b1r, b1i = e1
        a2r, a2i, b2r, b2i = e2
        return (a2r * a1r - a2i * a1i, a2r * a1i + a2i * a1r,
                a2r * b1r - a2i * b1i + b2r, a2r * b1i + a2i * b1r + b2i)

    _, _, h_re, h_im = lax.associative_scan(combine, (A_re, A_im, b_re, b_im), reverse=reverse, axis=1)
    return h_re, h_im


def s5_mixer(u, n_ctx, lam_re, lam_im, log_dt, b_re, b_im, c_re, c_im, d_skip, w_glu):
    B, T, _ = u.shape
    uf = u.astype(F32).reshape(B, T, S5_GROUPS, S5_GROUP)
    y = d_skip.astype(F32).reshape(S5_GROUPS, S5_GROUP) * uf
    for d in range(2):
        reverse = d == 1
        lr = lam_re[d].astype(F32)
        li = lam_im[d].astype(F32)
        dt = jnp.exp(log_dt[d].astype(F32))[:, None]
        mag = jnp.exp(lr * dt)
        ar = mag * jnp.cos(li * dt)
        ai = mag * jnp.sin(li * dt)
        den = lr * lr + li * li
        zr = ((ar - 1.0) * lr + ai * li) / den
        zi = (ai * lr - (ar - 1.0) * li) / den
        br = b_re[d].astype(F32)
        bi = b_im[d].astype(F32)
        bbar_re = zr[..., None] * br - zi[..., None] * bi
        bbar_im = zr[..., None] * bi + zi[..., None] * br
        bu_re = jnp.einsum('btgh,gph->btgp', uf, bbar_re)
        bu_im = jnp.einsum('btgh,gph->btgp', uf, bbar_im)
        h_re = jnp.zeros((B, S5_GROUPS, S5_STATE), F32)
        h_im = jnp.zeros((B, S5_GROUPS, S5_STATE), F32)
        seg_re, seg_im = [], []
        for lo, hi in ((0, n_ctx), (n_ctx, T)):
            s_re, s_im = s5_scan(ar, ai, bu_re[:, lo:hi], bu_im[:, lo:hi], h_re, h_im, reverse)
            last = 0 if reverse else -1
            h_re, h_im = s_re[:, last], s_im[:, last]
            seg_re.append(s_re)
            seg_im.append(s_im)
        s_re = jnp.concatenate(seg_re, axis=1)
        s_im = jnp.concatenate(seg_im, axis=1)
        y = y + jnp.einsum('ghp,btgp->btgh', c_re[d].astype(F32), s_re) - jnp.einsum('ghp,btgp->btgh', c_im[d].astype(F32), s_im)
    y = jax.nn.gelu(y.reshape(B, T, S5_WIDTH))
    return y * jax.nn.sigmoid(y @ w_glu.astype(F32))


def hgrn2_mixer(q, zf_fwd, zf_bwd, v, g, lb, n_ctx):
    B, T, _ = q.shape

    def forget(z, lb_d):
        f = lb_d + (1.0 - lb_d) * jax.nn.sigmoid(z.astype(F32))
        f = f.reshape(B, T, HG_HEADS, HG_DK)
        return jnp.log(f), 1.0 - f

    logf_f, k_f = forget(zf_fwd, lb[0])
    logf_b, k_b = forget(zf_bwd, lb[1])
    qh = q.astype(F32).reshape(B, T, HG_HEADS, HG_DK)
    vh = v.astype(F32).reshape(B, T, HG_HEADS, HG_DV)
    o = bidir_prefix_recurrence(qh, k_f, k_b, vh, logf_f, logf_b, n_ctx, False)
    return head_rmsnorm(o).reshape(B, T, HG_WIDTH) * jax.nn.silu(g.astype(F32))


def retention_mixer(q, k, v, g, n_ctx, rope_cos, rope_sin):
    B, T, _ = q.shape
    qh = q.astype(F32).reshape(B, T, RET_HEADS, RET_DK)
    kh = k.astype(F32).reshape(B, T, RET_HEADS, RET_DK) * (RET_DK ** -0.5)
    qh = jnp.concatenate([qh[:, :n_ctx], apply_rope(qh[:, n_ctx:], rope_cos, rope_sin)], axis=1)
    kh = jnp.concatenate([kh[:, :n_ctx], apply_rope(kh[:, n_ctx:], rope_cos, rope_sin)], axis=1)
    vh = v.astype(F32).reshape(B, T, RET_HEADS, RET_DV)
    log_gamma = jnp.log(1.0 - 2.0 ** (-5.0 - jnp.arange(RET_HEADS, dtype=F32)))
    logf = jnp.broadcast_to(log_gamma[:, None], qh.shape)
    o = bidir_prefix_recurrence(qh, kh, kh, vh, logf, logf, n_ctx, True)
    return head_rmsnorm(o).reshape(B, T, RET_WIDTH) * jax.nn.silu(g.astype(F32))


def token_mixer(h, n_ctx, rope_cos, rope_sin, lb, w_in, lam_re, lam_im, log_dt, b_re, b_im, c_re, c_im,
                d_skip, w_glu, w_br_s5, w_br_hg, w_br_ret, w_out):
    B, T, D = h.shape
    proj = h @ w_in
    parts = jnp.split(proj, np.cumsum(IN_SIZES)[:-1].tolist(), axis=-1)
    u, hq, hf_f, hf_b, hv, hg, rq, rk, rv, rg, gz = parts
    y_s5 = s5_mixer(u, n_ctx, lam_re, lam_im, log_dt, b_re, b_im, c_re, c_im, d_skip, w_glu)
    y_hg = hgrn2_mixer(hq, hf_f, hf_b, hv, hg, lb, n_ctx)
    y_ret = retention_mixer(rq, rk, rv, rg, n_ctx, rope_cos, rope_sin)
    gates = jax.nn.sigmoid(gz.astype(F32)).reshape(B, T, N_BRANCH, D)
    merged = (gates[:, :, 0] * (y_s5 @ w_br_s5.astype(F32))
              + gates[:, :, 1] * (y_hg @ w_br_hg.astype(F32))
              + gates[:, :, 2] * (y_ret @ w_br_ret.astype(F32)))
    return (merged @ w_out.astype(F32)).astype(h.dtype)


def hier_moe(t, w_grp, b_grp, w_exp, b_exp, w_gate, w_up, w_down):
    N, D = t.shape
    tf = t.astype(F32)
    grp_prob = jax.nn.softmax(tf @ w_grp.astype(F32) + b_grp.astype(F32), axis=-1)
    gp, gi = lax.top_k(grp_prob, 1)
    exp_logits = (tf @ w_exp.astype(F32) + b_exp.astype(F32)).reshape(N, N_GROUPS, EXPERTS_PER_GROUP)
    exp_logits = exp_logits[jnp.arange(N), gi[:, 0]]
    ep, ei = lax.top_k(jax.nn.softmax(exp_logits, axis=-1), TOP_K)
    wts = gp * ep / jnp.sum(ep, axis=-1, keepdims=True)
    eid = gi * EXPERTS_PER_GROUP + ei
    A = N * TOP_K
    flat_e = eid.reshape(A)
    flat_w = wts.reshape(A)
    flat_tok = jnp.arange(A, dtype=jnp.int32) // TOP_K
    order = jnp.argsort(flat_e)
    se = flat_e[order]
    counts = jnp.zeros((N_EXPERTS,), jnp.int32).at[flat_e].add(1)
    padded = (counts + MOE_BLOCK - 1) // MOE_BLOCK * MOE_BLOCK
    pad_end = jnp.cumsum(padded)
    pad_start = pad_end - padded
    start = jnp.cumsum(counts) - counts
    dest = pad_start[se] + jnp.arange(A, dtype=jnp.int32) - start[se]
    n_blocks = -(-A // MOE_BLOCK) + N_EXPERTS
    slot_tok = jnp.full((n_blocks * MOE_BLOCK,), N, jnp.int32).at[dest].set(flat_tok[order])
    slot_w = jnp.zeros((n_blocks * MOE_BLOCK,), F32).at[dest].set(flat_w[order])
    blk_e = jnp.minimum(jnp.searchsorted(pad_end, jnp.arange(n_blocks) * MOE_BLOCK, side='right'), N_EXPERTS - 1)
    t_pad = jnp.concatenate([t, jnp.zeros((1, D), t.dtype)], axis=0)

    def block_fn(args):
        e, tok = args
        xb = t_pad[tok]
        return (jax.nn.silu(xb @ w_gate[e]) * (xb @ w_up[e])) @ w_down[e]

    yb = lax.map(block_fn, (blk_e, slot_tok.reshape(n_blocks, MOE_BLOCK)))
    y = jnp.zeros((N + 1, D), F32).at[slot_tok].add(yb.reshape(-1, D).astype(F32) * slot_w[:, None])
    return y[:N].astype(t.dtype)


def setup_inputs(seed: int = 0) -> dict:
    key = jax.random.key(seed)
    ks = iter(jax.random.split(key, 40))

    def nrm(shape, scale):
        return scale * jax.random.normal(next(ks), shape, F32)

    G, P, Hg = S5_GROUPS, S5_STATE, S5_GROUP
    E, F = N_EXPERTS, EXPERT_HIDDEN
    inputs = {}
    inputs['x'] = nrm((BATCH, SEQ, D_MODEL), 1.0)
    inputs['c'] = nrm((BATCH, D_MODEL), 1.0)
    inputs['ctx'] = nrm((BATCH, CTX_LEN, D_MODEL), 1.0)
    inputs['c_ctx'] = nrm((D_MODEL,), 1.0)
    inputs['w_ada'] = nrm((DEPTH, D_MODEL, N_MOD * D_MODEL), 0.5 * D_MODEL ** -0.5)
    inputs['b_ada'] = nrm((DEPTH, N_MOD * D_MODEL), 0.01)
    inputs['norm1_g'] = 1.0 + nrm((DEPTH, D_MODEL), 0.02)
    inputs['norm2_g'] = 1.0 + nrm((DEPTH, D_MODEL), 0.02)
    inputs['w_in'] = nrm((DEPTH, D_MODEL, IN_WIDTH), D_MODEL ** -0.5)
    inputs['s5_lam_re'] = -0.5 + nrm((DEPTH, 2, G, P), 0.01)
    inputs['s5_lam_im'] = jnp.pi * jnp.arange(P, dtype=F32) + nrm((DEPTH, 2, G, P), 0.01)
    inputs['s5_log_dt'] = jax.random.uniform(next(ks), (DEPTH, 2, G), F32, math.log(1e-3), math.log(1e-1))
    inputs['s5_b_re'] = nrm((DEPTH, 2, G, P, Hg), (2.0 * Hg) ** -0.5)
    inputs['s5_b_im'] = nrm((DEPTH, 2, G, P, Hg), (2.0 * Hg) ** -0.5)
    inputs['s5_c_re'] = nrm((DEPTH, 2, G, Hg, P), P ** -0.5)
    inputs['s5_c_im'] = nrm((DEPTH, 2, G, Hg, P), P ** -0.5)
    inputs['s5_d'] = nrm((DEPTH, S5_WIDTH), 1.0)
    inputs['s5_w_glu'] = nrm((DEPTH, S5_WIDTH, S5_WIDTH), S5_WIDTH ** -0.5)
    inputs['hgrn_lb_raw'] = nrm((DEPTH, 2, HG_QK), 0.5)
    inputs['w_branch_s5'] = nrm((DEPTH, S5_WIDTH, D_MODEL), S5_WIDTH ** -0.5)
    inputs['w_branch_hgrn'] = nrm((DEPTH, HG_WIDTH, D_MODEL), HG_WIDTH ** -0.5)
    inputs['w_branch_ret'] = nrm((DEPTH, RET_WIDTH, D_MODEL), RET_WIDTH ** -0.5)
    inputs['w_out'] = nrm((DEPTH, D_MODEL, D_MODEL), D_MODEL ** -0.5)
    inputs['moe_w_group'] = nrm((DEPTH, D_MODEL, N_GROUPS), D_MODEL ** -0.5)
    inputs['moe_b_group'] = nrm((DEPTH, N_GROUPS), 0.01)
    inputs['moe_w_expert'] = nrm((DEPTH, D_MODEL, E), D_MODEL ** -0.5)
    inputs['moe_b_expert'] = nrm((DEPTH, E), 0.01)
    inputs['moe_w_gate'] = nrm((DEPTH, E, D_MODEL, F), D_MODEL ** -0.5)
    inputs['moe_w_up'] = nrm((DEPTH, E, D_MODEL, F), D_MODEL ** -0.5)
    inputs['moe_w_down'] = nrm((DEPTH, E, F, D_MODEL), F ** -0.5)
    inputs['final_norm_g'] = 1.0 + nrm((D_MODEL,), 0.02)
    return inputs


def reference(x, c, ctx, c_ctx, w_ada, b_ada, norm1_g, norm2_g, w_in, s5_lam_re, s5_lam_im, s5_log_dt,
              s5_b_re, s5_b_im, s5_c_re, s5_c_im, s5_d, s5_w_glu, hgrn_lb_raw, w_branch_s5, w_branch_hgrn,
              w_branch_ret, w_out, moe_w_group, moe_b_group, moe_w_expert, moe_b_expert, moe_w_gate,
              moe_w_up, moe_w_down, final_norm_g):
    B, L, D = x.shape
    n_ctx = ctx.shape[1]
    rows = L // GRID_W
    rope_cos, rope_sin = axial_rope(rows)
    p = jax.nn.softmax(hgrn_lb_raw.astype(F32), axis=0)
    lower_bounds = jnp.cumsum(p, axis=0) - p[0:1]
    xl, xc = x, ctx
    for l in range(DEPTH):
        last = l == DEPTH - 1
        mod_l = (jax.nn.silu(c) @ w_ada[l] + b_ada[l]).reshape(B, N_MOD, 1, D)
        mod_c = (jax.nn.silu(c_ctx) @ w_ada[l] + b_ada[l]).reshape(N_MOD, D)
        h = jnp.concatenate([modulate(rmsnorm(xc, norm1_g[l]), mod_c[0], mod_c[1]),
                             modulate(rmsnorm(xl, norm1_g[l]), mod_l[:, 0], mod_l[:, 1])], axis=1)
        mix = token_mixer(h, n_ctx, rope_cos, rope_sin, lower_bounds[l], w_in[l], s5_lam_re[l], s5_lam_im[l],
                          s5_log_dt[l], s5_b_re[l], s5_b_im[l], s5_c_re[l], s5_c_im[l], s5_d[l], s5_w_glu[l],
                          w_branch_s5[l], w_branch_hgrn[l], w_branch_ret[l], w_out[l])
        xl = xl + mod_l[:, 2] * mix[:, n_ctx:]
        h2_l = modulate(rmsnorm(xl, norm2_g[l]), mod_l[:, 3], mod_l[:, 4]).reshape(B * L, D)
        if last:
            y = hier_moe(h2_l, moe_w_group[l], moe_b_group[l], moe_w_expert[l], moe_b_expert[l],
                         moe_w_gate[l], moe_w_up[l], moe_w_down[l])
            xl = xl + mod_l[:, 5] * y.reshape(B, L, D)
        else:
            xc = xc + mod_c[2] * mix[:, :n_ctx]
            h2_c = modulate(rmsnorm(xc, norm2_g[l]), mod_c[3], mod_c[4]).reshape(B * n_ctx, D)
            y = hier_moe(jnp.concatenate([h2_c, h2_l], axis=0), moe_w_group[l], moe_b_group[l],
                         moe_w_expert[l], moe_b_expert[l], moe_w_gate[l], moe_w_up[l], moe_w_down[l])
            xc = xc + mod_c[5] * y[:B * n_ctx].reshape(B, n_ctx, D)
            xl = xl + mod_l[:, 5] * y[B * n_ctx:].reshape(B, L, D)
    return rmsnorm(xl, final_norm_g)
```

```python
import functools

import jax
import jax.numpy as jnp
import numpy as np
from jax import lax
from jax.experimental import pallas as pl
from jax.experimental.pallas import tpu as pltpu

F32 = jnp.float32
BF16 = jnp.bfloat16
HIGHEST = lax.Precision.HIGHEST

LANES = 128
SUBLANES = 8
VMEM_LIMIT = 56 * 1024 * 1024

EPS = 1e-6
N_MOD = 6
GRID_W = 64
HEAD_DIM = 64
PAIR = 2 * HEAD_DIM
S5_GROUP = 16
S5_STATE = 64
S5_CHUNK = 16
HG_CHUNK = 64
HG_HALF = HG_CHUNK // 2
RET_BLOCK = 256
ROPE_BASE = 10000.0
N_GROUPS = 4
EXPERTS_PER_GROUP = 8
N_EXPERTS = N_GROUPS * EXPERTS_PER_GROUP
TOP_K = 2
MOE_BLOCK = 512


def _cparams(*sem):
    return pltpu.CompilerParams(dimension_semantics=sem, vmem_limit_bytes=VMEM_LIMIT)


def _row_tile(t_len):
    for tm in (544, 512, 384, 272, 256, 128, 64, 32, 16, 8):
        if t_len % tm == 0:
            return tm
    raise ValueError(f"unsupported stream length {t_len}")


def _ada_kernel(c_ref, w_ref, b_ref, o_ref):
    a = jax.nn.silu(c_ref[...]).astype(BF16)
    o_ref[0] = jnp.dot(a, w_ref[0].astype(BF16), preferred_element_type=F32) + b_ref[0]


def _ada_tables(cc, w_ada, b_ada):
    depth, d, n = w_ada.shape
    rows = cc.shape[0]
    tn = 1536 if n % 1536 == 0 else n
    return pl.pallas_call(
        _ada_kernel,
        grid=(depth, n // tn),
        in_specs=[
            pl.BlockSpec((rows, d), lambda l, j: (0, 0)),
            pl.BlockSpec((1, d, tn), lambda l, j: (l, 0, j)),
            pl.BlockSpec((1, 1, tn), lambda l, j: (l, 0, j)),
        ],
        out_specs=pl.BlockSpec((1, rows, tn), lambda l, j: (l, 0, j)),
        out_shape=jax.ShapeDtypeStruct((depth, rows, n), F32),
        compiler_params=_cparams("parallel", "parallel"),
    )(cc, w_ada, b_ada.reshape(depth, 1, n))


def _mod_rows(ml_ref, mc_ref, idx, is_ctx):
    return jnp.where(is_ctx, mc_ref[0, idx:idx + 1, :], ml_ref[0, idx:idx + 1, :])


def _is_ctx_rows(tm, tiles_per_b, n_ctx):
    jt = pl.program_id(0) % tiles_per_b
    row = jt * tm + lax.broadcasted_iota(jnp.int32, (tm, 1), 0)
    return row < n_ctx


def _rmsnorm_rows(x, g):
    return x * lax.rsqrt(jnp.mean(x * x, axis=-1, keepdims=True) + EPS) * g


def _norm_mod_kernel(x_ref, g_ref, ml_ref, mc_ref, o_ref, *, tm, tiles_per_b, n_ctx):
    is_ctx = _is_ctx_rows(tm, tiles_per_b, n_ctx)
    y = _rmsnorm_rows(x_ref[...], g_ref[...])
    y = y * (1.0 + _mod_rows(ml_ref, mc_ref, 1, is_ctx)) + _mod_rows(ml_ref, mc_ref, 0, is_ctx)
    o_ref[...] = y.astype(o_ref.dtype)


def _norm_mod(x2, g, mod_l, mod_c, t_len, n_ctx):
    m, d = x2.shape
    tm = _row_tile(t_len)
    tpb = t_len // tm
    kern = functools.partial(_norm_mod_kernel, tm=tm, tiles_per_b=tpb, n_ctx=n_ctx)
    return pl.pallas_call(
        kern,
        grid=(m // tm,),
        in_specs=[
            pl.BlockSpec((tm, d), lambda i: (i, 0)),
            pl.BlockSpec((1, d), lambda i: (0, 0)),
            pl.BlockSpec((1, N_MOD, d), lambda i: (i // tpb, 0, 0)),
            pl.BlockSpec((1, N_MOD, d), lambda i: (0, 0, 0)),
        ],
        out_specs=pl.BlockSpec((tm, d), lambda i: (i, 0)),
        out_shape=jax.ShapeDtypeStruct((m, d), BF16),
        compiler_params=_cparams("parallel"),
    )(x2, g.reshape(1, d), mod_l, mod_c)


def _mm_kernel(a_ref, w_ref, o_ref):
    o_ref[...] = jnp.dot(a_ref[...], w_ref[...], preferred_element_type=F32).astype(o_ref.dtype)


def _matmul(a, w, out_dtype, tm, tn):
    m, k = a.shape
    n = w.shape[1]
    return pl.pallas_call(
        _mm_kernel,
        grid=(m // tm, n // tn),
        in_specs=[
            pl.BlockSpec((tm, k), lambda i, j: (i, 0)),
            pl.BlockSpec((k, tn), lambda i, j: (0, j)),
        ],
        out_specs=pl.BlockSpec((tm, tn), lambda i, j: (i, j)),
        out_shape=jax.ShapeDtypeStruct((m, n), out_dtype),
        compiler_params=_cparams("parallel", "parallel"),
    )(a, w)


def _s5_weights(lam_re, lam_im, log_dt, b_re, b_im, c_re, c_im, d_skip):
    n_dir, g_n, p_n = lam_re.shape
    h_n = b_re.shape[-1]
    cn = S5_CHUNK
    lr, li = lam_re.astype(F32), lam_im.astype(F32)
    dt = jnp.exp(log_dt.astype(F32))[..., None]
    mag = jnp.exp(lr * dt)
    ar = mag * jnp.cos(li * dt)
    ai = mag * jnp.sin(li * dt)
    den = lr * lr + li * li
    zr = ((ar - 1.0) * lr + ai * li) / den
    zi = (ai * lr - (ar - 1.0) * li) / den
    bbr = zr[..., None] * b_re - zi[..., None] * b_im
    bbi = zr[..., None] * b_im + zi[..., None] * b_re
    j = jnp.arange(cn + 1, dtype=F32)[:, None, None, None]
    pmag = jnp.exp(lr[None] * dt[None] * j)
    pang = li[None] * dt[None] * j
    pr = pmag * jnp.cos(pang)
    pi = pmag * jnp.sin(pang)
    wr = pr[..., None] * bbr[None] - pi[..., None] * bbi[None]
    wi = pr[..., None] * bbi[None] + pi[..., None] * bbr[None]
    kj = (jnp.einsum('dghp,jdgpk->jdghk', c_re, wr, precision=HIGHEST)
          - jnp.einsum('dghp,jdgpk->jdghk', c_im, wi, precision=HIGHEST))
    s_idx = jnp.arange(cn)[:, None]
    t_idx = jnp.arange(cn)[None, :]
    lag = t_idx - s_idx
    kf = kj[:cn, 0][jnp.clip(lag, 0, cn - 1)]
    kb = kj[:cn, 1][jnp.clip(-lag, 0, cn - 1)]
    kf = jnp.where((lag >= 0)[:, :, None, None, None], kf, 0.0)
    kb = jnp.where((lag <= 0)[:, :, None, None, None], kb, 0.0)
    dsk = d_skip.astype(F32).reshape(g_n, h_n)
    eye_h = jnp.eye(h_n, dtype=F32)
    diag = (lag == 0)[:, :, None, None, None] * (dsk[None, None, :, :, None] * eye_h[None, None, None])
    m_full = kf + kb + diag
    m_mat = m_full.transpose(2, 0, 4, 1, 3).reshape(g_n, cn * h_n, cn * h_n)
    wf_r = wr[:cn, 0][::-1]
    wf_i = wi[:cn, 0][::-1]
    wb_r = wr[:cn, 1]
    wb_i = wi[:cn, 1]
    w_in = jnp.stack([wf_r, wf_i, wb_r, wb_i], axis=0)
    w_in = w_in.transpose(2, 1, 4, 0, 3).reshape(g_n, cn * h_n, 4 * p_n)
    def out_map(d, powers):
        prd, pid = pr[powers, d], pi[powers, d]
        w_re = c_re[d][None] * prd[:, :, None, :] - c_im[d][None] * pid[:, :, None, :]
        w_im = c_re[d][None] * pid[:, :, None, :] + c_im[d][None] * prd[:, :, None, :]
        return w_re, -w_im
    t_arr = jnp.arange(cn)
    of_r, of_i = out_map(0, t_arr + 1)
    ob_r, ob_i = out_map(1, cn - t_arr)
    w_out = jnp.stack([of_r, of_i, ob_r, ob_i], axis=0)
    w_out = w_out.transpose(2, 0, 4, 1, 3).reshape(g_n, 4 * p_n, cn * h_n)
    a1 = jnp.concatenate([pr[cn, 0], pr[cn, 0], pr[cn, 1], pr[cn, 1]], axis=-1)
    a2 = jnp.concatenate([-pi[cn, 0], pi[cn, 0], -pi[cn, 1], pi[cn, 1]], axis=-1)
    a_mat = jnp.stack([a1, a2], axis=1)
    w1 = jnp.concatenate([m_mat, w_in], axis=-1).astype(BF16)
    return w1, w_out.astype(BF16), a_mat


def _s5_kernel(u_ref, w1_ref, wo_ref, a_ref, y_ref, x_scr, hp_scr, *, n_chunks, n_ctx_chunks, bsz):
    width = 4 * S5_STATE
    half = 2 * S5_STATE
    z = jnp.dot(u_ref[0], w1_ref[0], preferred_element_type=F32)
    y_ref[0] = z[:, :width]
    x_scr[...] = z[:, width:]
    a1 = jnp.broadcast_to(a_ref[0, 0:1, :], (bsz, width))
    a2 = jnp.broadcast_to(a_ref[0, 1:2, :], (bsz, width))
    a1f, a1b = a1[:, :half], a1[:, half:]
    a2f, a2b = a2[:, :half], a2[:, half:]

    def body(i, carry):
        hf, hb = carry
        nb = jnp.where(i < n_ctx_chunks, n_ctx_chunks - 1 - i, n_chunks - 1 - (i - n_ctx_chunks))
        rf = pl.multiple_of(i * bsz, bsz)
        rb = pl.multiple_of(nb * bsz, bsz)
        hp_scr[pl.ds(rf, bsz), 0:half] = hf
        hp_scr[pl.ds(rb, bsz), half:width] = hb
        hf = a1f * hf + a2f * pltpu.roll(hf, S5_STATE, 1) + x_scr[pl.ds(rf, bsz), 0:half]
        hb = a1b * hb + a2b * pltpu.roll(hb, S5_STATE, 1) + x_scr[pl.ds(rb, bsz), half:width]
        return hf, hb

    zero = jnp.zeros((bsz, half), F32)
    lax.fori_loop(0, n_chunks, body, (zero, zero))
    y_ref[0] += jnp.dot(hp_scr[...].astype(BF16), wo_ref[0], preferred_element_type=F32)


def _s5_mix(u, w1, w_out, a_mat, n_ctx):
    bsz, t_len, width = u.shape
    g_n = width // S5_GROUP
    cn = S5_CHUNK
    n_chunks = t_len // cn
    rows = n_chunks * bsz
    blk = cn * S5_GROUP
    ug = u.reshape(bsz, n_chunks, cn, g_n, S5_GROUP).transpose(3, 1, 0, 2, 4).reshape(g_n, rows, blk)
    kern = functools.partial(_s5_kernel, n_chunks=n_chunks, n_ctx_chunks=n_ctx // cn, bsz=bsz)
    y = pl.pallas_call(
        kern,
        grid=(g_n,),
        in_specs=[
            pl.BlockSpec((1, rows, blk), lambda g: (g, 0, 0)),
            pl.BlockSpec((1, blk, blk + 4 * S5_STATE), lambda g: (g, 0, 0)),
            pl.BlockSpec((1, 4 * S5_STATE, blk), lambda g: (g, 0, 0)),
            pl.BlockSpec((1, 2, 4 * S5_STATE), lambda g: (g, 0, 0)),
        ],
        out_specs=pl.BlockSpec((1, rows, blk), lambda g: (g, 0, 0)),
        out_shape=jax.ShapeDtypeStruct((g_n, rows, blk), F32),
        scratch_shapes=[pltpu.VMEM((rows, 4 * S5_STATE), F32), pltpu.VMEM((rows, 4 * S5_STATE), F32)],
        compiler_params=_cparams("parallel"),
    )(ug, w1, w_out, a_mat)
    y = y.reshape(g_n, n_chunks, bsz, cn, S5_GROUP).transpose(2, 1, 3, 0, 4)
    return y.reshape(bsz, t_len, width)


def _head0_lanes(shape):
    return lax.broadcasted_iota(jnp.int32, shape, len(shape) - 1) < HEAD_DIM


def _stack_heads(q):
    m0 = _head0_lanes(q.shape)
    return jnp.concatenate([jnp.where(m0, q, 0.0), jnp.where(m0, 0.0, q)], axis=0)


def _unstack_heads(o2):
    c = o2.shape[0] // 2
    return jnp.where(_head0_lanes((c, PAIR)), o2[:c], o2[c:])


def _pair_block_mask():
    r = lax.broadcasted_iota(jnp.int32, (PAIR, PAIR), 0) < HEAD_DIM
    c = lax.broadcasted_iota(jnp.int32, (PAIR, PAIR), 1) < HEAD_DIM
    return r == c


def _dot_nt(a, b):
    return lax.dot_general(a, b, (((1,), (1,)), ((), ())), preferred_element_type=F32)


def _dot_tn(a, b):
    return lax.dot_general(a, b, (((0,), (0,)), ((), ())), preferred_element_type=F32)


def _bwd_block(j, n_ctx_blocks, n_blocks):
    return jnp.where(j < n_ctx_blocks, n_ctx_blocks - 1 - j, n_blocks - 1 - (j - n_ctx_blocks))


def _hg_chunk(q, z, v, lb, ht, reverse):
    c = HG_CHUNK
    hh = HG_HALF
    f = lb + (1.0 - lb) * jax.nn.sigmoid(z)
    logf = jnp.log(f)
    kk = 1.0 - f
    ri = lax.broadcasted_iota(jnp.int32, (c, c), 0)
    ci = lax.broadcasted_iota(jnp.int32, (c, c), 1)
    tri = jnp.where((ci >= ri) if reverse else (ci <= ri), 1.0, 0.0).astype(F32)
    b = jnp.dot(tri, logf, precision=HIGHEST, preferred_element_type=F32)
    if reverse:
        first, second = slice(hh, c), slice(0, hh)
        r = b[hh:hh + 1]
        bend = b[0:1]
    else:
        first, second = slice(0, hh), slice(hh, c)
        r = b[hh - 1:hh]
        bend = b[c - 1:c]
    b1, b2 = b[first], b[second]
    qd1 = q[first] * jnp.exp(b1)
    qd2 = q[second] * jnp.exp(b2 - r)
    k1d = kk[first] * jnp.exp(-b1)
    k2 = kk * jnp.exp(r - b)
    s1 = _dot_nt(_stack_heads(qd1).astype(BF16), k1d.astype(BF16))
    s2 = _dot_nt(_stack_heads(qd2).astype(BF16), k2.astype(BF16))
    t1 = lax.broadcasted_iota(jnp.int32, (2 * hh, hh), 0) % hh
    c1 = lax.broadcasted_iota(jnp.int32, (2 * hh, hh), 1)
    t2 = lax.broadcasted_iota(jnp.int32, (2 * hh, c), 0) % hh
    c2 = lax.broadcasted_iota(jnp.int32, (2 * hh, c), 1)
    if reverse:
        s1 = jnp.where(c1 >= t1, s1, 0.0)
        s2 = jnp.where(c2 >= t2, s2, 0.0)
    else:
        s1 = jnp.where(c1 <= t1, s1, 0.0)
        s2 = jnp.where(c2 <= t2 + hh, s2, 0.0)
    vb = v.astype(BF16)
    o1 = _unstack_heads(jnp.dot(s1.astype(BF16), vb[first], preferred_element_type=F32))
    o2 = _unstack_heads(jnp.dot(s2.astype(BF16), vb, preferred_element_type=F32))
    o = jnp.concatenate([o2, o1] if reverse else [o1, o2], axis=0)
    o = o + _dot_nt((q * jnp.exp(b)).astype(BF16), ht.astype(BF16))
    ke = kk * jnp.exp(bend - b)
    upd = _dot_tn(vb, ke.astype(BF16))
    ht_new = ht * jnp.exp(bend) + jnp.where(_pair_block_mask(), upd, 0.0)
    return o, ht_new


def _hg_kernel(qf_ref, zf_ref, vf_ref, qb_ref, zb_ref, vb_ref, lb_ref, of_ref, ob_ref, hf_scr, hb_scr, *, n_pairs):
    @pl.when(pl.program_id(1) == 0)
    def _():
        hf_scr[...] = jnp.zeros_like(hf_scr)
        hb_scr[...] = jnp.zeros_like(hb_scr)

    for p in range(n_pairs):
        sl = slice(p * PAIR, (p + 1) * PAIR)
        o, hn = _hg_chunk(qf_ref[0, :, sl], zf_ref[0, :, sl], vf_ref[0, :, sl], lb_ref[0:1, sl], hf_scr[p], False)
        of_ref[0, :, sl] = o
        hf_scr[p] = hn
        o, hn = _hg_chunk(qb_ref[0, :, sl], zb_ref[0, :, sl], vb_ref[0, :, sl], lb_ref[1:2, sl], hb_scr[p], True)
        ob_ref[0, :, sl] = o
        hb_scr[p] = hn


def _hg_mix(hg, lb, n_ctx):
    bsz, t_len, total = hg.shape
    w = total // 5
    n_pairs = w // PAIR
    c = HG_CHUNK
    nc, ncc = t_len // c, n_ctx // c
    bw = functools.partial(_bwd_block, n_ctx_blocks=ncc, n_blocks=nc)

    def fspec(col):
        return pl.BlockSpec((1, c, w), lambda b, j: (b, j, col))

    def bspec(col):
        return pl.BlockSpec((1, c, w), lambda b, j: (b, bw(j), col))

    kern = functools.partial(_hg_kernel, n_pairs=n_pairs)
    return pl.pallas_call(
        kern,
        grid=(bsz, nc),
        in_specs=[fspec(0), fspec(1), fspec(3), bspec(0), bspec(2), bspec(3),
                  pl.BlockSpec((2, w), lambda b, j: (0, 0))],
        out_specs=[pl.BlockSpec((1, c, w), lambda b, j: (b, j, 0)),
                   pl.BlockSpec((1, c, w), lambda b, j: (b, bw(j), 0))],
        out_shape=[jax.ShapeDtypeStruct((bsz, t_len, w), F32)] * 2,
        scratch_shapes=[pltpu.VMEM((n_pairs, PAIR, PAIR), F32)] * 2,
        compiler_params=_cparams("parallel", "arbitrary"),
    )(hg, hg, hg, hg, hg, hg, lb)


def _rope_rows(x, cos, sin):
    hd2 = HEAD_DIM // 2
    lane = lax.broadcasted_iota(jnp.int32, x.shape, 1) % HEAD_DIM
    swapped = jnp.where(lane < hd2, pltpu.roll(x, PAIR - hd2, 1), pltpu.roll(x, hd2, 1))
    return x * cos + swapped * sin


def _ret_kernel(qf_ref, kf_ref, vf_ref, cf_ref, sf_ref, qb_ref, kb_ref, vb_ref, cb_ref, sb_ref,
                d_ref, gqf_ref, gkf_ref, gqb_ref, gkb_ref, gh_ref, of_ref, ob_ref, hf_scr, hb_scr, *, n_pairs):
    @pl.when(pl.program_id(1) == 0)
    def _():
        hf_scr[...] = jnp.zeros_like(hf_scr)
        hb_scr[...] = jnp.zeros_like(hb_scr)

    kscale = HEAD_DIM ** -0.5
    bmask = _pair_block_mask()
    for p in range(n_pairs):
        sl = slice(p * PAIR, (p + 1) * PAIR)
        q = _rope_rows(qf_ref[0, :, sl], cf_ref[...], sf_ref[...])
        k = _rope_rows(kf_ref[0, :, sl] * kscale, cf_ref[...], sf_ref[...])
        vb16 = vf_ref[0, :, sl].astype(BF16)
        s = _dot_nt(_stack_heads(q).astype(BF16), k.astype(BF16)) * d_ref[p]
        o = _unstack_heads(jnp.dot(s.astype(BF16), vb16, preferred_element_type=F32))
        ht = hf_scr[p]
        o = o + _dot_nt((q * gqf_ref[:, sl]).astype(BF16), ht.astype(BF16))
        upd = _dot_tn(vb16, (k * gkf_ref[:, sl]).astype(BF16))
        hf_scr[p] = ht * gh_ref[p] + jnp.where(bmask, upd, 0.0)
        of_ref[0, :, sl] = o

        q = _rope_rows(qb_ref[0, :, sl], cb_ref[...], sb_ref[...])
        k = _rope_rows(kb_ref[0, :, sl] * kscale, cb_ref[...], sb_ref[...])
        vb16 = vb_ref[0, :, sl].astype(BF16)
        ht = hb_scr[p]
        ob_ref[0, :, sl] = _dot_nt((q * gqb_ref[:, sl]).astype(BF16), ht.astype(BF16))
        upd = _dot_tn(vb16, (k * gkb_ref[:, sl]).astype(BF16))
        hb_scr[p] = ht * gh_ref[p] + jnp.where(bmask, upd, 0.0)


def _ret_constants(n_heads, t_len, n_ctx):
    c = RET_BLOCK
    log_gamma = jnp.log(1.0 - 2.0 ** (-5.0 - jnp.arange(n_heads, dtype=F32)))
    t = jnp.arange(c, dtype=F32)
    dist = jnp.abs(t[:, None] - t[None, :])
    dmat = jnp.exp(log_gamma[:, None, None] * dist[None])
    dmat = dmat.reshape(n_heads // 2, 2 * c, c)
    lg_lane = jnp.repeat(log_gamma, HEAD_DIM)[None, :]
    gqf = jnp.exp(lg_lane * (t[:, None] + 1.0))
    gkf = jnp.exp(lg_lane * (c - 1.0 - t[:, None]))
    gqb = jnp.exp(lg_lane * (c - t[:, None]))
    gkb = jnp.exp(lg_lane * t[:, None])
    gh = jnp.exp(lg_lane * float(c)).reshape(n_heads // 2, 1, PAIR)
    gh = jnp.broadcast_to(gh, (n_heads // 2, PAIR, PAIR))
    n_lat = t_len - n_ctx
    rows = n_lat // GRID_W
    row = jnp.repeat(jnp.arange(rows, dtype=F32), GRID_W)
    col = jnp.broadcast_to(jnp.arange(GRID_W, dtype=F32)[None, :], (rows, GRID_W)).reshape(-1)
    n_freq = HEAD_DIM // 4
    inv_freq = ROPE_BASE ** (-jnp.arange(n_freq, dtype=F32) / n_freq)
    ang = jnp.concatenate([row[:, None] * inv_freq, col[:, None] * inv_freq], axis=-1)
    cos_l, sin_l = jnp.cos(ang), jnp.sin(ang)
    cos_h = jnp.concatenate([cos_l, cos_l], axis=-1)
    sin_h = jnp.concatenate([-sin_l, sin_l], axis=-1)
    cos_t = jnp.concatenate([jnp.ones((n_ctx, HEAD_DIM), F32), cos_h], axis=0)
    sin_t = jnp.concatenate([jnp.zeros((n_ctx, HEAD_DIM), F32), sin_h], axis=0)
    cos_t = jnp.concatenate([cos_t, cos_t], axis=-1)
    sin_t = jnp.concatenate([sin_t, sin_t], axis=-1)
    return dmat, gqf, gkf, gqb, gkb, gh, cos_t, sin_t


def _ret_mix(ret, consts, n_ctx):
    dmat, gqf, gkf, gqb, gkb, gh, cos_t, sin_t = consts
    bsz, t_len, total = ret.shape
    w = total // 4
    n_pairs = w // PAIR
    c = RET_BLOCK
    nb, ncb = t_len // c, n_ctx // c
    bw = functools.partial(_bwd_block, n_ctx_blocks=ncb, n_blocks=nb)

    def fspec(col):
        return pl.BlockSpec((1, c, w), lambda b, j: (b, j, col))

    def bspec(col):
        return pl.BlockSpec((1, c, w), lambda b, j: (b, bw(j), col))

    def const(a):
        nd = a.ndim
        return pl.BlockSpec(a.shape, lambda b, j: (0,) * nd)

    rope_f = pl.BlockSpec((c, PAIR), lambda b, j: (j, 0))
    rope_b = pl.BlockSpec((c, PAIR), lambda b, j: (bw(j), 0))
    kern = functools.partial(_ret_kernel, n_pairs=n_pairs)
    return pl.pallas_call(
        kern,
        grid=(bsz, nb),
        in_specs=[fspec(0), fspec(1), fspec(2), rope_f, rope_f, bspec(0), bspec(1), bspec(2), rope_b, rope_b,
                  const(dmat), const(gqf), const(gkf), const(gqb), const(gkb), const(gh)],
        out_specs=[pl.BlockSpec((1, c, w), lambda b, j: (b, j, 0)),
                   pl.BlockSpec((1, c, w), lambda b, j: (b, bw(j), 0))],
        out_shape=[jax.ShapeDtypeStruct((bsz, t_len, w), F32)] * 2,
        scratch_shapes=[pltpu.VMEM((n_pairs, PAIR, PAIR), F32)] * 2,
        compiler_params=_cparams("parallel", "arbitrary"),
    )(ret, ret, ret, cos_t, sin_t, ret, ret, ret, cos_t, sin_t, dmat, gqf, gkf, gqb, gkb, gh)


def _head_norm_gate(o, g, ones_bd):
    sq = o * o
    hi = sq.astype(BF16)
    lo = (sq - hi.astype(F32)).astype(BF16)
    ss = jnp.dot(hi, ones_bd, preferred_element_type=F32) + jnp.dot(lo, ones_bd, preferred_element_type=F32)
    return o * lax.rsqrt(ss * (1.0 / HEAD_DIM) + EPS) * jax.nn.silu(g)


def _route(lg):
    lane = lax.broadcasted_iota(jnp.int32, lg.shape, 1).astype(F32)
    big = float(1 << 20)
    neg = -jnp.inf
    gmask = lane < N_GROUPS
    lgm = jnp.where(gmask, lg, neg)
    gmax = jnp.max(lgm, axis=1, keepdims=True)
    gsum = jnp.sum(jnp.where(gmask, jnp.exp(lgm - gmax), 0.0), axis=1, keepdims=True)
    gp = 1.0 / gsum
    gi = jnp.min(jnp.where(gmask & (lgm == gmax), lane, big), axis=1, keepdims=True)
    lo_lane = N_GROUPS + EXPERTS_PER_GROUP * gi
    emask = (lane >= lo_lane) & (lane < lo_lane + EXPERTS_PER_GROUP)
    l1 = jnp.max(jnp.where(emask, lg, neg), axis=1, keepdims=True)
    i1 = jnp.min(jnp.where(emask & (lg == l1), lane, big), axis=1, keepdims=True)
    mask2 = emask & (lane != i1)
    l2 = jnp.max(jnp.where(mask2, lg, neg), axis=1, keepdims=True)
    i2 = jnp.min(jnp.where(mask2 & (lg == l2), lane, big), axis=1, keepdims=True)
    e21 = jnp.exp(l2 - l1)
    w1 = gp / (1.0 + e21)
    w2 = gp * e21 / (1.0 + e21)
    e1 = i1 - N_GROUPS
    e2 = i2 - N_GROUPS
    return jnp.where(lane == 0, w1, jnp.where(lane == 1, w2, jnp.where(lane == 2, e1, jnp.where(lane == 3, e2, 0.0))))


def _merge_kernel(x_ref, ys_ref, hof_ref, hob_ref, hgg_ref, rof_ref, rob_ref, rgg_ref, gz_ref,
                  ml_ref, mc_ref, g2_ref, wglu_ref, wbs_ref, wbh_ref, wbr_ref, wout_ref, ones_ref,
                  wr_ref, br_ref, xo_ref, h2_ref, rt_ref, *, tm, tiles_per_b, n_ctx):
    d = x_ref.shape[1]
    is_ctx = _is_ctx_rows(tm, tiles_per_b, n_ctx)
    y = jax.nn.gelu(ys_ref[...], approximate=True)
    y = y * jax.nn.sigmoid(jnp.dot(y.astype(BF16), wglu_ref[...], preferred_element_type=F32))
    t_s5 = jnp.dot(y.astype(BF16), wbs_ref[...], preferred_element_type=F32)
    yh = _head_norm_gate(hof_ref[...] + hob_ref[...], hgg_ref[...], ones_ref[...])
    t_hg = jnp.dot(yh.astype(BF16), wbh_ref[...], preferred_element_type=F32)
    yr = _head_norm_gate(rof_ref[...] + rob_ref[...], rgg_ref[...], ones_ref[...])
    t_ret = jnp.dot(yr.astype(BF16), wbr_ref[...], preferred_element_type=F32)
    merged = (jax.nn.sigmoid(gz_ref[:, 0:d]) * t_s5 + jax.nn.sigmoid(gz_ref[:, d:2 * d]) * t_hg
              + jax.nn.sigmoid(gz_ref[:, 2 * d:3 * d]) * t_ret)
    mix = jnp.dot(merged.astype(BF16), wout_ref[...], preferred_element_type=F32)
    xn = x_ref[...] + _mod_rows(ml_ref, mc_ref, 2, is_ctx) * mix
    xo_ref[...] = xn
    h2 = _rmsnorm_rows(xn, g2_ref[...])
    h2 = h2 * (1.0 + _mod_rows(ml_ref, mc_ref, 4, is_ctx)) + _mod_rows(ml_ref, mc_ref, 3, is_ctx)
    h2_ref[...] = h2.astype(BF16)
    lg = jnp.dot(h2, wr_ref[...], precision=HIGHEST, preferred_element_type=F32) + br_ref[...]
    rt_ref[...] = _route(lg)


def _merge(x2, ys5, hg_of, hg_ob, hg2, ret_of, ret_ob, ret2, gz, mod_l, mod_c, g2, wts, t_len, n_ctx):
    m, d = x2.shape
    w_s5 = ys5.shape[1]
    w_h = hg_of.shape[1]
    tm = _row_tile(t_len)
    tpb = t_len // tm
    wglu, wbs, wbh, wbr, wout, ones_bd, wr, br = wts

    def rows(width, col=0):
        return pl.BlockSpec((tm, width), lambda i: (i, col))

    def const(a):
        nd = a.ndim
        return pl.BlockSpec(a.shape, lambda i: (0,) * nd)

    kern = functools.partial(_merge_kernel, tm=tm, tiles_per_b=tpb, n_ctx=n_ctx)
    return pl.pallas_call(
        kern,
        grid=(m // tm,),
        in_specs=[rows(d), rows(w_s5), rows(w_h), rows(w_h), rows(w_h, 4), rows(w_h), rows(w_h), rows(w_h, 3),
                  rows(3 * d),
                  pl.BlockSpec((1, N_MOD, d), lambda i: (i // tpb, 0, 0)),
                  pl.BlockSpec((1, N_MOD, d), lambda i: (0, 0, 0)),
                  const(g2), const(wglu), const(wbs), const(wbh), const(wbr), const(wout), const(ones_bd),
                  const(wr), const(br)],
        out_specs=[rows(d), rows(d), rows(LANES)],
        out_shape=[jax.ShapeDtypeStruct((m, d), F32), jax.ShapeDtypeStruct((m, d), BF16),
                   jax.ShapeDtypeStruct((m, LANES), F32)],
        compiler_params=_cparams("parallel"),
    )(x2, ys5, hg_of, hg_ob, hg2, ret_of, ret_ob, ret2, gz, mod_l, mod_c, g2, wglu, wbs, wbh, wbr, wout,
      ones_bd, wr, br)


def _ffn_kernel(be_ref, nu_ref, x_ref, wg_ref, wu_ref, wd_ref, o_ref):
    i = pl.program_id(0)

    @pl.when(i < nu_ref[0])
    def _():
        xb = x_ref[...]
        hid = (jax.nn.silu(jnp.dot(xb, wg_ref[0], preferred_element_type=F32))
               * jnp.dot(xb, wu_ref[0], preferred_element_type=F32))
        o_ref[...] = jnp.dot(hid.astype(BF16), wd_ref[0], preferred_element_type=F32)

    @pl.when(i >= nu_ref[0])
    def _():
        o_ref[...] = jnp.zeros_like(o_ref)


def _expert_ffn(blk_e, n_used, xg, wg, wu, wd):
    n_slots, d = xg.shape
    f = wg.shape[2]
    bm = MOE_BLOCK
    grid_spec = pltpu.PrefetchScalarGridSpec(
        num_scalar_prefetch=2,
        grid=(n_slots // bm,),
        in_specs=[
            pl.BlockSpec((bm, d), lambda i, be, nu: (i, 0)),
            pl.BlockSpec((1, d, f), lambda i, be, nu: (be[i], 0, 0)),
            pl.BlockSpec((1, d, f), lambda i, be, nu: (be[i], 0, 0)),
            pl.BlockSpec((1, f, d), lambda i, be, nu: (be[i], 0, 0)),
        ],
        out_specs=pl.BlockSpec((bm, d), lambda i, be, nu: (i, 0)),
    )
    return pl.pallas_call(
        _ffn_kernel,
        grid_spec=grid_spec,
        out_shape=jax.ShapeDtypeStruct((n_slots, d), F32),
        compiler_params=_cparams("arbitrary"),
    )(blk_e, n_used, xg, wg, wu, wd)


def _combine_kernel(x_ref, y0_ref, y1_ref, rt_ref, ml_ref, mc_ref, o_ref, *, tm, tiles_per_b, n_ctx):
    is_ctx = _is_ctx_rows(tm, tiles_per_b, n_ctx)
    rt = rt_ref[...]
    y = rt[:, 0:1] * y0_ref[...] + rt[:, 1:2] * y1_ref[...]
    o_ref[...] = x_ref[...] + _mod_rows(ml_ref, mc_ref, 5, is_ctx) * y


def _combine(x2, y0, y1, route, mod_l, mod_c, t_len, n_ctx):
    m, d = x2.shape
    tm = _row_tile(t_len)
    tpb = t_len // tm
    kern = functools.partial(_combine_kernel, tm=tm, tiles_per_b=tpb, n_ctx=n_ctx)
    return pl.pallas_call(
        kern,
        grid=(m // tm,),
        in_specs=[pl.BlockSpec((tm, d), lambda i: (i, 0))] * 3
        + [pl.BlockSpec((tm, LANES), lambda i: (i, 0)),
           pl.BlockSpec((1, N_MOD, d), lambda i: (i // tpb, 0, 0)),
           pl.BlockSpec((1, N_MOD, d), lambda i: (0, 0, 0))],
        out_specs=pl.BlockSpec((tm, d), lambda i: (i, 0)),
        out_shape=jax.ShapeDtypeStruct((m, d), F32),
        compiler_params=_cparams("parallel"),
    )(x2, y0, y1, route, mod_l, mod_c)


def _final_norm_kernel(x_ref, g_ref, o_ref):
    o_ref[0] = _rmsnorm_rows(x_ref[0], g_ref[...])


def _final_norm(x3, g, n_ctx):
    bsz, t_len, d = x3.shape
    n_lat = t_len - n_ctx
    tm = n_ctx
    off = n_ctx // tm
    return pl.pallas_call(
        _final_norm_kernel,
        grid=(bsz, n_lat // tm),
        in_specs=[pl.BlockSpec((1, tm, d), lambda b, j: (b, j + off, 0)),
                  pl.BlockSpec((1, d), lambda b, j: (0, 0))],
        out_specs=pl.BlockSpec((1, tm, d), lambda b, j: (b, j, 0)),
        out_shape=jax.ShapeDtypeStruct((bsz, n_lat, d), F32),
        compiler_params=_cparams("parallel", "parallel"),
    )(x3, g.reshape(1, d))


def _dispatch_indices(eid, n_tok):
    a = n_tok * TOP_K
    bm = MOE_BLOCK
    flat_e = eid.reshape(a)
    order = jnp.argsort(flat_e)
    se = flat_e[order]
    counts = jnp.zeros((N_EXPERTS,), jnp.int32).at[flat_e].add(1)
    padded = (counts + bm - 1) // bm * bm
    pad_end = jnp.cumsum(padded)
    pad_start = pad_end - padded
    start = jnp.cumsum(counts) - counts
    dest = pad_start[se] + jnp.arange(a, dtype=jnp.int32) - start[se]
    n_blocks = -(-a // bm) + N_EXPERTS
    slot_tok = jnp.zeros((n_blocks * bm,), jnp.int32).at[dest].set((order // TOP_K).astype(jnp.int32))
    pos = jnp.zeros((a,), jnp.int32).at[order].set(dest.astype(jnp.int32)).reshape(n_tok, TOP_K)
    blk_e = jnp.minimum(jnp.searchsorted(pad_end, jnp.arange(n_blocks) * bm, side='right'), N_EXPERTS - 1)
    n_used = (pad_end[-1] // bm).astype(jnp.int32).reshape(1)
    return slot_tok, pos, blk_e.astype(jnp.int32), n_used


def _moe(h2, route, wg, wu, wd):
    n_tok = h2.shape[0]
    eid = route[:, 2:4].astype(jnp.int32)
    slot_tok, pos, blk_e, n_used = _dispatch_indices(eid, n_tok)
    xg = jnp.take(h2, slot_tok, axis=0)
    yb = _expert_ffn(blk_e, n_used, xg, wg, wu, wd)
    return jnp.take(yb, pos[:, 0], axis=0), jnp.take(yb, pos[:, 1], axis=0)


def _mm_tiles(t_len, n):
    tm = _row_tile(t_len)
    for tn in (1536, 1024, 768, 640, 512, 384, 256, 128):
        if n % tn == 0:
            return tm, tn
    return tm, n


def kernel(x, c, ctx, c_ctx, w_ada, b_ada, norm1_g, norm2_g, w_in, s5_lam_re, s5_lam_im, s5_log_dt, s5_b_re, s5_b_im, s5_c_re, s5_c_im, s5_d, s5_w_glu, hgrn_lb_raw, w_branch_s5, w_branch_hgrn, w_branch_ret, w_out, moe_w_group, moe_b_group, moe_w_expert, moe_b_expert, moe_w_gate, moe_w_up, moe_w_down, final_norm_g):
    bsz, n_lat, d = x.shape
    n_ctx = ctx.shape[1]
    t_len = n_ctx + n_lat
    depth = w_ada.shape[0]
    m = bsz * t_len
    w_s5 = s5_d.shape[1]
    w_hg = w_branch_hgrn.shape[1]
    w_ret = w_branch_ret.shape[1]
    n_heads = w_ret // HEAD_DIM

    pad_rows = (-(bsz + 1)) % SUBLANES
    cc = jnp.concatenate([c, c_ctx[None, :], jnp.zeros((pad_rows, d), F32)], axis=0)
    mods = _ada_tables(cc, w_ada, b_ada).reshape(depth, bsz + 1 + pad_rows, N_MOD, d)

    p_lb = jax.nn.softmax(hgrn_lb_raw.astype(F32), axis=0)
    lower_bounds = jnp.cumsum(p_lb, axis=0) - p_lb[0:1]
    ret_consts = _ret_constants(n_heads, t_len, n_ctx)
    ones_bd = jnp.kron(jnp.eye(w_hg // HEAD_DIM, dtype=F32), jnp.ones((HEAD_DIM, HEAD_DIM), F32)).astype(BF16)

    xs = jnp.concatenate([ctx, x], axis=1).reshape(m, d)
    off_hg = w_s5
    off_ret = off_hg + 5 * w_hg
    off_gz = off_ret + 4 * w_ret
    for l in range(depth):
        mod_l = mods[l, :bsz]
        mod_c = mods[l, bsz:bsz + 1]
        h = _norm_mod(xs, norm1_g[l], mod_l, mod_c, t_len, n_ctx)
        wl = w_in[l].astype(BF16)
        u = _matmul(h, wl[:, :off_hg], BF16, *_mm_tiles(t_len, w_s5))
        hg = _matmul(h, wl[:, off_hg:off_ret], F32, *_mm_tiles(t_len, 5 * w_hg))
        ret = _matmul(h, wl[:, off_ret:off_gz], F32, *_mm_tiles(t_len, 4 * w_ret))
        gz = _matmul(h, wl[:, off_gz:], F32, *_mm_tiles(t_len, 3 * d))

        s5w = _s5_weights(s5_lam_re[l], s5_lam_im[l], s5_log_dt[l], s5_b_re[l], s5_b_im[l],
                          s5_c_re[l], s5_c_im[l], s5_d[l])
        ys5 = _s5_mix(u.reshape(bsz, t_len, w_s5), *s5w, n_ctx).reshape(m, w_s5)
        hg_of, hg_ob = _hg_mix(hg.reshape(bsz, t_len, 5 * w_hg), lower_bounds[l], n_ctx)
        ret_of, ret_ob = _ret_mix(ret.reshape(bsz, t_len, 4 * w_ret), ret_consts, n_ctx)

        wr = jnp.concatenate([moe_w_group[l], moe_w_expert[l],
                              jnp.zeros((d, LANES - N_GROUPS - N_EXPERTS), F32)], axis=1)
        br = jnp.concatenate([moe_b_group[l], moe_b_expert[l],
                              jnp.zeros((LANES - N_GROUPS - N_EXPERTS,), F32)])[None, :]
        wts = (s5_w_glu[l].astype(BF16), w_branch_s5[l].astype(BF16), w_branch_hgrn[l].astype(BF16),
               w_branch_ret[l].astype(BF16), w_out[l].astype(BF16), ones_bd, wr, br)
        xs, h2, route = _merge(xs, ys5, hg_of.reshape(m, w_hg), hg_ob.reshape(m, w_hg), hg,
                               ret_of.reshape(m, w_ret), ret_ob.reshape(m, w_ret), ret, gz,
                               mod_l, mod_c, norm2_g[l].reshape(1, d), wts, t_len, n_ctx)
        y0, y1 = _moe(h2, route, moe_w_gate[l].astype(BF16), moe_w_up[l].astype(BF16),
                      moe_w_down[l].astype(BF16))
        xs = _combine(xs, y0, y1, route, mod_l, mod_c, t_len, n_ctx)
    return _final_norm(xs.reshape(bsz, t_len, d), final_norm_g, n_ctx)
```

```python
import functools

import jax
import jax.numpy as jnp
import numpy as np
from jax import lax
from jax.experimental import pallas as pl
from jax.experimental.pallas import tpu as pltpu

F32 = jnp.float32
BF16 = jnp.bfloat16
HIGHEST = lax.Precision.HIGHEST

LANES = 128
SUBLANES = 8
VMEM_LIMIT = 56 * 1024 * 1024

EPS = 1e-6
N_MOD = 6
GRID_W = 64
HEAD_DIM = 64
PAIR = 2 * HEAD_DIM
S5_GROUP = 16
S5_STATE = 64
S5_CHUNK = 16
HG_CHUNK = 64
HG_HALF = HG_CHUNK // 2
HG_BLOCK = 256
RET_BLOCK = 256
ROPE_BASE = 10000.0
N_GROUPS = 4
EXPERTS_PER_GROUP = 8
N_EXPERTS = N_GROUPS * EXPERTS_PER_GROUP
TOP_K = 2
MOE_BLOCK = 512


def _cparams(*sem):
    return pltpu.CompilerParams(dimension_semantics=sem, vmem_limit_bytes=VMEM_LIMIT)


def _row_tile(t_len):
    for tm in (544, 512, 384, 272, 256, 128, 64, 32, 16, 8):
        if t_len % tm == 0:
            return tm
    raise ValueError(f"unsupported stream length {t_len}")


def _ada_kernel(c_ref, w_ref, b_ref, o_ref):
    a = jax.nn.silu(c_ref[...]).astype(BF16)
    o_ref[0] = jnp.dot(a, w_ref[0].astype(BF16), preferred_element_type=F32) + b_ref[0]


def _ada_tables(cc, w_ada, b_ada):
    depth, d, n = w_ada.shape
    rows = cc.shape[0]
    tn = 1536 if n % 1536 == 0 else n
    return pl.pallas_call(
        _ada_kernel,
        grid=(depth, n // tn),
        in_specs=[
            pl.BlockSpec((rows, d), lambda l, j: (0, 0)),
            pl.BlockSpec((1, d, tn), lambda l, j: (l, 0, j)),
            pl.BlockSpec((1, 1, tn), lambda l, j: (l, 0, j)),
        ],
        out_specs=pl.BlockSpec((1, rows, tn), lambda l, j: (l, 0, j)),
        out_shape=jax.ShapeDtypeStruct((depth, rows, n), F32),
        compiler_params=_cparams("parallel", "parallel"),
    )(cc, w_ada, b_ada.reshape(depth, 1, n))


def _mod_rows(ml_ref, mc_ref, idx, is_ctx):
    return jnp.where(is_ctx, mc_ref[0, idx:idx + 1, :], ml_ref[0, idx:idx + 1, :])


def _is_ctx_rows(tm, tiles_per_b, n_ctx):
    jt = pl.program_id(0) % tiles_per_b
    row = jt * tm + lax.broadcasted_iota(jnp.int32, (tm, 1), 0)
    return row < n_ctx


def _rmsnorm_rows(x, g):
    return x * lax.rsqrt(jnp.mean(x * x, axis=-1, keepdims=True) + EPS) * g


def _norm_mod_kernel(x_ref, g_ref, ml_ref, mc_ref, o_ref, *, tm, tiles_per_b, n_ctx):
    is_ctx = _is_ctx_rows(tm, tiles_per_b, n_ctx)
    y = _rmsnorm_rows(x_ref[...], g_ref[...])
    y = y * (1.0 + _mod_rows(ml_ref, mc_ref, 1, is_ctx)) + _mod_rows(ml_ref, mc_ref, 0, is_ctx)
    o_ref[...] = y.astype(o_ref.dtype)


def _norm_mod(x2, g, mod_l, mod_c, t_len, n_ctx):
    m, d = x2.shape
    tm = _row_tile(t_len)
    tpb = t_len // tm
    kern = functools.partial(_norm_mod_kernel, tm=tm, tiles_per_b=tpb, n_ctx=n_ctx)
    return pl.pallas_call(
        kern,
        grid=(m // tm,),
        in_specs=[
            pl.BlockSpec((tm, d), lambda i: (i, 0)),
            pl.BlockSpec((1, d), lambda i: (0, 0)),
            pl.BlockSpec((1, N_MOD, d), lambda i: (i // tpb, 0, 0)),
            pl.BlockSpec((1, N_MOD, d), lambda i: (0, 0, 0)),
        ],
        out_specs=pl.BlockSpec((tm, d), lambda i: (i, 0)),
        out_shape=jax.ShapeDtypeStruct((m, d), BF16),
        compiler_params=_cparams("parallel"),
    )(x2, g.reshape(1, d), mod_l, mod_c)


def _mm_kernel(a_ref, w_ref, o_ref):
    o_ref[...] = jnp.dot(a_ref[...], w_ref[...], preferred_element_type=F32).astype(o_ref.dtype)


def _matmul(a, w, out_dtype, tm, tn):
    m, k = a.shape
    n = w.shape[1]
    return pl.pallas_call(
        _mm_kernel,
        grid=(m // tm, n // tn),
        in_specs=[
            pl.BlockSpec((tm, k), lambda i, j: (i, 0)),
            pl.BlockSpec((k, tn), lambda i, j: (0, j)),
        ],
        out_specs=pl.BlockSpec((tm, tn), lambda i, j: (i, j)),
        out_shape=jax.ShapeDtypeStruct((m, n), out_dtype),
        compiler_params=_cparams("parallel", "parallel"),
    )(a, w)


def _s5_weights(lam_re, lam_im, log_dt, b_re, b_im, c_re, c_im, d_skip):
    n_dir, g_n, p_n = lam_re.shape
    h_n = b_re.shape[-1]
    cn = S5_CHUNK
    lr, li = lam_re.astype(F32), lam_im.astype(F32)
    dt = jnp.exp(log_dt.astype(F32))[..., None]
    mag = jnp.exp(lr * dt)
    ar = mag * jnp.cos(li * dt)
    ai = mag * jnp.sin(li * dt)
    den = lr * lr + li * li
    zr = ((ar - 1.0) * lr + ai * li) / den
    zi = (ai * lr - (ar - 1.0) * li) / den
    bbr = zr[..., None] * b_re - zi[..., None] * b_im
    bbi = zr[..., None] * b_im + zi[..., None] * b_re
    j = jnp.arange(cn + 1, dtype=F32)[:, None, None, None]
    pmag = jnp.exp(lr[None] * dt[None] * j)
    pang = li[None] * dt[None] * j
    pr = pmag * jnp.cos(pang)
    pi = pmag * jnp.sin(pang)
    wr = pr[..., None] * bbr[None] - pi[..., None] * bbi[None]
    wi = pr[..., None] * bbi[None] + pi[..., None] * bbr[None]
    kj = (jnp.einsum('dghp,jdgpk->jdghk', c_re, wr, precision=HIGHEST)
          - jnp.einsum('dghp,jdgpk->jdghk', c_im, wi, precision=HIGHEST))
    s_idx = jnp.arange(cn)[:, None]
    t_idx = jnp.arange(cn)[None, :]
    lag = t_idx - s_idx
    kf = kj[:cn, 0][jnp.clip(lag, 0, cn - 1)]
    kb = kj[:cn, 1][jnp.clip(-lag, 0, cn - 1)]
    kf = jnp.where((lag >= 0)[:, :, None, None, None], kf, 0.0)
    kb = jnp.where((lag <= 0)[:, :, None, None, None], kb, 0.0)
    dsk = d_skip.astype(F32).reshape(g_n, h_n)
    eye_h = jnp.eye(h_n, dtype=F32)
    diag = (lag == 0)[:, :, None, None, None] * (dsk[None, None, :, :, None] * eye_h[None, None, None])
    m_full = kf + kb + diag
    m_mat = m_full.transpose(2, 0, 4, 1, 3).reshape(g_n, cn * h_n, cn * h_n)
    wf_r = wr[:cn, 0][::-1]
    wf_i = wi[:cn, 0][::-1]
    wb_r = wr[:cn, 1]
    wb_i = wi[:cn, 1]
    w_in = jnp.stack([wf_r, wf_i, wb_r, wb_i], axis=0)
    w_in = w_in.transpose(2, 1, 4, 0, 3).reshape(g_n, cn * h_n, 4 * p_n)
    def out_map(d, powers):
        prd, pid = pr[powers, d], pi[powers, d]
        w_re = c_re[d][None] * prd[:, :, None, :] - c_im[d][None] * pid[:, :, None, :]
        w_im = c_re[d][None] * pid[:, :, None, :] + c_im[d][None] * prd[:, :, None, :]
        return w_re, -w_im
    t_arr = jnp.arange(cn)
    of_r, of_i = out_map(0, t_arr + 1)
    ob_r, ob_i = out_map(1, cn - t_arr)
    w_out = jnp.stack([of_r, of_i, ob_r, ob_i], axis=0)
    w_out = w_out.transpose(2, 0, 4, 1, 3).reshape(g_n, 4 * p_n, cn * h_n)
    a1 = jnp.concatenate([pr[cn, 0], pr[cn, 0], pr[cn, 1], pr[cn, 1]], axis=-1)
    a2 = jnp.concatenate([-pi[cn, 0], pi[cn, 0], -pi[cn, 1], pi[cn, 1]], axis=-1)
    a_mat = jnp.stack([a1, a2], axis=1)
    w1 = jnp.concatenate([m_mat, w_in], axis=-1).astype(BF16)
    return w1, w_out.astype(BF16), a_mat


def _s5_kernel(u_ref, w1_ref, wo_ref, a_ref, y_ref, x_scr, hp_scr, *, n_chunks, n_ctx_chunks, bsz):
    width = 4 * S5_STATE
    half = 2 * S5_STATE
    z = jnp.dot(u_ref[0], w1_ref[0], preferred_element_type=F32)
    y_ref[0] = z[:, :width]
    x_scr[...] = z[:, width:]
    a1 = jnp.broadcast_to(a_ref[0, 0:1, :], (bsz, width))
    a2 = jnp.broadcast_to(a_ref[0, 1:2, :], (bsz, width))
    a1f, a1b = a1[:, :half], a1[:, half:]
    a2f, a2b = a2[:, :half], a2[:, half:]

    def body(i, carry):
        hf, hb = carry
        nb = jnp.where(i < n_ctx_chunks, n_ctx_chunks - 1 - i, n_chunks - 1 - (i - n_ctx_chunks))
        rf = pl.multiple_of(i * bsz, bsz)
        rb = pl.multiple_of(nb * bsz, bsz)
        hp_scr[pl.ds(rf, bsz), 0:half] = hf
        hp_scr[pl.ds(rb, bsz), half:width] = hb
        hf = a1f * hf + a2f * pltpu.roll(hf, S5_STATE, 1) + x_scr[pl.ds(rf, bsz), 0:half]
        hb = a1b * hb + a2b * pltpu.roll(hb, S5_STATE, 1) + x_scr[pl.ds(rb, bsz), half:width]
        return hf, hb

    zero = jnp.zeros((bsz, half), F32)
    lax.fori_loop(0, n_chunks, body, (zero, zero))
    y_ref[0] += jnp.dot(hp_scr[...].astype(BF16), wo_ref[0], preferred_element_type=F32)


def _s5_mix(u, w1, w_out, a_mat, n_ctx):
    bsz, t_len, width = u.shape
    g_n = width // S5_GROUP
    cn = S5_CHUNK
    n_chunks = t_len // cn
    rows = n_chunks * bsz
    blk = cn * S5_GROUP
    ug = u.reshape(bsz, n_chunks, cn, g_n, S5_GROUP).transpose(3, 1, 0, 2, 4).reshape(g_n, rows, blk)
    kern = functools.partial(_s5_kernel, n_chunks=n_chunks, n_ctx_chunks=n_ctx // cn, bsz=bsz)
    y = pl.pallas_call(
        kern,
        grid=(g_n,),
        in_specs=[
            pl.BlockSpec((1, rows, blk), lambda g: (g, 0, 0)),
            pl.BlockSpec((1, blk, blk + 4 * S5_STATE), lambda g: (g, 0, 0)),
            pl.BlockSpec((1, 4 * S5_STATE, blk), lambda g: (g, 0, 0)),
            pl.BlockSpec((1, 2, 4 * S5_STATE), lambda g: (g, 0, 0)),
        ],
        out_specs=pl.BlockSpec((1, rows, blk), lambda g: (g, 0, 0)),
        out_shape=jax.ShapeDtypeStruct((g_n, rows, blk), F32),
        scratch_shapes=[pltpu.VMEM((rows, 4 * S5_STATE), F32), pltpu.VMEM((rows, 4 * S5_STATE), F32)],
        compiler_params=_cparams("parallel"),
    )(ug, w1, w_out, a_mat)
    y = y.reshape(g_n, n_chunks, bsz, cn, S5_GROUP).transpose(2, 1, 3, 0, 4)
    return y.reshape(bsz, t_len, width)


def _head0_lanes(shape):
    return lax.broadcasted_iota(jnp.int32, shape, len(shape) - 1) < HEAD_DIM


def _stack_heads(q):
    m0 = _head0_lanes(q.shape)
    return jnp.concatenate([jnp.where(m0, q, 0.0), jnp.where(m0, 0.0, q)], axis=-2)


def _unstack_heads(o2):
    c = o2.shape[-2] // 2
    top, bottom = o2[..., :c, :], o2[..., c:, :]
    return jnp.where(_head0_lanes(top.shape), top, bottom)


def _pair_block_mask():
    r = lax.broadcasted_iota(jnp.int32, (PAIR, PAIR), 0) < HEAD_DIM
    c = lax.broadcasted_iota(jnp.int32, (PAIR, PAIR), 1) < HEAD_DIM
    return r == c


def _dot_nt(a, b):
    return lax.dot_general(a, b, (((1,), (1,)), ((), ())), preferred_element_type=F32)


def _dot_tn(a, b):
    return lax.dot_general(a, b, (((0,), (0,)), ((), ())), preferred_element_type=F32)


def _bwd_block(j, n_ctx_blocks, n_blocks):
    return jnp.where(j < n_ctx_blocks, n_ctx_blocks - 1 - j, n_blocks - 1 - (j - n_ctx_blocks))


def _bdot_nt(a, b):
    return lax.dot_general(a, b, (((2,), (2,)), ((0,), (0,))), preferred_element_type=F32)


def _bdot_nn(a, b):
    return lax.dot_general(a, b, (((2,), (1,)), ((0,), (0,))), preferred_element_type=F32)


def _hg_block(q, z, v, lb, ht, reverse):
    rows = q.shape[0]
    c = HG_CHUNK
    hh = HG_HALF
    n = rows // c
    f = lb + (1.0 - lb) * jax.nn.sigmoid(z)
    logf = jnp.log(f)
    kk = 1.0 - f
    ri = lax.broadcasted_iota(jnp.int32, (rows, rows), 0)
    ci = lax.broadcasted_iota(jnp.int32, (rows, rows), 1)
    same_chunk = (ri // c) == (ci // c)
    tri = jnp.where(same_chunk & ((ci >= ri) if reverse else (ci <= ri)), 1.0, 0.0).astype(F32)
    b = jnp.dot(tri, logf, precision=HIGHEST, preferred_element_type=F32)
    b4, q4, k4 = (a.reshape(n, c, PAIR) for a in (b, q, kk))
    vb = v.astype(BF16).reshape(n, c, PAIR)
    if reverse:
        first, second = slice(hh, c), slice(0, hh)
        r = b4[:, hh:hh + 1]
        bend = b4[:, 0:1]
    else:
        first, second = slice(0, hh), slice(hh, c)
        r = b4[:, hh - 1:hh]
        bend = b4[:, c - 1:c]
    b1, b2 = b4[:, first], b4[:, second]
    qd1 = q4[:, first] * jnp.exp(b1)
    qd2 = q4[:, second] * jnp.exp(b2 - r)
    k1d = k4[:, first] * jnp.exp(-b1)
    k2 = k4 * jnp.exp(r - b4)
    s1 = _bdot_nt(_stack_heads(qd1).astype(BF16), k1d.astype(BF16))
    s2 = _bdot_nt(_stack_heads(qd2).astype(BF16), k2.astype(BF16))
    t1 = lax.broadcasted_iota(jnp.int32, (n, 2 * hh, hh), 1) % hh
    c1 = lax.broadcasted_iota(jnp.int32, (n, 2 * hh, hh), 2)
    t2 = lax.broadcasted_iota(jnp.int32, (n, 2 * hh, c), 1) % hh
    c2 = lax.broadcasted_iota(jnp.int32, (n, 2 * hh, c), 2)
    if reverse:
        s1 = jnp.where(c1 >= t1, s1, 0.0)
        s2 = jnp.where(c2 >= t2, s2, 0.0)
    else:
        s1 = jnp.where(c1 <= t1, s1, 0.0)
        s2 = jnp.where(c2 <= t2 + hh, s2, 0.0)
    o1 = _unstack_heads(_bdot_nn(s1.astype(BF16), vb[:, first]))
    o2 = _unstack_heads(_bdot_nn(s2.astype(BF16), vb))
    o = jnp.concatenate([o2, o1] if reverse else [o1, o2], axis=1)
    ke = (k4 * jnp.exp(bend - b4)).astype(BF16)
    dec = jnp.exp(bend)
    bmask = _pair_block_mask()
    states = [None] * n
    for ch in (range(n - 1, -1, -1) if reverse else range(n)):
        states[ch] = ht.astype(BF16)
        ht = ht * dec[ch] + jnp.where(bmask, _dot_tn(vb[ch], ke[ch]), 0.0)
    o = o + _bdot_nt((q4 * jnp.exp(b4)).astype(BF16), jnp.stack(states, axis=0))
    return o.reshape(rows, PAIR), ht


def _hg_kernel(qf_ref, zf_ref, vf_ref, qb_ref, zb_ref, vb_ref, lb_ref, of_ref, ob_ref, hf_scr, hb_scr, *, n_pairs):
    @pl.when(pl.program_id(1) == 0)
    def _():
        hf_scr[...] = jnp.zeros_like(hf_scr)
        hb_scr[...] = jnp.zeros_like(hb_scr)

    for p in range(n_pairs):
        sl = slice(p * PAIR, (p + 1) * PAIR)
        o, hn = _hg_block(qf_ref[0, :, sl], zf_ref[0, :, sl], vf_ref[0, :, sl], lb_ref[0:1, sl], hf_scr[p], False)
        of_ref[0, :, sl] = o
        hf_scr[p] = hn
        o, hn = _hg_block(qb_ref[0, :, sl], zb_ref[0, :, sl], vb_ref[0, :, sl], lb_ref[1:2, sl], hb_scr[p], True)
        ob_ref[0, :, sl] = o
        hb_scr[p] = hn


def _hg_mix(hg, lb, n_ctx):
    bsz, t_len, total = hg.shape
    w = total // 5
    n_pairs = w // PAIR
    c = HG_BLOCK
    nc, ncc = t_len // c, n_ctx // c
    bw = functools.partial(_bwd_block, n_ctx_blocks=ncc, n_blocks=nc)

    def fspec(col):
        return pl.BlockSpec((1, c, w), lambda b, j: (b, j, col))

    def bspec(col):
        return pl.BlockSpec((1, c, w), lambda b, j: (b, bw(j), col))

    kern = functools.partial(_hg_kernel, n_pairs=n_pairs)
    return pl.pallas_call(
        kern,
        grid=(bsz, nc),
        in_specs=[fspec(0), fspec(1), fspec(3), bspec(0), bspec(2), bspec(3),
                  pl.BlockSpec((2, w), lambda b, j: (0, 0))],
        out_specs=[pl.BlockSpec((1, c, w), lambda b, j: (b, j, 0)),
                   pl.BlockSpec((1, c, w), lambda b, j: (b, bw(j), 0))],
        out_shape=[jax.ShapeDtypeStruct((bsz, t_len, w), F32)] * 2,
        scratch_shapes=[pltpu.VMEM((n_pairs, PAIR, PAIR), F32)] * 2,
        compiler_params=_cparams("parallel", "arbitrary"),
    )(hg, hg, hg, hg, hg, hg, lb)


def _rope_rows(x, cos, sin):
    hd2 = HEAD_DIM // 2
    lane = lax.broadcasted_iota(jnp.int32, x.shape, 1) % HEAD_DIM
    swapped = jnp.where(lane < hd2, pltpu.roll(x, PAIR - hd2, 1), pltpu.roll(x, hd2, 1))
    return x * cos + swapped * sin


def _ret_kernel(qf_ref, kf_ref, vf_ref, cf_ref, sf_ref, qb_ref, kb_ref, vb_ref, cb_ref, sb_ref,
                d_ref, gqf_ref, gkf_ref, gqb_ref, gkb_ref, gh_ref, of_ref, ob_ref, hf_scr, hb_scr, *, n_pairs):
    @pl.when(pl.program_id(1) == 0)
    def _():
        hf_scr[...] = jnp.zeros_like(hf_scr)
        hb_scr[...] = jnp.zeros_like(hb_scr)

    kscale = HEAD_DIM ** -0.5
    bmask = _pair_block_mask()
    for p in range(n_pairs):
        sl = slice(p * PAIR, (p + 1) * PAIR)
        q = _rope_rows(qf_ref[0, :, sl], cf_ref[...], sf_ref[...])
        k = _rope_rows(kf_ref[0, :, sl] * kscale, cf_ref[...], sf_ref[...])
        vb16 = vf_ref[0, :, sl].astype(BF16)
        s = _dot_nt(_stack_heads(q).astype(BF16), k.astype(BF16)) * d_ref[p]
        o = _unstack_heads(jnp.dot(s.astype(BF16), vb16, preferred_element_type=F32))
        ht = hf_scr[p]
        o = o + _dot_nt((q * gqf_ref[:, sl]).astype(BF16), ht.astype(BF16))
        upd = _dot_tn(vb16, (k * gkf_ref[:, sl]).astype(BF16))
        hf_scr[p] = ht * gh_ref[p] + jnp.where(bmask, upd, 0.0)
        of_ref[0, :, sl] = o

        q = _rope_rows(qb_ref[0, :, sl], cb_ref[...], sb_ref[...])
        k = _rope_rows(kb_ref[0, :, sl] * kscale, cb_ref[...], sb_ref[...])
        vb16 = vb_ref[0, :, sl].astype(BF16)
        ht = hb_scr[p]
        ob_ref[0, :, sl] = _dot_nt((q * gqb_ref[:, sl]).astype(BF16), ht.astype(BF16))
        upd = _dot_tn(vb16, (k * gkb_ref[:, sl]).astype(BF16))
        hb_scr[p] = ht * gh_ref[p] + jnp.where(bmask, upd, 0.0)


def _ret_constants(n_heads, t_len, n_ctx):
    c = RET_BLOCK
    log_gamma = jnp.log(1.0 - 2.0 ** (-5.0 - jnp.arange(n_heads, dtype=F32)))
    t = jnp.arange(c, dtype=F32)
    dist = jnp.abs(t[:, None] - t[None, :])
    dmat = jnp.exp(log_gamma[:, None, None] * dist[None])
    dmat = dmat.reshape(n_heads // 2, 2 * c, c)
    lg_lane = jnp.repeat(log_gamma, HEAD_DIM)[None, :]
    gqf = jnp.exp(lg_lane * (t[:, None] + 1.0))
    gkf = jnp.exp(lg_lane * (c - 1.0 - t[:, None]))
    gqb = jnp.exp(lg_lane * (c - t[:, None]))
    gkb = jnp.exp(lg_lane * t[:, None])
    gh = jnp.exp(lg_lane * float(c)).reshape(n_heads // 2, 1, PAIR)
    gh = jnp.broadcast_to(gh, (n_heads // 2, PAIR, PAIR))
    n_lat = t_len - n_ctx
    rows = n_lat // GRID_W
    row = jnp.repeat(jnp.arange(rows, dtype=F32), GRID_W)
    col = jnp.broadcast_to(jnp.arange(GRID_W, dtype=F32)[None, :], (rows, GRID_W)).reshape(-1)
    n_freq = HEAD_DIM // 4
    inv_freq = ROPE_BASE ** (-jnp.arange(n_freq, dtype=F32) / n_freq)
    ang = jnp.concatenate([row[:, None] * inv_freq, col[:, None] * inv_freq], axis=-1)
    cos_l, sin_l = jnp.cos(ang), jnp.sin(ang)
    cos_h = jnp.concatenate([cos_l, cos_l], axis=-1)
    sin_h = jnp.concatenate([-sin_l, sin_l], axis=-1)
    cos_t = jnp.concatenate([jnp.ones((n_ctx, HEAD_DIM), F32), cos_h], axis=0)
    sin_t = jnp.concatenate([jnp.zeros((n_ctx, HEAD_DIM), F32), sin_h], axis=0)
    cos_t = jnp.concatenate([cos_t, cos_t], axis=-1)
    sin_t = jnp.concatenate([sin_t, sin_t], axis=-1)
    return dmat, gqf, gkf, gqb, gkb, gh, cos_t, sin_t


def _ret_mix(ret, consts, n_ctx):
    dmat, gqf, gkf, gqb, gkb, gh, cos_t, sin_t = consts
    bsz, t_len, total = ret.shape
    w = total // 4
    n_pairs = w // PAIR
    c = RET_BLOCK
    nb, ncb = t_len // c, n_ctx // c
    bw = functools.partial(_bwd_block, n_ctx_blocks=ncb, n_blocks=nb)

    def fspec(col):
        return pl.BlockSpec((1, c, w), lambda b, j: (b, j, col))

    def bspec(col):
        return pl.BlockSpec((1, c, w), lambda b, j: (b, bw(j), col))

    def const(a):
        nd = a.ndim
        return pl.BlockSpec(a.shape, lambda b, j: (0,) * nd)

    rope_f = pl.BlockSpec((c, PAIR), lambda b, j: (j, 0))
    rope_b = pl.BlockSpec((c, PAIR), lambda b, j: (bw(j), 0))
    kern = functools.partial(_ret_kernel, n_pairs=n_pairs)
    return pl.pallas_call(
        kern,
        grid=(bsz, nb),
        in_specs=[fspec(0), fspec(1), fspec(2), rope_f, rope_f, bspec(0), bspec(1), bspec(2), rope_b, rope_b,
                  const(dmat), const(gqf), const(gkf), const(gqb), const(gkb), const(gh)],
        out_specs=[pl.BlockSpec((1, c, w), lambda b, j: (b, j, 0)),
                   pl.BlockSpec((1, c, w), lambda b, j: (b, bw(j), 0))],
        out_shape=[jax.ShapeDtypeStruct((bsz, t_len, w), F32)] * 2,
        scratch_shapes=[pltpu.VMEM((n_pairs, PAIR, PAIR), F32)] * 2,
        compiler_params=_cparams("parallel", "arbitrary"),
    )(ret, ret, ret, cos_t, sin_t, ret, ret, ret, cos_t, sin_t, dmat, gqf, gkf, gqb, gkb, gh)


def _head_norm_gate(o, g, ones_bd):
    sq = o * o
    hi = sq.astype(BF16)
    lo = (sq - hi.astype(F32)).astype(BF16)
    ss = jnp.dot(hi, ones_bd, preferred_element_type=F32) + jnp.dot(lo, ones_bd, preferred_element_type=F32)
    return o * lax.rsqrt(ss * (1.0 / HEAD_DIM) + EPS) * jax.nn.silu(g)


def _route(lg):
    lane = lax.broadcasted_iota(jnp.int32, lg.shape, 1).astype(F32)
    big = float(1 << 20)
    neg = -jnp.inf
    gmask = lane < N_GROUPS
    lgm = jnp.where(gmask, lg, neg)
    gmax = jnp.max(lgm, axis=1, keepdims=True)
    gsum = jnp.sum(jnp.where(gmask, jnp.exp(lgm - gmax), 0.0), axis=1, keepdims=True)
    gp = 1.0 / gsum
    gi = jnp.min(jnp.where(gmask & (lgm == gmax), lane, big), axis=1, keepdims=True)
    lo_lane = N_GROUPS + EXPERTS_PER_GROUP * gi
    emask = (lane >= lo_lane) & (lane < lo_lane + EXPERTS_PER_GROUP)
    l1 = jnp.max(jnp.where(emask, lg, neg), axis=1, keepdims=True)
    i1 = jnp.min(jnp.where(emask & (lg == l1), lane, big), axis=1, keepdims=True)
    mask2 = emask & (lane != i1)
    l2 = jnp.max(jnp.where(mask2, lg, neg), axis=1, keepdims=True)
    i2 = jnp.min(jnp.where(mask2 & (lg == l2), lane, big), axis=1, keepdims=True)
    e21 = jnp.exp(l2 - l1)
    w1 = gp / (1.0 + e21)
    w2 = gp * e21 / (1.0 + e21)
    e1 = i1 - N_GROUPS
    e2 = i2 - N_GROUPS
    return jnp.where(lane == 0, w1, jnp.where(lane == 1, w2, jnp.where(lane == 2, e1, jnp.where(lane == 3, e2, 0.0))))


def _merge_kernel(x_ref, ys_ref, hof_ref, hob_ref, hgg_ref, rof_ref, rob_ref, rgg_ref, gz_ref,
                  ml_ref, mc_ref, g2_ref, wglu_ref, wbs_ref, wbh_ref, wbr_ref, wout_ref, ones_ref,
                  wr_ref, br_ref, xo_ref, h2_ref, rt_ref, *, tm, tiles_per_b, n_ctx):
    d = x_ref.shape[1]
    is_ctx = _is_ctx_rows(tm, tiles_per_b, n_ctx)
    y = jax.nn.gelu(ys_ref[...], approximate=True)
    y = y * jax.nn.sigmoid(jnp.dot(y.astype(BF16), wglu_ref[...], preferred_element_type=F32))
    t_s5 = jnp.dot(y.astype(BF16), wbs_ref[...], preferred_element_type=F32)
    yh = _head_norm_gate(hof_ref[...] + hob_ref[...], hgg_ref[...], ones_ref[...])
    t_hg = jnp.dot(yh.astype(BF16), wbh_ref[...], preferred_element_type=F32)
    yr = _head_norm_gate(rof_ref[...] + rob_ref[...], rgg_ref[...], ones_ref[...])
    t_ret = jnp.dot(yr.astype(BF16), wbr_ref[...], preferred_element_type=F32)
    merged = (jax.nn.sigmoid(gz_ref[:, 0:d]) * t_s5 + jax.nn.sigmoid(gz_ref[:, d:2 * d]) * t_hg
              + jax.nn.sigmoid(gz_ref[:, 2 * d:3 * d]) * t_ret)
    mix = jnp.dot(merged.astype(BF16), wout_ref[...], preferred_element_type=F32)
    xn = x_ref[...] + _mod_rows(ml_ref, mc_ref, 2, is_ctx) * mix
    xo_ref[...] = xn
    h2 = _rmsnorm_rows(xn, g2_ref[...])
    h2 = h2 * (1.0 + _mod_rows(ml_ref, mc_ref, 4, is_ctx)) + _mod_rows(ml_ref, mc_ref, 3, is_ctx)
    h2_ref[...] = h2.astype(BF16)
    lg = jnp.dot(h2, wr_ref[...], precision=HIGHEST, preferred_element_type=F32) + br_ref[...]
    rt_ref[...] = _route(lg)


def _merge(x2, ys5, hg_of, hg_ob, hg2, ret_of, ret_ob, ret2, gz, mod_l, mod_c, g2, wts, t_len, n_ctx):
    m, d = x2.shape
    w_s5 = ys5.shape[1]
    w_h = hg_of.shape[1]
    tm = _row_tile(t_len)
    tpb = t_len // tm
    wglu, wbs, wbh, wbr, wout, ones_bd, wr, br = wts

    def rows(width, col=0):
        return pl.BlockSpec((tm, width), lambda i: (i, col))

    def const(a):
        nd = a.ndim
        return pl.BlockSpec(a.shape, lambda i: (0,) * nd)

    kern = functools.partial(_merge_kernel, tm=tm, tiles_per_b=tpb, n_ctx=n_ctx)
    return pl.pallas_call(
        kern,
        grid=(m // tm,),
        in_specs=[rows(d), rows(w_s5), rows(w_h), rows(w_h), rows(w_h, 4), rows(w_h), rows(w_h), rows(w_h, 3),
                  rows(3 * d),
                  pl.BlockSpec((1, N_MOD, d), lambda i: (i // tpb, 0, 0)),
                  pl.BlockSpec((1, N_MOD, d), lambda i: (0, 0, 0)),
                  const(g2), const(wglu), const(wbs), const(wbh), const(wbr), const(wout), const(ones_bd),
                  const(wr), const(br)],
        out_specs=[rows(d), rows(d), rows(LANES)],
        out_shape=[jax.ShapeDtypeStruct((m, d), F32), jax.ShapeDtypeStruct((m, d), BF16),
                   jax.ShapeDtypeStruct((m, LANES), F32)],
        compiler_params=_cparams("parallel"),
    )(x2, ys5, hg_of, hg_ob, hg2, ret_of, ret_ob, ret2, gz, mod_l, mod_c, g2, wglu, wbs, wbh, wbr, wout,
      ones_bd, wr, br)


def _ffn_kernel(lay_ref, be_ref, nu_ref, x_ref, wg_ref, wu_ref, wd_ref, o_ref, wg_scr, wu_scr, wd_scr):
    i = pl.program_id(0)
    prev = be_ref[jnp.maximum(i - 1, 0)]

    @pl.when((i == 0) | (be_ref[i] != prev))
    def _():
        wg_scr[...] = wg_ref[0, 0].astype(BF16)
        wu_scr[...] = wu_ref[0, 0].astype(BF16)
        wd_scr[...] = wd_ref[0, 0].astype(BF16)

    @pl.when(i < nu_ref[0])
    def _():
        xb = x_ref[...]
        hid = (jax.nn.silu(jnp.dot(xb, wg_scr[...], preferred_element_type=F32))
               * jnp.dot(xb, wu_scr[...], preferred_element_type=F32))
        o_ref[...] = jnp.dot(hid.astype(BF16), wd_scr[...], preferred_element_type=F32)

    @pl.when(i >= nu_ref[0])
    def _():
        o_ref[...] = jnp.zeros_like(o_ref)


def _expert_ffn(layer, blk_e, n_used, xg, wg, wu, wd):
    n_slots, d = xg.shape
    f = wg.shape[3]
    bm = MOE_BLOCK
    grid_spec = pltpu.PrefetchScalarGridSpec(
        num_scalar_prefetch=3,
        grid=(n_slots // bm,),
        in_specs=[
            pl.BlockSpec((bm, d), lambda i, lay, be, nu: (i, 0)),
            pl.BlockSpec((1, 1, d, f), lambda i, lay, be, nu: (lay[0], be[i], 0, 0)),
            pl.BlockSpec((1, 1, d, f), lambda i, lay, be, nu: (lay[0], be[i], 0, 0)),
            pl.BlockSpec((1, 1, f, d), lambda i, lay, be, nu: (lay[0], be[i], 0, 0)),
        ],
        out_specs=pl.BlockSpec((bm, d), lambda i, lay, be, nu: (i, 0)),
        scratch_shapes=[pltpu.VMEM((d, f), BF16), pltpu.VMEM((d, f), BF16), pltpu.VMEM((f, d), BF16)],
    )
    return pl.pallas_call(
        _ffn_kernel,
        grid_spec=grid_spec,
        out_shape=jax.ShapeDtypeStruct((n_slots, d), F32),
        compiler_params=_cparams("arbitrary"),
    )(layer, blk_e, n_used, xg, wg, wu, wd)


def _combine_kernel(x_ref, y0_ref, y1_ref, rt_ref, ml_ref, mc_ref, o_ref, *, tm, tiles_per_b, n_ctx):
    is_ctx = _is_ctx_rows(tm, tiles_per_b, n_ctx)
    rt = rt_ref[...]
    y = rt[:, 0:1] * y0_ref[...] + rt[:, 1:2] * y1_ref[...]
    o_ref[...] = x_ref[...] + _mod_rows(ml_ref, mc_ref, 5, is_ctx) * y


def _combine(x2, y0, y1, route, mod_l, mod_c, t_len, n_ctx):
    m, d = x2.shape
    tm = _row_tile(t_len)
    tpb = t_len // tm
    kern = functools.partial(_combine_kernel, tm=tm, tiles_per_b=tpb, n_ctx=n_ctx)
    return pl.pallas_call(
        kern,
        grid=(m // tm,),
        in_specs=[pl.BlockSpec((tm, d), lambda i: (i, 0))] * 3
        + [pl.BlockSpec((tm, LANES), lambda i: (i, 0)),
           pl.BlockSpec((1, N_MOD, d), lambda i: (i // tpb, 0, 0)),
           pl.BlockSpec((1, N_MOD, d), lambda i: (0, 0, 0))],
        out_specs=pl.BlockSpec((tm, d), lambda i: (i, 0)),
        out_shape=jax.ShapeDtypeStruct((m, d), F32),
        compiler_params=_cparams("parallel"),
    )(x2, y0, y1, route, mod_l, mod_c)


def _final_norm_kernel(x_ref, g_ref, o_ref):
    o_ref[0] = _rmsnorm_rows(x_ref[0], g_ref[...])


def _final_norm(x3, g, n_ctx):
    bsz, t_len, d = x3.shape
    n_lat = t_len - n_ctx
    tm = n_ctx
    off = n_ctx // tm
    return pl.pallas_call(
        _final_norm_kernel,
        grid=(bsz, n_lat // tm),
        in_specs=[pl.BlockSpec((1, tm, d), lambda b, j: (b, j + off, 0)),
                  pl.BlockSpec((1, d), lambda b, j: (0, 0))],
        out_specs=pl.BlockSpec((1, tm, d), lambda b, j: (b, j, 0)),
        out_shape=jax.ShapeDtypeStruct((bsz, n_lat, d), F32),
        compiler_params=_cparams("parallel", "parallel"),
    )(x3, g.reshape(1, d))


def _dispatch_indices(eid, n_tok):
    a = n_tok * TOP_K
    bm = MOE_BLOCK
    flat_e = eid.reshape(a)
    experts = jnp.arange(N_EXPERTS, dtype=jnp.int32)
    onehot = flat_e[:, None] == experts[None, :]
    counts = jnp.sum(onehot, axis=0, dtype=jnp.int32)
    padded = (counts + bm - 1) // bm * bm
    pad_end = jnp.cumsum(padded)
    pad_start = pad_end - padded
    start = jnp.cumsum(counts) - counts
    order = jnp.argsort(flat_e).astype(jnp.int32)
    inv = jnp.argsort(order).astype(jnp.int32)
    shift = jnp.sum(jnp.where(onehot, (pad_start - start)[None, :], 0), axis=1, dtype=jnp.int32)
    pos = (inv + shift).reshape(n_tok, TOP_K)
    n_blocks = -(-a // bm) + N_EXPERTS
    blk_start = jnp.arange(n_blocks, dtype=jnp.int32) * bm
    blk_e = jnp.minimum(jnp.sum(pad_end[None, :] <= blk_start[:, None], axis=1, dtype=jnp.int32), N_EXPERTS - 1)
    slot = jnp.arange(n_blocks * bm, dtype=jnp.int32)
    slot_shift = jnp.repeat((pad_start - start)[blk_e], bm)
    src = jnp.clip(slot - slot_shift, 0, a - 1)
    slot_tok = order.at[src].get(mode='promise_in_bounds') // TOP_K
    n_used = (pad_end[-1] // bm).astype(jnp.int32).reshape(1)
    return slot_tok, pos, blk_e, n_used


def _moe(layer, h2, route, wg, wu, wd):
    n_tok = h2.shape[0]
    eid = route[:, 2:4].astype(jnp.int32)
    slot_tok, pos, blk_e, n_used = _dispatch_indices(eid, n_tok)
    xg = h2.at[slot_tok].get(mode='promise_in_bounds')
    yb = _expert_ffn(layer, blk_e, n_used, xg, wg, wu, wd)
    return yb.at[pos[:, 0]].get(mode='promise_in_bounds'), yb.at[pos[:, 1]].get(mode='promise_in_bounds')


def _mm_tiles(t_len, n):
    tm = _row_tile(t_len)
    for tn in (1536, 1024, 768, 640, 512, 384, 256, 128):
        if n % tn == 0:
            return tm, tn
    return tm, n


def kernel(x, c, ctx, c_ctx, w_ada, b_ada, norm1_g, norm2_g, w_in, s5_lam_re, s5_lam_im, s5_log_dt, s5_b_re, s5_b_im, s5_c_re, s5_c_im, s5_d, s5_w_glu, hgrn_lb_raw, w_branch_s5, w_branch_hgrn, w_branch_ret, w_out, moe_w_group, moe_b_group, moe_w_expert, moe_b_expert, moe_w_gate, moe_w_up, moe_w_down, final_norm_g):
    bsz, n_lat, d = x.shape
    n_ctx = ctx.shape[1]
    t_len = n_ctx + n_lat
    depth = w_ada.shape[0]
    m = bsz * t_len
    w_s5 = s5_d.shape[1]
    w_hg = w_branch_hgrn.shape[1]
    w_ret = w_branch_ret.shape[1]
    n_heads = w_ret // HEAD_DIM

    pad_rows = (-(bsz + 1)) % SUBLANES
    cc = jnp.concatenate([c, c_ctx[None, :], jnp.zeros((pad_rows, d), F32)], axis=0)
    mods = _ada_tables(cc, w_ada, b_ada).reshape(depth, bsz + 1 + pad_rows, N_MOD, d)

    p_lb = jax.nn.softmax(hgrn_lb_raw.astype(F32), axis=0)
    lower_bounds = jnp.cumsum(p_lb, axis=0) - p_lb[0:1]
    ret_consts = _ret_constants(n_heads, t_len, n_ctx)
    ones_bd = jnp.kron(jnp.eye(w_hg // HEAD_DIM, dtype=F32), jnp.ones((HEAD_DIM, HEAD_DIM), F32)).astype(BF16)

    xs = jnp.concatenate([ctx, x], axis=1).reshape(m, d)
    off_hg = w_s5
    off_ret = off_hg + 5 * w_hg
    off_gz = off_ret + 4 * w_ret
    for l in range(depth):
        mod_l = mods[l, :bsz]
        mod_c = mods[l, bsz:bsz + 1]
        h = _norm_mod(xs, norm1_g[l], mod_l, mod_c, t_len, n_ctx)
        wl = w_in[l].astype(BF16)
        u = _matmul(h, wl[:, :off_hg], BF16, *_mm_tiles(t_len, w_s5))
        hg = _matmul(h, wl[:, off_hg:off_ret], F32, *_mm_tiles(t_len, 5 * w_hg))
        ret = _matmul(h, wl[:, off_ret:off_gz], F32, *_mm_tiles(t_len, 4 * w_ret))
        gz = _matmul(h, wl[:, off_gz:], F32, *_mm_tiles(t_len, 3 * d))

        s5w = _s5_weights(s5_lam_re[l], s5_lam_im[l], s5_log_dt[l], s5_b_re[l], s5_b_im[l],
                          s5_c_re[l], s5_c_im[l], s5_d[l])
        ys5 = _s5_mix(u.reshape(bsz, t_len, w_s5), *s5w, n_ctx).reshape(m, w_s5)
        hg_of, hg_ob = _hg_mix(hg.reshape(bsz, t_len, 5 * w_hg), lower_bounds[l], n_ctx)
        ret_of, ret_ob = _ret_mix(ret.reshape(bsz, t_len, 4 * w_ret), ret_consts, n_ctx)

        wr = jnp.concatenate([moe_w_group[l], moe_w_expert[l],
                              jnp.zeros((d, LANES - N_GROUPS - N_EXPERTS), F32)], axis=1)
        br = jnp.concatenate([moe_b_group[l], moe_b_expert[l],
                              jnp.zeros((LANES - N_GROUPS - N_EXPERTS,), F32)])[None, :]
        wts = (s5_w_glu[l].astype(BF16), w_branch_s5[l].astype(BF16), w_branch_hgrn[l].astype(BF16),
               w_branch_ret[l].astype(BF16), w_out[l].astype(BF16), ones_bd, wr, br)
        xs, h2, route = _merge(xs, ys5, hg_of.reshape(m, w_hg), hg_ob.reshape(m, w_hg), hg,
                               ret_of.reshape(m, w_ret), ret_ob.reshape(m, w_ret), ret, gz,
                               mod_l, mod_c, norm2_g[l].reshape(1, d), wts, t_len, n_ctx)
        y0, y1 = _moe(jnp.full((1,), l, jnp.int32), h2, route, moe_w_gate, moe_w_up, moe_w_down)
        xs = _combine(xs, y0, y1, route, mod_l, mod_c, t_len, n_ctx)
    return _final_norm(xs.reshape(bsz, t_len, d), final_norm_g, n_ctx)
```

```python
import functools

import jax
import jax.numpy as jnp
import numpy as np
from jax import lax
from jax.experimental import pallas as pl
from jax.experimental.pallas import tpu as pltpu

F32 = jnp.float32
BF16 = jnp.bfloat16
HIGHEST = lax.Precision.HIGHEST

LANES = 128
SUBLANES = 8
VMEM_LIMIT = 56 * 1024 * 1024

EPS = 1e-6
N_MOD = 6
GRID_W = 64
HEAD_DIM = 64
PAIR = 2 * HEAD_DIM
S5_GROUP = 16
S5_STATE = 64
S5_CHUNK = 16
HG_CHUNK = 64
HG_HALF = HG_CHUNK // 2
HG_BLOCK = 256
RET_BLOCK = 256
ROPE_BASE = 10000.0
N_GROUPS = 4
EXPERTS_PER_GROUP = 8
N_EXPERTS = N_GROUPS * EXPERTS_PER_GROUP
TOP_K = 2
MOE_BLOCK = 512


def _cparams(*sem):
    return pltpu.CompilerParams(dimension_semantics=sem, vmem_limit_bytes=VMEM_LIMIT)


def _row_tile(t_len):
    for tm in (544, 512, 384, 272, 256, 128, 64, 32, 16, 8):
        if t_len % tm == 0:
            return tm
    raise ValueError(f"unsupported stream length {t_len}")


def _ada_kernel(c_ref, w_ref, b_ref, o_ref):
    a = jax.nn.silu(c_ref[...]).astype(BF16)
    o_ref[0] = jnp.dot(a, w_ref[0].astype(BF16), preferred_element_type=F32) + b_ref[0]


def _ada_tables(cc, w_ada, b_ada):
    depth, d, n = w_ada.shape
    rows = cc.shape[0]
    tn = 1536 if n % 1536 == 0 else n
    return pl.pallas_call(
        _ada_kernel,
        grid=(depth, n // tn),
        in_specs=[
            pl.BlockSpec((rows, d), lambda l, j: (0, 0)),
            pl.BlockSpec((1, d, tn), lambda l, j: (l, 0, j)),
            pl.BlockSpec((1, 1, tn), lambda l, j: (l, 0, j)),
        ],
        out_specs=pl.BlockSpec((1, rows, tn), lambda l, j: (l, 0, j)),
        out_shape=jax.ShapeDtypeStruct((depth, rows, n), F32),
        compiler_params=_cparams("parallel", "parallel"),
    )(cc, w_ada, b_ada.reshape(depth, 1, n))


def _mod_rows(ml_ref, mc_ref, idx, is_ctx):
    return jnp.where(is_ctx, mc_ref[0, idx:idx + 1, :], ml_ref[0, idx:idx + 1, :])


def _is_ctx_rows(tm, tiles_per_b, n_ctx):
    jt = pl.program_id(0) % tiles_per_b
    row = jt * tm + lax.broadcasted_iota(jnp.int32, (tm, 1), 0)
    return row < n_ctx


def _rmsnorm_rows(x, g):
    return x * lax.rsqrt(jnp.mean(x * x, axis=-1, keepdims=True) + EPS) * g


def _norm_mod_kernel(x_ref, g_ref, ml_ref, mc_ref, o_ref, *, tm, tiles_per_b, n_ctx):
    is_ctx = _is_ctx_rows(tm, tiles_per_b, n_ctx)
    y = _rmsnorm_rows(x_ref[...], g_ref[...])
    y = y * (1.0 + _mod_rows(ml_ref, mc_ref, 1, is_ctx)) + _mod_rows(ml_ref, mc_ref, 0, is_ctx)
    o_ref[...] = y.astype(o_ref.dtype)


def _norm_mod(x2, g, mod_l, mod_c, t_len, n_ctx):
    m, d = x2.shape
    tm = _row_tile(t_len)
    tpb = t_len // tm
    kern = functools.partial(_norm_mod_kernel, tm=tm, tiles_per_b=tpb, n_ctx=n_ctx)
    return pl.pallas_call(
        kern,
        grid=(m // tm,),
        in_specs=[
            pl.BlockSpec((tm, d), lambda i: (i, 0)),
            pl.BlockSpec((1, d), lambda i: (0, 0)),
            pl.BlockSpec((1, N_MOD, d), lambda i: (i // tpb, 0, 0)),
            pl.BlockSpec((1, N_MOD, d), lambda i: (0, 0, 0)),
        ],
        out_specs=pl.BlockSpec((tm, d), lambda i: (i, 0)),
        out_shape=jax.ShapeDtypeStruct((m, d), BF16),
        compiler_params=_cparams("parallel"),
    )(x2, g.reshape(1, d), mod_l, mod_c)


def _mm_kernel(a_ref, w_ref, o_ref):
    o_ref[...] = jnp.dot(a_ref[...], w_ref[...], preferred_element_type=F32).astype(o_ref.dtype)


def _matmul(a, w, out_dtype, tm, tn):
    m, k = a.shape
    n = w.shape[1]
    return pl.pallas_call(
        _mm_kernel,
        grid=(m // tm, n // tn),
        in_specs=[
            pl.BlockSpec((tm, k), lambda i, j: (i, 0)),
            pl.BlockSpec((k, tn), lambda i, j: (0, j)),
        ],
        out_specs=pl.BlockSpec((tm, tn), lambda i, j: (i, j)),
        out_shape=jax.ShapeDtypeStruct((m, n), out_dtype),
        compiler_params=_cparams("parallel", "parallel"),
    )(a, w)


def _s5_weights(lam_re, lam_im, log_dt, b_re, b_im, c_re, c_im, d_skip):
    n_dir, g_n, p_n = lam_re.shape
    h_n = b_re.shape[-1]
    cn = S5_CHUNK
    lr, li = lam_re.astype(F32), lam_im.astype(F32)
    dt = jnp.exp(log_dt.astype(F32))[..., None]
    mag = jnp.exp(lr * dt)
    ar = mag * jnp.cos(li * dt)
    ai = mag * jnp.sin(li * dt)
    den = lr * lr + li * li
    zr = ((ar - 1.0) * lr + ai * li) / den
    zi = (ai * lr - (ar - 1.0) * li) / den
    bbr = zr[..., None] * b_re - zi[..., None] * b_im
    bbi = zr[..., None] * b_im + zi[..., None] * b_re
    j = jnp.arange(cn + 1, dtype=F32)[:, None, None, None]
    pmag = jnp.exp(lr[None] * dt[None] * j)
    pang = li[None] * dt[None] * j
    pr = pmag * jnp.cos(pang)
    pi = pmag * jnp.sin(pang)
    wr = pr[..., None] * bbr[None] - pi[..., None] * bbi[None]
    wi = pr[..., None] * bbi[None] + pi[..., None] * bbr[None]
    kj = (jnp.einsum('dghp,jdgpk->jdghk', c_re, wr, precision=HIGHEST)
          - jnp.einsum('dghp,jdgpk->jdghk', c_im, wi, precision=HIGHEST))
    s_idx = jnp.arange(cn)[:, None]
    t_idx = jnp.arange(cn)[None, :]
    lag = t_idx - s_idx
    kf = kj[:cn, 0][jnp.clip(lag, 0, cn - 1)]
    kb = kj[:cn, 1][jnp.clip(-lag, 0, cn - 1)]
    kf = jnp.where((lag >= 0)[:, :, None, None, None], kf, 0.0)
    kb = jnp.where((lag <= 0)[:, :, None, None, None], kb, 0.0)
    dsk = d_skip.astype(F32).reshape(g_n, h_n)
    eye_h = jnp.eye(h_n, dtype=F32)
    diag = (lag == 0)[:, :, None, None, None] * (dsk[None, None, :, :, None] * eye_h[None, None, None])
    m_full = kf + kb + diag
    m_mat = m_full.transpose(2, 0, 4, 1, 3).reshape(g_n, cn * h_n, cn * h_n)
    wf_r = wr[:cn, 0][::-1]
    wf_i = wi[:cn, 0][::-1]
    wb_r = wr[:cn, 1]
    wb_i = wi[:cn, 1]
    w_in = jnp.stack([wf_r, wf_i, wb_r, wb_i], axis=0)
    w_in = w_in.transpose(2, 1, 4, 0, 3).reshape(g_n, cn * h_n, 4 * p_n)
    def out_map(d, powers):
        prd, pid = pr[powers, d], pi[powers, d]
        w_re = c_re[d][None] * prd[:, :, None, :] - c_im[d][None] * pid[:, :, None, :]
        w_im = c_re[d][None] * pid[:, :, None, :] + c_im[d][None] * prd[:, :, None, :]
        return w_re, -w_im
    t_arr = jnp.arange(cn)
    of_r, of_i = out_map(0, t_arr + 1)
    ob_r, ob_i = out_map(1, cn - t_arr)
    w_out = jnp.stack([of_r, of_i, ob_r, ob_i], axis=0)
    w_out = w_out.transpose(2, 0, 4, 1, 3).reshape(g_n, 4 * p_n, cn * h_n)
    slot = jax.nn.one_hot(jnp.arange(g_n) % 2, 2, dtype=F32)
    w_in = w_in.reshape(g_n, cn * h_n, 4, 1, p_n) * slot[:, None, None, :, None]
    w_in = w_in.reshape(g_n, cn * h_n, 8 * p_n)
    w_out = w_out.reshape(g_n, 4, 1, p_n, cn * h_n) * slot[:, None, :, None, None]
    w_out = w_out.reshape(g_n, 8 * p_n, cn * h_n)
    a_mat = jnp.stack([pr[cn, 0].reshape(-1), pi[cn, 0].reshape(-1),
                       pr[cn, 1].reshape(-1), pi[cn, 1].reshape(-1)], axis=0)
    w1 = jnp.concatenate([m_mat, w_in], axis=-1).astype(BF16)
    return w1, w_out.astype(BF16), a_mat


def _s5_fold_perm():
    j, g8, h = np.meshgrid(np.arange(8), np.arange(8), np.arange(S5_GROUP), indexing='ij')
    src = (j * 8 + g8) * S5_GROUP + h
    dst = (g8 * 8 + j) * S5_GROUP + h
    perm = np.zeros((8 * LANES, 8 * LANES), np.float32)
    perm[src.reshape(-1), dst.reshape(-1)] = 1.0
    return jnp.asarray(perm, dtype=BF16)


def _s5_kernel(u_ref, perm_ref, w1_ref, wo_ref, a_ref, y_ref, ug_scr, yg_scr, xfr, xfi, xbr, xbi,
               *, n_chunks, n_ctx_chunks, n_groups):
    cn = S5_CHUNK
    blk = cn * S5_GROUP
    tok_per_slab = LANES // S5_GROUP
    for c in range(cn // tok_per_slab):
        slabs = [u_ref[0, pl.ds(tok_per_slab * c + j, n_chunks, stride=cn), :] for j in range(tok_per_slab)]
        s = jnp.concatenate(slabs, axis=1).astype(BF16)
        up = jnp.dot(s, perm_ref[...], preferred_element_type=F32).astype(BF16)
        for g in range(n_groups):
            ug_scr[g, :, LANES * c:LANES * (c + 1)] = up[:, LANES * g:LANES * (g + 1)]
    xs = (xfr, xfi, xbr, xbi)
    for g in range(n_groups):
        z = jnp.dot(ug_scr[g], w1_ref[g], preferred_element_type=F32)
        yg_scr[g] = z[:, :blk]
        lanes = slice(LANES * (g // 2), LANES * (g // 2 + 1))
        for k, xr in enumerate(xs):
            piece = z[:, blk + LANES * k:blk + LANES * (k + 1)]
            if g % 2 == 0:
                xr[:, lanes] = piece
            else:
                xr[:, lanes] += piece
    arf, aif, arb, aib = a_ref[0:1, :], a_ref[1:2, :], a_ref[2:3, :], a_ref[3:4, :]

    def body(i, carry):
        fr, fi, br, bi = carry
        nb = jnp.where(i < n_ctx_chunks, n_ctx_chunks - 1 - i, n_chunks - 1 - (i - n_ctx_chunks))
        x_fr, x_fi = xfr[pl.ds(i, 1), :], xfi[pl.ds(i, 1), :]
        x_br, x_bi = xbr[pl.ds(nb, 1), :], xbi[pl.ds(nb, 1), :]
        xfr[pl.ds(i, 1), :] = fr
        xfi[pl.ds(i, 1), :] = fi
        xbr[pl.ds(nb, 1), :] = br
        xbi[pl.ds(nb, 1), :] = bi
        return (arf * fr - aif * fi + x_fr, arf * fi + aif * fr + x_fi,
                arb * br - aib * bi + x_br, arb * bi + aib * br + x_bi)

    zero = jnp.zeros((1, xfr.shape[1]), F32)
    lax.fori_loop(0, n_chunks, body, (zero, zero, zero, zero))
    for g in range(n_groups):
        lanes = slice(LANES * (g // 2), LANES * (g // 2 + 1))
        hp = jnp.concatenate([xr[:, lanes] for xr in xs], axis=1).astype(BF16)
        yg_scr[g] += jnp.dot(hp, wo_ref[g], preferred_element_type=F32)
    for c in range(cn // tok_per_slab):
        ycat = jnp.concatenate([yg_scr[g, :, LANES * c:LANES * (c + 1)] for g in range(n_groups)], axis=1)
        hi = ycat.astype(BF16)
        lo = (ycat - hi.astype(F32)).astype(BF16)
        r = _dot_nt(hi, perm_ref[...]) + _dot_nt(lo, perm_ref[...])
        for j in range(tok_per_slab):
            y_ref[0, pl.ds(tok_per_slab * c + j, n_chunks, stride=cn), :] = r[:, LANES * j:LANES * (j + 1)]


def _s5_mix(u, w1, w_out, a_mat, n_ctx):
    bsz, t_len, width = u.shape
    g_n = width // S5_GROUP
    cn = S5_CHUNK
    n_chunks = t_len // cn
    blk = cn * S5_GROUP
    perm = _s5_fold_perm()
    gs = LANES // S5_GROUP
    state_w = gs * S5_STATE
    kern = functools.partial(_s5_kernel, n_chunks=n_chunks, n_ctx_chunks=n_ctx // cn, n_groups=gs)
    return pl.pallas_call(
        kern,
        grid=(bsz, g_n // gs),
        in_specs=[pl.BlockSpec((1, t_len, LANES), lambda b, s: (b, 0, s)),
                  pl.BlockSpec(perm.shape, lambda b, s: (0, 0), pipeline_mode=pl.Buffered(1)),
                  pl.BlockSpec((gs,) + w1.shape[1:], lambda b, s: (s, 0, 0)),
                  pl.BlockSpec((gs,) + w_out.shape[1:], lambda b, s: (s, 0, 0)),
                  pl.BlockSpec((4, state_w), lambda b, s: (0, s))],
        out_specs=pl.BlockSpec((1, t_len, LANES), lambda b, s: (b, 0, s)),
        out_shape=jax.ShapeDtypeStruct((bsz, t_len, width), F32),
        scratch_shapes=[pltpu.VMEM((gs, n_chunks, blk), BF16), pltpu.VMEM((gs, n_chunks, blk), F32)]
        + [pltpu.VMEM((n_chunks, state_w), F32)] * 4,
        compiler_params=_cparams("parallel", "parallel"),
    )(u, perm, w1, w_out, a_mat)


def _head0_lanes(shape):
    return lax.broadcasted_iota(jnp.int32, shape, len(shape) - 1) < HEAD_DIM


def _stack_heads(q):
    m0 = _head0_lanes(q.shape)
    return jnp.concatenate([jnp.where(m0, q, 0.0), jnp.where(m0, 0.0, q)], axis=-2)


def _unstack_heads(o2):
    c = o2.shape[-2] // 2
    top, bottom = o2[..., :c, :], o2[..., c:, :]
    return jnp.where(_head0_lanes(top.shape), top, bottom)


def _pair_block_mask():
    r = lax.broadcasted_iota(jnp.int32, (PAIR, PAIR), 0) < HEAD_DIM
    c = lax.broadcasted_iota(jnp.int32, (PAIR, PAIR), 1) < HEAD_DIM
    return r == c


def _dot_nt(a, b):
    return lax.dot_general(a, b, (((1,), (1,)), ((), ())), preferred_element_type=F32)


def _dot_tn(a, b):
    return lax.dot_general(a, b, (((0,), (0,)), ((), ())), preferred_element_type=F32)


def _bwd_block(j, n_ctx_blocks, n_blocks):
    return jnp.where(j < n_ctx_blocks, n_ctx_blocks - 1 - j, n_blocks - 1 - (j - n_ctx_blocks))


def _bdot_nt(a, b):
    return lax.dot_general(a, b, (((2,), (2,)), ((0,), (0,))), preferred_element_type=F32)


def _bdot_nn(a, b):
    return lax.dot_general(a, b, (((2,), (1,)), ((0,), (0,))), preferred_element_type=F32)


def _hg_block(q, z, v, lb, ht, reverse):
    rows = q.shape[0]
    c = HG_CHUNK
    hh = HG_HALF
    n = rows // c
    f = lb + (1.0 - lb) * jax.nn.sigmoid(z)
    logf = jnp.log(f)
    kk = 1.0 - f
    ri = lax.broadcasted_iota(jnp.int32, (rows, rows), 0)
    ci = lax.broadcasted_iota(jnp.int32, (rows, rows), 1)
    same_chunk = (ri // c) == (ci // c)
    tri = jnp.where(same_chunk & ((ci >= ri) if reverse else (ci <= ri)), 1.0, 0.0).astype(F32)
    b = jnp.dot(tri, logf, precision=HIGHEST, preferred_element_type=F32)
    b4, q4, k4 = (a.reshape(n, c, PAIR) for a in (b, q, kk))
    vb = v.astype(BF16).reshape(n, c, PAIR)
    if reverse:
        first, second = slice(hh, c), slice(0, hh)
        r = b4[:, hh:hh + 1]
        bend = b4[:, 0:1]
    else:
        first, second = slice(0, hh), slice(hh, c)
        r = b4[:, hh - 1:hh]
        bend = b4[:, c - 1:c]
    b1, b2 = b4[:, first], b4[:, second]
    qd1 = q4[:, first] * jnp.exp(b1)
    qd2 = q4[:, second] * jnp.exp(b2 - r)
    k1d = k4[:, first] * jnp.exp(-b1)
    k2 = k4 * jnp.exp(r - b4)
    s1 = _bdot_nt(_stack_heads(qd1).astype(BF16), k1d.astype(BF16))
    s2 = _bdot_nt(_stack_heads(qd2).astype(BF16), k2.astype(BF16))
    t1 = lax.broadcasted_iota(jnp.int32, (n, 2 * hh, hh), 1) % hh
    c1 = lax.broadcasted_iota(jnp.int32, (n, 2 * hh, hh), 2)
    t2 = lax.broadcasted_iota(jnp.int32, (n, 2 * hh, c), 1) % hh
    c2 = lax.broadcasted_iota(jnp.int32, (n, 2 * hh, c), 2)
    if reverse:
        s1 = jnp.where(c1 >= t1, s1, 0.0)
        s2 = jnp.where(c2 >= t2, s2, 0.0)
    else:
        s1 = jnp.where(c1 <= t1, s1, 0.0)
        s2 = jnp.where(c2 <= t2 + hh, s2, 0.0)
    o1 = _unstack_heads(_bdot_nn(s1.astype(BF16), vb[:, first]))
    o2 = _unstack_heads(_bdot_nn(s2.astype(BF16), vb))
    o = jnp.concatenate([o2, o1] if reverse else [o1, o2], axis=1)
    ke = (k4 * jnp.exp(bend - b4)).astype(BF16)
    dec = jnp.exp(bend)
    bmask = _pair_block_mask()
    states = [None] * n
    for ch in (range(n - 1, -1, -1) if reverse else range(n)):
        states[ch] = ht.astype(BF16)
        ht = ht * dec[ch] + jnp.where(bmask, _dot_tn(vb[ch], ke[ch]), 0.0)
    o = o + _bdot_nt((q4 * jnp.exp(b4)).astype(BF16), jnp.stack(states, axis=0))
    return o.reshape(rows, PAIR), ht


def _hg_kernel(qf_ref, zf_ref, vf_ref, qb_ref, zb_ref, vb_ref, lb_ref, of_ref, ob_ref, hf_scr, hb_scr, *, n_pairs):
    @pl.when(pl.program_id(1) == 0)
    def _():
        hf_scr[...] = jnp.zeros_like(hf_scr)
        hb_scr[...] = jnp.zeros_like(hb_scr)

    for p in range(n_pairs):
        sl = slice(p * PAIR, (p + 1) * PAIR)
        o, hn = _hg_block(qf_ref[0, :, sl], zf_ref[0, :, sl], vf_ref[0, :, sl], lb_ref[0:1, sl], hf_scr[p], False)
        of_ref[0, :, sl] = o
        hf_scr[p] = hn
        o, hn = _hg_block(qb_ref[0, :, sl], zb_ref[0, :, sl], vb_ref[0, :, sl], lb_ref[1:2, sl], hb_scr[p], True)
        ob_ref[0, :, sl] = o
        hb_scr[p] = hn


def _hg_mix(hg, lb, n_ctx):
    bsz, t_len, total = hg.shape
    w = total // 5
    n_pairs = w // PAIR
    c = HG_BLOCK
    nc, ncc = t_len // c, n_ctx // c
    bw = functools.partial(_bwd_block, n_ctx_blocks=ncc, n_blocks=nc)

    def fspec(col):
        return pl.BlockSpec((1, c, w), lambda b, j: (b, j, col))

    def bspec(col):
        return pl.BlockSpec((1, c, w), lambda b, j: (b, bw(j), col))

    kern = functools.partial(_hg_kernel, n_pairs=n_pairs)
    return pl.pallas_call(
        kern,
        grid=(bsz, nc),
        in_specs=[fspec(0), fspec(1), fspec(3), bspec(0), bspec(2), bspec(3),
                  pl.BlockSpec((2, w), lambda b, j: (0, 0))],
        out_specs=[pl.BlockSpec((1, c, w), lambda b, j: (b, j, 0)),
                   pl.BlockSpec((1, c, w), lambda b, j: (b, bw(j), 0))],
        out_shape=[jax.ShapeDtypeStruct((bsz, t_len, w), F32)] * 2,
        scratch_shapes=[pltpu.VMEM((n_pairs, PAIR, PAIR), F32)] * 2,
        compiler_params=_cparams("parallel", "arbitrary"),
    )(hg, hg, hg, hg, hg, hg, lb)


def _rope_rows(x, cos, sin):
    hd2 = HEAD_DIM // 2
    lane = lax.broadcasted_iota(jnp.int32, x.shape, 1) % HEAD_DIM
    swapped = jnp.where(lane < hd2, pltpu.roll(x, PAIR - hd2, 1), pltpu.roll(x, hd2, 1))
    return x * cos + swapped * sin


def _ret_kernel(qf_ref, kf_ref, vf_ref, cf_ref, sf_ref, qb_ref, kb_ref, vb_ref, cb_ref, sb_ref,
                d_ref, gqf_ref, gkf_ref, gqb_ref, gkb_ref, gh_ref, of_ref, ob_ref, hf_scr, hb_scr, *, n_pairs):
    @pl.when(pl.program_id(1) == 0)
    def _():
        hf_scr[...] = jnp.zeros_like(hf_scr)
        hb_scr[...] = jnp.zeros_like(hb_scr)

    kscale = HEAD_DIM ** -0.5
    bmask = _pair_block_mask()
    for p in range(n_pairs):
        sl = slice(p * PAIR, (p + 1) * PAIR)
        q = _rope_rows(qf_ref[0, :, sl], cf_ref[...], sf_ref[...])
        k = _rope_rows(kf_ref[0, :, sl] * kscale, cf_ref[...], sf_ref[...])
        vb16 = vf_ref[0, :, sl].astype(BF16)
        s = _dot_nt(_stack_heads(q).astype(BF16), k.astype(BF16)) * d_ref[p]
        o = _unstack_heads(jnp.dot(s.astype(BF16), vb16, preferred_element_type=F32))
        ht = hf_scr[p]
        o = o + _dot_nt((q * gqf_ref[:, sl]).astype(BF16), ht.astype(BF16))
        upd = _dot_tn(vb16, (k * gkf_ref[:, sl]).astype(BF16))
        hf_scr[p] = ht * gh_ref[p] + jnp.where(bmask, upd, 0.0)
        of_ref[0, :, sl] = o

        q = _rope_rows(qb_ref[0, :, sl], cb_ref[...], sb_ref[...])
        k = _rope_rows(kb_ref[0, :, sl] * kscale, cb_ref[...], sb_ref[...])
        vb16 = vb_ref[0, :, sl].astype(BF16)
        ht = hb_scr[p]
        ob_ref[0, :, sl] = _dot_nt((q * gqb_ref[:, sl]).astype(BF16), ht.astype(BF16))
        upd = _dot_tn(vb16, (k * gkb_ref[:, sl]).astype(BF16))
        hb_scr[p] = ht * gh_ref[p] + jnp.where(bmask, upd, 0.0)


def _ret_constants(n_heads, t_len, n_ctx):
    c = RET_BLOCK
    log_gamma = jnp.log(1.0 - 2.0 ** (-5.0 - jnp.arange(n_heads, dtype=F32)))
    t = jnp.arange(c, dtype=F32)
    dist = jnp.abs(t[:, None] - t[None, :])
    dmat = jnp.exp(log_gamma[:, None, None] * dist[None])
    dmat = dmat.reshape(n_heads // 2, 2 * c, c)
    lg_lane = jnp.repeat(log_gamma, HEAD_DIM)[None, :]
    gqf = jnp.exp(lg_lane * (t[:, None] + 1.0))
    gkf = jnp.exp(lg_lane * (c - 1.0 - t[:, None]))
    gqb = jnp.exp(lg_lane * (c - t[:, None]))
    gkb = jnp.exp(lg_lane * t[:, None])
    gh = jnp.exp(lg_lane * float(c)).reshape(n_heads // 2, 1, PAIR)
    gh = jnp.broadcast_to(gh, (n_heads // 2, PAIR, PAIR))
    n_lat = t_len - n_ctx
    rows = n_lat // GRID_W
    row = jnp.repeat(jnp.arange(rows, dtype=F32), GRID_W)
    col = jnp.broadcast_to(jnp.arange(GRID_W, dtype=F32)[None, :], (rows, GRID_W)).reshape(-1)
    n_freq = HEAD_DIM // 4
    inv_freq = ROPE_BASE ** (-jnp.arange(n_freq, dtype=F32) / n_freq)
    ang = jnp.concatenate([row[:, None] * inv_freq, col[:, None] * inv_freq], axis=-1)
    cos_l, sin_l = jnp.cos(ang), jnp.sin(ang)
    cos_h = jnp.concatenate([cos_l, cos_l], axis=-1)
    sin_h = jnp.concatenate([-sin_l, sin_l], axis=-1)
    cos_t = jnp.concatenate([jnp.ones((n_ctx, HEAD_DIM), F32), cos_h], axis=0)
    sin_t = jnp.concatenate([jnp.zeros((n_ctx, HEAD_DIM), F32), sin_h], axis=0)
    cos_t = jnp.concatenate([cos_t, cos_t], axis=-1)
    sin_t = jnp.concatenate([sin_t, sin_t], axis=-1)
    return dmat, gqf, gkf, gqb, gkb, gh, cos_t, sin_t


def _ret_mix(ret, consts, n_ctx):
    dmat, gqf, gkf, gqb, gkb, gh, cos_t, sin_t = consts
    bsz, t_len, total = ret.shape
    w = total // 4
    n_pairs = w // PAIR
    c = RET_BLOCK
    nb, ncb = t_len // c, n_ctx // c
    bw = functools.partial(_bwd_block, n_ctx_blocks=ncb, n_blocks=nb)

    def fspec(col):
        return pl.BlockSpec((1, c, w), lambda b, j: (b, j, col))

    def bspec(col):
        return pl.BlockSpec((1, c, w), lambda b, j: (b, bw(j), col))

    def const(a):
        nd = a.ndim
        return pl.BlockSpec(a.shape, lambda b, j: (0,) * nd)

    rope_f = pl.BlockSpec((c, PAIR), lambda b, j: (j, 0))
    rope_b = pl.BlockSpec((c, PAIR), lambda b, j: (bw(j), 0))
    kern = functools.partial(_ret_kernel, n_pairs=n_pairs)
    return pl.pallas_call(
        kern,
        grid=(bsz, nb),
        in_specs=[fspec(0), fspec(1), fspec(2), rope_f, rope_f, bspec(0), bspec(1), bspec(2), rope_b, rope_b,
                  const(dmat), const(gqf), const(gkf), const(gqb), const(gkb), const(gh)],
        out_specs=[pl.BlockSpec((1, c, w), lambda b, j: (b, j, 0)),
                   pl.BlockSpec((1, c, w), lambda b, j: (b, bw(j), 0))],
        out_shape=[jax.ShapeDtypeStruct((bsz, t_len, w), F32)] * 2,
        scratch_shapes=[pltpu.VMEM((n_pairs, PAIR, PAIR), F32)] * 2,
        compiler_params=_cparams("parallel", "arbitrary"),
    )(ret, ret, ret, cos_t, sin_t, ret, ret, ret, cos_t, sin_t, dmat, gqf, gkf, gqb, gkb, gh)


def _head_norm_gate(o, g, ones_bd):
    sq = o * o
    hi = sq.astype(BF16)
    lo = (sq - hi.astype(F32)).astype(BF16)
    ss = jnp.dot(hi, ones_bd, preferred_element_type=F32) + jnp.dot(lo, ones_bd, preferred_element_type=F32)
    return o * lax.rsqrt(ss * (1.0 / HEAD_DIM) + EPS) * jax.nn.silu(g)


def _route(lg):
    lane = lax.broadcasted_iota(jnp.int32, lg.shape, 1).astype(F32)
    big = float(1 << 20)
    neg = -jnp.inf
    gmask = lane < N_GROUPS
    lgm = jnp.where(gmask, lg, neg)
    gmax = jnp.max(lgm, axis=1, keepdims=True)
    gsum = jnp.sum(jnp.where(gmask, jnp.exp(lgm - gmax), 0.0), axis=1, keepdims=True)
    gp = 1.0 / gsum
    gi = jnp.min(jnp.where(gmask & (lgm == gmax), lane, big), axis=1, keepdims=True)
    lo_lane = N_GROUPS + EXPERTS_PER_GROUP * gi
    emask = (lane >= lo_lane) & (lane < lo_lane + EXPERTS_PER_GROUP)
    l1 = jnp.max(jnp.where(emask, lg, neg), axis=1, keepdims=True)
    i1 = jnp.min(jnp.where(emask & (lg == l1), lane, big), axis=1, keepdims=True)
    mask2 = emask & (lane != i1)
    l2 = jnp.max(jnp.where(mask2, lg, neg), axis=1, keepdims=True)
    i2 = jnp.min(jnp.where(mask2 & (lg == l2), lane, big), axis=1, keepdims=True)
    e21 = jnp.exp(l2 - l1)
    w1 = gp / (1.0 + e21)
    w2 = gp * e21 / (1.0 + e21)
    e1 = i1 - N_GROUPS
    e2 = i2 - N_GROUPS
    return jnp.where(lane == 0, w1, jnp.where(lane == 1, w2, jnp.where(lane == 2, e1, jnp.where(lane == 3, e2, 0.0))))


def _merge_kernel(x_ref, ys_ref, hof_ref, hob_ref, hgg_ref, rof_ref, rob_ref, rgg_ref, gz_ref,
                  ml_ref, mc_ref, g2_ref, wglu_ref, wbs_ref, wbh_ref, wbr_ref, wout_ref, ones_ref,
                  wr_ref, br_ref, xo_ref, h2_ref, rt_ref, *, tm, tiles_per_b, n_ctx):
    d = x_ref.shape[1]
    is_ctx = _is_ctx_rows(tm, tiles_per_b, n_ctx)
    y = jax.nn.gelu(ys_ref[...], approximate=True)
    y = y * jax.nn.sigmoid(jnp.dot(y.astype(BF16), wglu_ref[...], preferred_element_type=F32))
    t_s5 = jnp.dot(y.astype(BF16), wbs_ref[...], preferred_element_type=F32)
    yh = _head_norm_gate(hof_ref[...] + hob_ref[...], hgg_ref[...], ones_ref[...])
    t_hg = jnp.dot(yh.astype(BF16), wbh_ref[...], preferred_element_type=F32)
    yr = _head_norm_gate(rof_ref[...] + rob_ref[...], rgg_ref[...], ones_ref[...])
    t_ret = jnp.dot(yr.astype(BF16), wbr_ref[...], preferred_element_type=F32)
    merged = (jax.nn.sigmoid(gz_ref[:, 0:d]) * t_s5 + jax.nn.sigmoid(gz_ref[:, d:2 * d]) * t_hg
              + jax.nn.sigmoid(gz_ref[:, 2 * d:3 * d]) * t_ret)
    mix = jnp.dot(merged.astype(BF16), wout_ref[...], preferred_element_type=F32)
    xn = x_ref[...] + _mod_rows(ml_ref, mc_ref, 2, is_ctx) * mix
    xo_ref[...] = xn
    h2 = _rmsnorm_rows(xn, g2_ref[...])
    h2 = h2 * (1.0 + _mod_rows(ml_ref, mc_ref, 4, is_ctx)) + _mod_rows(ml_ref, mc_ref, 3, is_ctx)
    h2_ref[...] = h2
    lg = jnp.dot(h2, wr_ref[...], precision=HIGHEST, preferred_element_type=F32) + br_ref[...]
    rt_ref[...] = _route(lg)


def _merge(x2, ys5, hg_of, hg_ob, hg2, ret_of, ret_ob, ret2, gz, mod_l, mod_c, g2, wts, t_len, n_ctx):
    m, d = x2.shape
    w_s5 = ys5.shape[1]
    w_h = hg_of.shape[1]
    tm = _row_tile(t_len)
    tpb = t_len // tm
    wglu, wbs, wbh, wbr, wout, ones_bd, wr, br = wts

    def rows(width, col=0):
        return pl.BlockSpec((tm, width), lambda i: (i, col))

    def const(a):
        nd = a.ndim
        return pl.BlockSpec(a.shape, lambda i: (0,) * nd)

    kern = functools.partial(_merge_kernel, tm=tm, tiles_per_b=tpb, n_ctx=n_ctx)
    return pl.pallas_call(
        kern,
        grid=(m // tm,),
        in_specs=[rows(d), rows(w_s5), rows(w_h), rows(w_h), rows(w_h, 4), rows(w_h), rows(w_h), rows(w_h, 3),
                  rows(3 * d),
                  pl.BlockSpec((1, N_MOD, d), lambda i: (i // tpb, 0, 0)),
                  pl.BlockSpec((1, N_MOD, d), lambda i: (0, 0, 0)),
                  const(g2), const(wglu), const(wbs), const(wbh), const(wbr), const(wout), const(ones_bd),
                  const(wr), const(br)],
        out_specs=[rows(d), rows(d), rows(LANES)],
        out_shape=[jax.ShapeDtypeStruct((m, d), F32), jax.ShapeDtypeStruct((m, d), F32),
                   jax.ShapeDtypeStruct((m, LANES), F32)],
        compiler_params=_cparams("parallel"),
    )(x2, ys5, hg_of, hg_ob, hg2, ret_of, ret_ob, ret2, gz, mod_l, mod_c, g2, wglu, wbs, wbh, wbr, wout,
      ones_bd, wr, br)


def _ffn_kernel(lay_ref, be_ref, nu_ref, x_ref, wg_ref, wu_ref, wd_ref, o_ref, wg_scr, wu_scr, wd_scr):
    i = pl.program_id(0)
    prev = be_ref[jnp.maximum(i - 1, 0)]

    @pl.when((i == 0) | (be_ref[i] != prev))
    def _():
        wg_scr[...] = wg_ref[0, 0].astype(BF16)
        wu_scr[...] = wu_ref[0, 0].astype(BF16)
        wd_scr[...] = wd_ref[0, 0].astype(BF16)

    @pl.when(i < nu_ref[0])
    def _():
        xb = x_ref[...].astype(BF16)
        hid = (jax.nn.silu(jnp.dot(xb, wg_scr[...], preferred_element_type=F32))
               * jnp.dot(xb, wu_scr[...], preferred_element_type=F32))
        o_ref[...] = jnp.dot(hid.astype(BF16), wd_scr[...], preferred_element_type=F32)

    @pl.when(i >= nu_ref[0])
    def _():
        o_ref[...] = jnp.zeros_like(o_ref)


def _expert_ffn(layer, blk_e, n_used, xg, wg, wu, wd):
    n_slots, d = xg.shape
    f = wg.shape[3]
    bm = MOE_BLOCK
    grid_spec = pltpu.PrefetchScalarGridSpec(
        num_scalar_prefetch=3,
        grid=(n_slots // bm,),
        in_specs=[
            pl.BlockSpec((bm, d), lambda i, lay, be, nu: (i, 0)),
            pl.BlockSpec((1, 1, d, f), lambda i, lay, be, nu: (lay[0], be[i], 0, 0)),
            pl.BlockSpec((1, 1, d, f), lambda i, lay, be, nu: (lay[0], be[i], 0, 0)),
            pl.BlockSpec((1, 1, f, d), lambda i, lay, be, nu: (lay[0], be[i], 0, 0)),
        ],
        out_specs=pl.BlockSpec((bm, d), lambda i, lay, be, nu: (i, 0)),
        scratch_shapes=[pltpu.VMEM((d, f), BF16), pltpu.VMEM((d, f), BF16), pltpu.VMEM((f, d), BF16)],
    )
    return pl.pallas_call(
        _ffn_kernel,
        grid_spec=grid_spec,
        out_shape=jax.ShapeDtypeStruct((n_slots, d), F32),
        compiler_params=_cparams("arbitrary"),
    )(layer, blk_e, n_used, xg, wg, wu, wd)


def _combine_kernel(x_ref, y0_ref, y1_ref, rt_ref, ml_ref, mc_ref, o_ref, *, tm, tiles_per_b, n_ctx):
    is_ctx = _is_ctx_rows(tm, tiles_per_b, n_ctx)
    rt = rt_ref[...]
    y = rt[:, 0:1] * y0_ref[...] + rt[:, 1:2] * y1_ref[...]
    o_ref[...] = x_ref[...] + _mod_rows(ml_ref, mc_ref, 5, is_ctx) * y


def _combine(x2, y0, y1, route, mod_l, mod_c, t_len, n_ctx):
    m, d = x2.shape
    tm = _row_tile(t_len)
    tpb = t_len // tm
    kern = functools.partial(_combine_kernel, tm=tm, tiles_per_b=tpb, n_ctx=n_ctx)
    return pl.pallas_call(
        kern,
        grid=(m // tm,),
        in_specs=[pl.BlockSpec((tm, d), lambda i: (i, 0))] * 3
        + [pl.BlockSpec((tm, LANES), lambda i: (i, 0)),
           pl.BlockSpec((1, N_MOD, d), lambda i: (i // tpb, 0, 0)),
           pl.BlockSpec((1, N_MOD, d), lambda i: (0, 0, 0))],
        out_specs=pl.BlockSpec((tm, d), lambda i: (i, 0)),
        out_shape=jax.ShapeDtypeStruct((m, d), F32),
        compiler_params=_cparams("parallel"),
    )(x2, y0, y1, route, mod_l, mod_c)


def _final_norm_kernel(x_ref, g_ref, o_ref):
    o_ref[0] = _rmsnorm_rows(x_ref[0], g_ref[...])


def _final_norm(x3, g, n_ctx):
    bsz, t_len, d = x3.shape
    n_lat = t_len - n_ctx
    tm = n_ctx
    off = n_ctx // tm
    return pl.pallas_call(
        _final_norm_kernel,
        grid=(bsz, n_lat // tm),
        in_specs=[pl.BlockSpec((1, tm, d), lambda b, j: (b, j + off, 0)),
                  pl.BlockSpec((1, d), lambda b, j: (0, 0))],
        out_specs=pl.BlockSpec((1, tm, d), lambda b, j: (b, j, 0)),
        out_shape=jax.ShapeDtypeStruct((bsz, n_lat, d), F32),
        compiler_params=_cparams("parallel", "parallel"),
    )(x3, g.reshape(1, d))


def _dispatch_indices(eid, n_tok):
    a = n_tok * TOP_K
    bm = MOE_BLOCK
    flat_e = eid.reshape(a)
    experts = jnp.arange(N_EXPERTS, dtype=jnp.int32)
    onehot = flat_e[:, None] == experts[None, :]
    counts = jnp.sum(onehot, axis=0, dtype=jnp.int32)
    padded = (counts + bm - 1) // bm * bm
    pad_end = jnp.cumsum(padded)
    pad_start = pad_end - padded
    start = jnp.cumsum(counts) - counts
    order = jnp.argsort(flat_e).astype(jnp.int32)
    inv = jnp.argsort(order).astype(jnp.int32)
    shift = jnp.sum(jnp.where(onehot, (pad_start - start)[None, :], 0), axis=1, dtype=jnp.int32)
    pos = (inv + shift).reshape(n_tok, TOP_K)
    n_blocks = -(-a // bm) + N_EXPERTS
    blk_start = jnp.arange(n_blocks, dtype=jnp.int32) * bm
    blk_e = jnp.minimum(jnp.sum(pad_end[None, :] <= blk_start[:, None], axis=1, dtype=jnp.int32), N_EXPERTS - 1)
    slot = jnp.arange(n_blocks * bm, dtype=jnp.int32)
    slot_shift = jnp.repeat((pad_start - start)[blk_e], bm)
    src = jnp.clip(slot - slot_shift, 0, a - 1)
    slot_tok = order.at[src].get(mode='promise_in_bounds') // TOP_K
    n_used = (pad_end[-1] // bm).astype(jnp.int32).reshape(1)
    return slot_tok, pos, blk_e, n_used


def _moe(layer, h2, route, wg, wu, wd):
    n_tok = h2.shape[0]
    eid = route[:, 2:4].astype(jnp.int32)
    slot_tok, pos, blk_e, n_used = _dispatch_indices(eid, n_tok)
    xg = h2.at[slot_tok].get(mode='promise_in_bounds')
    yb = _expert_ffn(layer, blk_e, n_used, xg, wg, wu, wd)
    return yb.at[pos[:, 0]].get(mode='promise_in_bounds'), yb.at[pos[:, 1]].get(mode='promise_in_bounds')


def _mm_tiles(t_len, n):
    tm = _row_tile(t_len)
    for tn in (1536, 1024, 768, 640, 512, 384, 256, 128):
        if n % tn == 0:
            return tm, tn
    return tm, n


def kernel(x, c, ctx, c_ctx, w_ada, b_ada, norm1_g, norm2_g, w_in, s5_lam_re, s5_lam_im, s5_log_dt, s5_b_re, s5_b_im, s5_c_re, s5_c_im, s5_d, s5_w_glu, hgrn_lb_raw, w_branch_s5, w_branch_hgrn, w_branch_ret, w_out, moe_w_group, moe_b_group, moe_w_expert, moe_b_expert, moe_w_gate, moe_w_up, moe_w_down, final_norm_g):
    bsz, n_lat, d = x.shape
    n_ctx = ctx.shape[1]
    t_len = n_ctx + n_lat
    depth = w_ada.shape[0]
    m = bsz * t_len
    w_s5 = s5_d.shape[1]
    w_hg = w_branch_hgrn.shape[1]
    w_ret = w_branch_ret.shape[1]
    n_heads = w_ret // HEAD_DIM

    pad_rows = (-(bsz + 1)) % SUBLANES
    cc = jnp.concatenate([c, c_ctx[None, :], jnp.zeros((pad_rows, d), F32)], axis=0)
    mods = _ada_tables(cc, w_ada, b_ada).reshape(depth, bsz + 1 + pad_rows, N_MOD, d)

    p_lb = jax.nn.softmax(hgrn_lb_raw.astype(F32), axis=0)
    lower_bounds = jnp.cumsum(p_lb, axis=0) - p_lb[0:1]
    ret_consts = _ret_constants(n_heads, t_len, n_ctx)
    ones_bd = jnp.kron(jnp.eye(w_hg // HEAD_DIM, dtype=F32), jnp.ones((HEAD_DIM, HEAD_DIM), F32)).astype(BF16)

    xs = jnp.concatenate([ctx, x], axis=1).reshape(m, d)
    off_hg = w_s5
    off_ret = off_hg + 5 * w_hg
    off_gz = off_ret + 4 * w_ret
    for l in range(depth):
        mod_l = mods[l, :bsz]
        mod_c = mods[l, bsz:bsz + 1]
        h = _norm_mod(xs, norm1_g[l], mod_l, mod_c, t_len, n_ctx)
        wl = w_in[l].astype(BF16)
        u = _matmul(h, wl[:, :off_hg], F32, *_mm_tiles(t_len, w_s5))
        hg = _matmul(h, wl[:, off_hg:off_ret], F32, *_mm_tiles(t_len, 5 * w_hg))
        ret = _matmul(h, wl[:, off_ret:off_gz], F32, *_mm_tiles(t_len, 4 * w_ret))
        gz = _matmul(h, wl[:, off_gz:], F32, *_mm_tiles(t_len, 3 * d))

        s5w = _s5_weights(s5_lam_re[l], s5_lam_im[l], s5_log_dt[l], s5_b_re[l], s5_b_im[l],
                          s5_c_re[l], s5_c_im[l], s5_d[l])
        ys5 = _s5_mix(u.reshape(bsz, t_len, w_s5), *s5w, n_ctx).reshape(m, w_s5)
        hg_of, hg_ob = _hg_mix(hg.reshape(bsz, t_len, 5 * w_hg), lower_bounds[l], n_ctx)
        ret_of, ret_ob = _ret_mix(ret.reshape(bsz, t_len, 4 * w_ret), ret_consts, n_ctx)

        wr = jnp.concatenate([moe_w_group[l], moe_w_expert[l],
                              jnp.zeros((d, LANES - N_GROUPS - N_EXPERTS), F32)], axis=1)
        br = jnp.concatenate([moe_b_group[l], moe_b_expert[l],
                              jnp.zeros((LANES - N_GROUPS - N_EXPERTS,), F32)])[None, :]
        wts = (s5_w_glu[l].astype(BF16), w_branch_s5[l].astype(BF16), w_branch_hgrn[l].astype(BF16),
               w_branch_ret[l].astype(BF16), w_out[l].astype(BF16), ones_bd, wr, br)
        xs, h2, route = _merge(xs, ys5, hg_of.reshape(m, w_hg), hg_ob.reshape(m, w_hg), hg,
                               ret_of.reshape(m, w_ret), ret_ob.reshape(m, w_ret), ret, gz,
                               mod_l, mod_c, norm2_g[l].reshape(1, d), wts, t_len, n_ctx)
        y0, y1 = _moe(jnp.full((1,), l, jnp.int32), h2, route, moe_w_gate, moe_w_up, moe_w_down)
        xs = _combine(xs, y0, y1, route, mod_l, mod_c, t_len, n_ctx)
    return _final_norm(xs.reshape(bsz, t_len, d), final_norm_g, n_ctx)
```

```python
import functools

import jax
import jax.numpy as jnp
import numpy as np
from jax import lax
from jax.experimental import pallas as pl
from jax.experimental.pallas import tpu as pltpu

F32 = jnp.float32
BF16 = jnp.bfloat16
HIGHEST = lax.Precision.HIGHEST

LANES = 128
SUBLANES = 8
VMEM_LIMIT = 56 * 1024 * 1024

EPS = 1e-6
N_MOD = 6
GRID_W = 64
HEAD_DIM = 64
PAIR = 2 * HEAD_DIM
S5_GROUP = 16
S5_STATE = 64
S5_CHUNK = 16
HG_CHUNK = 64
HG_HALF = HG_CHUNK // 2
HG_BLOCK = 256
RET_BLOCK = 256
ROPE_BASE = 10000.0
N_GROUPS = 4
EXPERTS_PER_GROUP = 8
N_EXPERTS = N_GROUPS * EXPERTS_PER_GROUP
TOP_K = 2
MOE_BLOCK = 512


def _cparams(*sem):
    return pltpu.CompilerParams(dimension_semantics=sem, vmem_limit_bytes=VMEM_LIMIT)


def _row_tile(t_len):
    for tm in (544, 512, 384, 272, 256, 128, 64, 32, 16, 8):
        if t_len % tm == 0:
            return tm
    raise ValueError(f"unsupported stream length {t_len}")


def _ada_kernel(c_ref, w_ref, b_ref, o_ref):
    a = jax.nn.silu(c_ref[...]).astype(BF16)
    o_ref[0] = jnp.dot(a, w_ref[0].astype(BF16), preferred_element_type=F32) + b_ref[0]


def _ada_tables(cc, w_ada, b_ada):
    depth, d, n = w_ada.shape
    rows = cc.shape[0]
    tn = 1536 if n % 1536 == 0 else n
    return pl.pallas_call(
        _ada_kernel,
        grid=(depth, n // tn),
        in_specs=[
            pl.BlockSpec((rows, d), lambda l, j: (0, 0)),
            pl.BlockSpec((1, d, tn), lambda l, j: (l, 0, j)),
            pl.BlockSpec((1, 1, tn), lambda l, j: (l, 0, j)),
        ],
        out_specs=pl.BlockSpec((1, rows, tn), lambda l, j: (l, 0, j)),
        out_shape=jax.ShapeDtypeStruct((depth, rows, n), F32),
        compiler_params=_cparams("parallel", "parallel"),
    )(cc, w_ada, b_ada.reshape(depth, 1, n))


def _mod_rows(ml_ref, mc_ref, idx, is_ctx):
    return jnp.where(is_ctx, mc_ref[0, idx:idx + 1, :], ml_ref[0, idx:idx + 1, :])


def _is_ctx_rows(tm, tiles_per_b, n_ctx):
    jt = pl.program_id(0) % tiles_per_b
    row = jt * tm + lax.broadcasted_iota(jnp.int32, (tm, 1), 0)
    return row < n_ctx


def _rmsnorm_rows(x, g):
    return x * lax.rsqrt(jnp.mean(x * x, axis=-1, keepdims=True) + EPS) * g


def _proj_kernel(x_ref, g_ref, ml_ref, mc_ref, wu_ref, wh_ref, wr_ref, wg_ref,
                 u_ref, hg_ref, ret_ref, gz_ref, *, tm, tiles_per_b, n_ctx):
    is_ctx = _is_ctx_rows(tm, tiles_per_b, n_ctx)
    y = _rmsnorm_rows(x_ref[...], g_ref[...])
    y = y * (1.0 + _mod_rows(ml_ref, mc_ref, 1, is_ctx)) + _mod_rows(ml_ref, mc_ref, 0, is_ctx)
    h = y.astype(BF16)
    u_ref[...] = jnp.dot(h, wu_ref[...], preferred_element_type=F32)
    hg_ref[...] = jnp.dot(h, wh_ref[...], preferred_element_type=F32)
    ret_ref[...] = jnp.dot(h, wr_ref[...], preferred_element_type=F32)
    gz_ref[...] = jnp.dot(h, wg_ref[...], preferred_element_type=F32).astype(gz_ref.dtype)


def _project(x2, g, mod_l, mod_c, w_parts, t_len, n_ctx):
    m, d = x2.shape
    tm = _row_tile(t_len)
    tpb = t_len // tm
    kern = functools.partial(_proj_kernel, tm=tm, tiles_per_b=tpb, n_ctx=n_ctx)

    def const(a):
        return pl.BlockSpec(a.shape, lambda i: (0, 0), pipeline_mode=pl.Buffered(1))

    out_dtypes = (F32, F32, F32, BF16)
    return pl.pallas_call(
        kern,
        grid=(m // tm,),
        in_specs=[
            pl.BlockSpec((tm, d), lambda i: (i, 0)),
            pl.BlockSpec((1, d), lambda i: (0, 0)),
            pl.BlockSpec((1, N_MOD, d), lambda i: (i // tpb, 0, 0)),
            pl.BlockSpec((1, N_MOD, d), lambda i: (0, 0, 0)),
        ] + [const(w) for w in w_parts],
        out_specs=[pl.BlockSpec((tm, w.shape[1]), lambda i: (i, 0)) for w in w_parts],
        out_shape=[jax.ShapeDtypeStruct((m, w.shape[1]), dt) for w, dt in zip(w_parts, out_dtypes)],
        compiler_params=_cparams("parallel"),
    )(x2, g.reshape(1, d), mod_l, mod_c, *w_parts)


def _s5_weights(lam_re, lam_im, log_dt, b_re, b_im, c_re, c_im, d_skip):
    n_dir, g_n, p_n = lam_re.shape
    h_n = b_re.shape[-1]
    cn = S5_CHUNK
    lr, li = lam_re.astype(F32), lam_im.astype(F32)
    dt = jnp.exp(log_dt.astype(F32))[..., None]
    mag = jnp.exp(lr * dt)
    ar = mag * jnp.cos(li * dt)
    ai = mag * jnp.sin(li * dt)
    den = lr * lr + li * li
    zr = ((ar - 1.0) * lr + ai * li) / den
    zi = (ai * lr - (ar - 1.0) * li) / den
    bbr = zr[..., None] * b_re - zi[..., None] * b_im
    bbi = zr[..., None] * b_im + zi[..., None] * b_re
    j = jnp.arange(cn + 1, dtype=F32)[:, None, None, None]
    pmag = jnp.exp(lr[None] * dt[None] * j)
    pang = li[None] * dt[None] * j
    pr = pmag * jnp.cos(pang)
    pi = pmag * jnp.sin(pang)
    wr = pr[..., None] * bbr[None] - pi[..., None] * bbi[None]
    wi = pr[..., None] * bbi[None] + pi[..., None] * bbr[None]
    kj = (jnp.einsum('dghp,jdgpk->jdghk', c_re, wr, precision=HIGHEST)
          - jnp.einsum('dghp,jdgpk->jdghk', c_im, wi, precision=HIGHEST))
    s_idx = jnp.arange(cn)[:, None]
    t_idx = jnp.arange(cn)[None, :]
    lag = t_idx - s_idx
    kf = kj[:cn, 0][jnp.clip(lag, 0, cn - 1)]
    kb = kj[:cn, 1][jnp.clip(-lag, 0, cn - 1)]
    kf = jnp.where((lag >= 0)[:, :, None, None, None], kf, 0.0)
    kb = jnp.where((lag <= 0)[:, :, None, None, None], kb, 0.0)
    dsk = d_skip.astype(F32).reshape(g_n, h_n)
    eye_h = jnp.eye(h_n, dtype=F32)
    diag = (lag == 0)[:, :, None, None, None] * (dsk[None, None, :, :, None] * eye_h[None, None, None])
    m_full = kf + kb + diag
    m_mat = m_full.transpose(2, 0, 4, 1, 3).reshape(g_n, cn * h_n, cn * h_n)
    wf_r = wr[:cn, 0][::-1]
    wf_i = wi[:cn, 0][::-1]
    wb_r = wr[:cn, 1]
    wb_i = wi[:cn, 1]
    w_in = jnp.stack([wf_r, wf_i, wb_r, wb_i], axis=0)
    w_in = w_in.transpose(2, 1, 4, 0, 3).reshape(g_n, cn * h_n, 4 * p_n)
    def out_map(d, powers):
        prd, pid = pr[powers, d], pi[powers, d]
        w_re = c_re[d][None] * prd[:, :, None, :] - c_im[d][None] * pid[:, :, None, :]
        w_im = c_re[d][None] * pid[:, :, None, :] + c_im[d][None] * prd[:, :, None, :]
        return w_re, -w_im
    t_arr = jnp.arange(cn)
    of_r, of_i = out_map(0, t_arr + 1)
    ob_r, ob_i = out_map(1, cn - t_arr)
    w_out = jnp.stack([of_r, of_i, ob_r, ob_i], axis=0)
    w_out = w_out.transpose(2, 0, 4, 1, 3).reshape(g_n, 4 * p_n, cn * h_n)
    slot = jax.nn.one_hot(jnp.arange(g_n) % 2, 2, dtype=F32)
    w_in = w_in.reshape(g_n, cn * h_n, 4, 1, p_n) * slot[:, None, None, :, None]
    w_in = w_in.reshape(g_n, cn * h_n, 8 * p_n)
    w_out = w_out.reshape(g_n, 4, 1, p_n, cn * h_n) * slot[:, None, :, None, None]
    w_out = w_out.reshape(g_n, 8 * p_n, cn * h_n)
    a_mat = jnp.stack([pr[cn, 0].reshape(-1), pi[cn, 0].reshape(-1),
                       pr[cn, 1].reshape(-1), pi[cn, 1].reshape(-1)], axis=0)
    w1 = jnp.concatenate([m_mat, w_in], axis=-1).astype(BF16)
    return w1, w_out.astype(BF16), a_mat


def _s5_fold_perm():
    j, g8, h = np.meshgrid(np.arange(8), np.arange(8), np.arange(S5_GROUP), indexing='ij')
    src = (j * 8 + g8) * S5_GROUP + h
    dst = (g8 * 8 + j) * S5_GROUP + h
    perm = np.zeros((8 * LANES, 8 * LANES), np.float32)
    perm[src.reshape(-1), dst.reshape(-1)] = 1.0
    return jnp.asarray(perm, dtype=BF16)


def _s5_kernel(u_ref, perm_ref, w1_ref, wo_ref, a_ref, y_ref, ug_scr, yg_scr, xfr, xfi, xbr, xbi,
               *, n_chunks, n_ctx_chunks, n_groups):
    cn = S5_CHUNK
    blk = cn * S5_GROUP
    tok_per_slab = LANES // S5_GROUP
    for c in range(cn // tok_per_slab):
        slabs = [u_ref[0, pl.ds(tok_per_slab * c + j, n_chunks, stride=cn), :] for j in range(tok_per_slab)]
        s = jnp.concatenate(slabs, axis=1).astype(BF16)
        up = jnp.dot(s, perm_ref[...], preferred_element_type=F32).astype(BF16)
        for g in range(n_groups):
            ug_scr[g, :, LANES * c:LANES * (c + 1)] = up[:, LANES * g:LANES * (g + 1)]
    xs = (xfr, xfi, xbr, xbi)
    for g in range(n_groups):
        z = jnp.dot(ug_scr[g], w1_ref[g], preferred_element_type=F32)
        yg_scr[g] = z[:, :blk]
        lanes = slice(LANES * (g // 2), LANES * (g // 2 + 1))
        for k, xr in enumerate(xs):
            piece = z[:, blk + LANES * k:blk + LANES * (k + 1)]
            if g % 2 == 0:
                xr[:, lanes] = piece
            else:
                xr[:, lanes] += piece
    arf, aif, arb, aib = a_ref[0:1, :], a_ref[1:2, :], a_ref[2:3, :], a_ref[3:4, :]

    def body(i, carry):
        fr, fi, br, bi = carry
        nb = jnp.where(i < n_ctx_chunks, n_ctx_chunks - 1 - i, n_chunks - 1 - (i - n_ctx_chunks))
        x_fr, x_fi = xfr[pl.ds(i, 1), :], xfi[pl.ds(i, 1), :]
        x_br, x_bi = xbr[pl.ds(nb, 1), :], xbi[pl.ds(nb, 1), :]
        xfr[pl.ds(i, 1), :] = fr
        xfi[pl.ds(i, 1), :] = fi
        xbr[pl.ds(nb, 1), :] = br
        xbi[pl.ds(nb, 1), :] = bi
        return (arf * fr - aif * fi + x_fr, arf * fi + aif * fr + x_fi,
                arb * br - aib * bi + x_br, arb * bi + aib * br + x_bi)

    zero = jnp.zeros((1, xfr.shape[1]), F32)
    lax.fori_loop(0, n_chunks, body, (zero, zero, zero, zero))
    for g in range(n_groups):
        lanes = slice(LANES * (g // 2), LANES * (g // 2 + 1))
        hp = jnp.concatenate([xr[:, lanes] for xr in xs], axis=1).astype(BF16)
        yg_scr[g] += jnp.dot(hp, wo_ref[g], preferred_element_type=F32)
    for c in range(cn // tok_per_slab):
        ycat = jnp.concatenate([yg_scr[g, :, LANES * c:LANES * (c + 1)] for g in range(n_groups)], axis=1)
        hi = ycat.astype(BF16)
        lo = (ycat - hi.astype(F32)).astype(BF16)
        r = _dot_nt(hi, perm_ref[...]) + _dot_nt(lo, perm_ref[...])
        for j in range(tok_per_slab):
            y_ref[0, pl.ds(tok_per_slab * c + j, n_chunks, stride=cn), :] = r[:, LANES * j:LANES * (j + 1)]


def _s5_mix(u, w1, w_out, a_mat, n_ctx):
    bsz, t_len, width = u.shape
    g_n = width // S5_GROUP
    cn = S5_CHUNK
    n_chunks = t_len // cn
    blk = cn * S5_GROUP
    perm = _s5_fold_perm()
    gs = LANES // S5_GROUP
    state_w = gs * S5_STATE
    kern = functools.partial(_s5_kernel, n_chunks=n_chunks, n_ctx_chunks=n_ctx // cn, n_groups=gs)
    return pl.pallas_call(
        kern,
        grid=(bsz, g_n // gs),
        in_specs=[pl.BlockSpec((1, t_len, LANES), lambda b, s: (b, 0, s)),
                  pl.BlockSpec(perm.shape, lambda b, s: (0, 0), pipeline_mode=pl.Buffered(1)),
                  pl.BlockSpec((gs,) + w1.shape[1:], lambda b, s: (s, 0, 0)),
                  pl.BlockSpec((gs,) + w_out.shape[1:], lambda b, s: (s, 0, 0)),
                  pl.BlockSpec((4, state_w), lambda b, s: (0, s))],
        out_specs=pl.BlockSpec((1, t_len, LANES), lambda b, s: (b, 0, s)),
        out_shape=jax.ShapeDtypeStruct((bsz, t_len, width), F32),
        scratch_shapes=[pltpu.VMEM((gs, n_chunks, blk), BF16), pltpu.VMEM((gs, n_chunks, blk), F32)]
        + [pltpu.VMEM((n_chunks, state_w), F32)] * 4,
        compiler_params=_cparams("parallel", "parallel"),
    )(u, perm, w1, w_out, a_mat)


def _head0_lanes(shape):
    return lax.broadcasted_iota(jnp.int32, shape, len(shape) - 1) < HEAD_DIM


def _stack_heads(q):
    m0 = _head0_lanes(q.shape)
    return jnp.concatenate([jnp.where(m0, q, 0.0), jnp.where(m0, 0.0, q)], axis=-2)


def _unstack_heads(o2):
    c = o2.shape[-2] // 2
    top, bottom = o2[..., :c, :], o2[..., c:, :]
    return jnp.where(_head0_lanes(top.shape), top, bottom)


def _pair_block_mask():
    r = lax.broadcasted_iota(jnp.int32, (PAIR, PAIR), 0) < HEAD_DIM
    c = lax.broadcasted_iota(jnp.int32, (PAIR, PAIR), 1) < HEAD_DIM
    return r == c


def _dot_nt(a, b):
    return lax.dot_general(a, b, (((1,), (1,)), ((), ())), preferred_element_type=F32)


def _dot_tn(a, b):
    return lax.dot_general(a, b, (((0,), (0,)), ((), ())), preferred_element_type=F32)


def _bwd_block(j, n_ctx_blocks, n_blocks):
    return jnp.where(j < n_ctx_blocks, n_ctx_blocks - 1 - j, n_blocks - 1 - (j - n_ctx_blocks))


def _bdot_nt(a, b):
    return lax.dot_general(a, b, (((2,), (2,)), ((0,), (0,))), preferred_element_type=F32)


def _bdot_nn(a, b):
    return lax.dot_general(a, b, (((2,), (1,)), ((0,), (0,))), preferred_element_type=F32)


def _hg_block(q, z, v, lb, ht, reverse):
    rows = q.shape[0]
    c = HG_CHUNK
    hh = HG_HALF
    n = rows // c
    f = lb + (1.0 - lb) * jax.nn.sigmoid(z)
    logf = jnp.log(f)
    kk = 1.0 - f
    ri = lax.broadcasted_iota(jnp.int32, (rows, rows), 0)
    ci = lax.broadcasted_iota(jnp.int32, (rows, rows), 1)
    same_chunk = (ri // c) == (ci // c)
    tri = jnp.where(same_chunk & ((ci >= ri) if reverse else (ci <= ri)), 1.0, 0.0).astype(BF16)
    l1 = logf.astype(BF16)
    r1 = logf - l1.astype(F32)
    l2 = r1.astype(BF16)
    l3 = (r1 - l2.astype(F32)).astype(BF16)
    b3 = jnp.dot(tri, jnp.concatenate([l1, l2, l3], axis=1), preferred_element_type=F32)
    b = b3[:, 0:PAIR] + b3[:, PAIR:2 * PAIR] + b3[:, 2 * PAIR:3 * PAIR]
    b4, q4, k4 = (a.reshape(n, c, PAIR) for a in (b, q, kk))
    vb = v.astype(BF16).reshape(n, c, PAIR)
    if reverse:
        first, second = slice(hh, c), slice(0, hh)
        r = b4[:, hh:hh + 1]
        bend = b4[:, 0:1]
    else:
        first, second = slice(0, hh), slice(hh, c)
        r = b4[:, hh - 1:hh]
        bend = b4[:, c - 1:c]
    b1, b2 = b4[:, first], b4[:, second]
    qd1 = q4[:, first] * jnp.exp(b1)
    qd2 = q4[:, second] * jnp.exp(b2 - r)
    k1d = k4[:, first] * jnp.exp(-b1)
    k2 = k4 * jnp.exp(r - b4)
    s1 = _bdot_nt(_stack_heads(qd1).astype(BF16), k1d.astype(BF16))
    s2 = _bdot_nt(_stack_heads(qd2).astype(BF16), k2.astype(BF16))
    t1 = lax.broadcasted_iota(jnp.int32, (n, 2 * hh, hh), 1) % hh
    c1 = lax.broadcasted_iota(jnp.int32, (n, 2 * hh, hh), 2)
    t2 = lax.broadcasted_iota(jnp.int32, (n, 2 * hh, c), 1) % hh
    c2 = lax.broadcasted_iota(jnp.int32, (n, 2 * hh, c), 2)
    if reverse:
        s1 = jnp.where(c1 >= t1, s1, 0.0)
        s2 = jnp.where(c2 >= t2, s2, 0.0)
    else:
        s1 = jnp.where(c1 <= t1, s1, 0.0)
        s2 = jnp.where(c2 <= t2 + hh, s2, 0.0)
    o1 = _unstack_heads(_bdot_nn(s1.astype(BF16), vb[:, first]))
    o2 = _unstack_heads(_bdot_nn(s2.astype(BF16), vb))
    o = jnp.concatenate([o2, o1] if reverse else [o1, o2], axis=1)
    ke = (k4 * jnp.exp(bend - b4)).astype(BF16)
    dec = jnp.exp(bend)
    bmask = _pair_block_mask()
    states = [None] * n
    for ch in (range(n - 1, -1, -1) if reverse else range(n)):
        states[ch] = ht.astype(BF16)
        ht = ht * dec[ch] + jnp.where(bmask, _dot_tn(vb[ch], ke[ch]), 0.0)
    o = o + _bdot_nt((q4 * jnp.exp(b4)).astype(BF16), jnp.stack(states, axis=0))
    return o.reshape(rows, PAIR), ht


def _hg_kernel(qf_ref, zf_ref, vf_ref, qb_ref, zb_ref, vb_ref, lb_ref, of_ref, ob_ref, hf_scr, hb_scr, *, n_pairs):
    @pl.when(pl.program_id(1) == 0)
    def _():
        hf_scr[...] = jnp.zeros_like(hf_scr)
        hb_scr[...] = jnp.zeros_like(hb_scr)

    for p in range(n_pairs):
        sl = slice(p * PAIR, (p + 1) * PAIR)
        o, hn = _hg_block(qf_ref[0, :, sl], zf_ref[0, :, sl], vf_ref[0, :, sl], lb_ref[0:1, sl], hf_scr[p], False)
        of_ref[0, :, sl] = o
        hf_scr[p] = hn
        o, hn = _hg_block(qb_ref[0, :, sl], zb_ref[0, :, sl], vb_ref[0, :, sl], lb_ref[1:2, sl], hb_scr[p], True)
        ob_ref[0, :, sl] = o
        hb_scr[p] = hn


def _hg_mix(hg, lb, n_ctx):
    bsz, t_len, total = hg.shape
    w = total // 5
    n_pairs = w // PAIR
    c = HG_BLOCK
    nc, ncc = t_len // c, n_ctx // c
    bw = functools.partial(_bwd_block, n_ctx_blocks=ncc, n_blocks=nc)

    def fspec(col):
        return pl.BlockSpec((1, c, w), lambda b, j: (b, j, col))

    def bspec(col):
        return pl.BlockSpec((1, c, w), lambda b, j: (b, bw(j), col))

    kern = functools.partial(_hg_kernel, n_pairs=n_pairs)
    return pl.pallas_call(
        kern,
        grid=(bsz, nc),
        in_specs=[fspec(0), fspec(1), fspec(3), bspec(0), bspec(2), bspec(3),
                  pl.BlockSpec((2, w), lambda b, j: (0, 0))],
        out_specs=[pl.BlockSpec((1, c, w), lambda b, j: (b, j, 0)),
                   pl.BlockSpec((1, c, w), lambda b, j: (b, bw(j), 0))],
        out_shape=[jax.ShapeDtypeStruct((bsz, t_len, w), F32)] * 2,
        scratch_shapes=[pltpu.VMEM((n_pairs, PAIR, PAIR), F32)] * 2,
        compiler_params=_cparams("parallel", "arbitrary"),
    )(hg, hg, hg, hg, hg, hg, lb)


def _rope_rows(x, cos, sin):
    hd2 = HEAD_DIM // 2
    lane = lax.broadcasted_iota(jnp.int32, x.shape, 1) % HEAD_DIM
    swapped = jnp.where(lane < hd2, pltpu.roll(x, PAIR - hd2, 1), pltpu.roll(x, hd2, 1))
    return x * cos + swapped * sin


def _ret_kernel(qf_ref, kf_ref, vf_ref, cf_ref, sf_ref, qb_ref, kb_ref, vb_ref, cb_ref, sb_ref,
                d_ref, gqf_ref, gkf_ref, gqb_ref, gkb_ref, gh_ref, of_ref, ob_ref, hf_scr, hb_scr, *, n_pairs):
    @pl.when(pl.program_id(1) == 0)
    def _():
        hf_scr[...] = jnp.zeros_like(hf_scr)
        hb_scr[...] = jnp.zeros_like(hb_scr)

    kscale = HEAD_DIM ** -0.5
    bmask = _pair_block_mask()
    for p in range(n_pairs):
        sl = slice(p * PAIR, (p + 1) * PAIR)
        q = _rope_rows(qf_ref[0, :, sl], cf_ref[...], sf_ref[...])
        k = _rope_rows(kf_ref[0, :, sl] * kscale, cf_ref[...], sf_ref[...])
        vb16 = vf_ref[0, :, sl].astype(BF16)
        s = _dot_nt(_stack_heads(q).astype(BF16), k.astype(BF16)) * d_ref[p]
        o = _unstack_heads(jnp.dot(s.astype(BF16), vb16, preferred_element_type=F32))
        ht = hf_scr[p]
        o = o + _dot_nt((q * gqf_ref[:, sl]).astype(BF16), ht.astype(BF16))
        upd = _dot_tn(vb16, (k * gkf_ref[:, sl]).astype(BF16))
        hf_scr[p] = ht * gh_ref[p] + jnp.where(bmask, upd, 0.0)
        of_ref[0, :, sl] = o

        q = _rope_rows(qb_ref[0, :, sl], cb_ref[...], sb_ref[...])
        k = _rope_rows(kb_ref[0, :, sl] * kscale, cb_ref[...], sb_ref[...])
        vb16 = vb_ref[0, :, sl].astype(BF16)
        ht = hb_scr[p]
        ob_ref[0, :, sl] = _dot_nt((q * gqb_ref[:, sl]).astype(BF16), ht.astype(BF16))
        upd = _dot_tn(vb16, (k * gkb_ref[:, sl]).astype(BF16))
        hb_scr[p] = ht * gh_ref[p] + jnp.where(bmask, upd, 0.0)


def _ret_constants(n_heads, t_len, n_ctx):
    c = RET_BLOCK
    log_gamma = jnp.log(1.0 - 2.0 ** (-5.0 - jnp.arange(n_heads, dtype=F32)))
    t = jnp.arange(c, dtype=F32)
    dist = jnp.abs(t[:, None] - t[None, :])
    dmat = jnp.exp(log_gamma[:, None, None] * dist[None])
    dmat = dmat.reshape(n_heads // 2, 2 * c, c)
    lg_lane = jnp.repeat(log_gamma, HEAD_DIM)[None, :]
    gqf = jnp.exp(lg_lane * (t[:, None] + 1.0))
    gkf = jnp.exp(lg_lane * (c - 1.0 - t[:, None]))
    gqb = jnp.exp(lg_lane * (c - t[:, None]))
    gkb = jnp.exp(lg_lane * t[:, None])
    gh = jnp.exp(lg_lane * float(c)).reshape(n_heads // 2, 1, PAIR)
    gh = jnp.broadcast_to(gh, (n_heads // 2, PAIR, PAIR))
    n_lat = t_len - n_ctx
    rows = n_lat // GRID_W
    row = jnp.repeat(jnp.arange(rows, dtype=F32), GRID_W)
    col = jnp.broadcast_to(jnp.arange(GRID_W, dtype=F32)[None, :], (rows, GRID_W)).reshape(-1)
    n_freq = HEAD_DIM // 4
    inv_freq = ROPE_BASE ** (-jnp.arange(n_freq, dtype=F32) / n_freq)
    ang = jnp.concatenate([row[:, None] * inv_freq, col[:, None] * inv_freq], axis=-1)
    cos_l, sin_l = jnp.cos(ang), jnp.sin(ang)
    cos_h = jnp.concatenate([cos_l, cos_l], axis=-1)
    sin_h = jnp.concatenate([-sin_l, sin_l], axis=-1)
    cos_t = jnp.concatenate([jnp.ones((n_ctx, HEAD_DIM), F32), cos_h], axis=0)
    sin_t = jnp.concatenate([jnp.zeros((n_ctx, HEAD_DIM), F32), sin_h], axis=0)
    cos_t = jnp.concatenate([cos_t, cos_t], axis=-1)
    sin_t = jnp.concatenate([sin_t, sin_t], axis=-1)
    return dmat, gqf, gkf, gqb, gkb, gh, cos_t, sin_t


def _ret_mix(ret, consts, n_ctx):
    dmat, gqf, gkf, gqb, gkb, gh, cos_t, sin_t = consts
    bsz, t_len, total = ret.shape
    w = total // 4
    n_pairs = w // PAIR
    c = RET_BLOCK
    nb, ncb = t_len // c, n_ctx // c
    bw = functools.partial(_bwd_block, n_ctx_blocks=ncb, n_blocks=nb)

    def fspec(col):
        return pl.BlockSpec((1, c, w), lambda b, j: (b, j, col))

    def bspec(col):
        return pl.BlockSpec((1, c, w), lambda b, j: (b, bw(j), col))

    def const(a):
        nd = a.ndim
        return pl.BlockSpec(a.shape, lambda b, j: (0,) * nd)

    rope_f = pl.BlockSpec((c, PAIR), lambda b, j: (j, 0))
    rope_b = pl.BlockSpec((c, PAIR), lambda b, j: (bw(j), 0))
    kern = functools.partial(_ret_kernel, n_pairs=n_pairs)
    return pl.pallas_call(
        kern,
        grid=(bsz, nb),
        in_specs=[fspec(0), fspec(1), fspec(2), rope_f, rope_f, bspec(0), bspec(1), bspec(2), rope_b, rope_b,
                  const(dmat), const(gqf), const(gkf), const(gqb), const(gkb), const(gh)],
        out_specs=[pl.BlockSpec((1, c, w), lambda b, j: (b, j, 0)),
                   pl.BlockSpec((1, c, w), lambda b, j: (b, bw(j), 0))],
        out_shape=[jax.ShapeDtypeStruct((bsz, t_len, w), F32)] * 2,
        scratch_shapes=[pltpu.VMEM((n_pairs, PAIR, PAIR), F32)] * 2,
        compiler_params=_cparams("parallel", "arbitrary"),
    )(ret, ret, ret, cos_t, sin_t, ret, ret, ret, cos_t, sin_t, dmat, gqf, gkf, gqb, gkb, gh)


def _head_norm_gate(o, g, ones_bd):
    sq = o * o
    hi = sq.astype(BF16)
    lo = (sq - hi.astype(F32)).astype(BF16)
    ss = jnp.dot(hi, ones_bd, preferred_element_type=F32) + jnp.dot(lo, ones_bd, preferred_element_type=F32)
    return o * lax.rsqrt(ss * (1.0 / HEAD_DIM) + EPS) * jax.nn.silu(g)


def _route(lg):
    lane = lax.broadcasted_iota(jnp.int32, lg.shape, 1).astype(F32)
    big = float(1 << 20)
    neg = -jnp.inf
    gmask = lane < N_GROUPS
    lgm = jnp.where(gmask, lg, neg)
    gmax = jnp.max(lgm, axis=1, keepdims=True)
    gsum = jnp.sum(jnp.where(gmask, jnp.exp(lgm - gmax), 0.0), axis=1, keepdims=True)
    gp = 1.0 / gsum
    gi = jnp.min(jnp.where(gmask & (lgm == gmax), lane, big), axis=1, keepdims=True)
    lo_lane = N_GROUPS + EXPERTS_PER_GROUP * gi
    emask = (lane >= lo_lane) & (lane < lo_lane + EXPERTS_PER_GROUP)
    l1 = jnp.max(jnp.where(emask, lg, neg), axis=1, keepdims=True)
    i1 = jnp.min(jnp.where(emask & (lg == l1), lane, big), axis=1, keepdims=True)
    mask2 = emask & (lane != i1)
    l2 = jnp.max(jnp.where(mask2, lg, neg), axis=1, keepdims=True)
    i2 = jnp.min(jnp.where(mask2 & (lg == l2), lane, big), axis=1, keepdims=True)
    e21 = jnp.exp(l2 - l1)
    w1 = gp / (1.0 + e21)
    w2 = gp * e21 / (1.0 + e21)
    e1 = i1 - N_GROUPS
    e2 = i2 - N_GROUPS
    return jnp.where(lane == 0, w1, jnp.where(lane == 1, w2, jnp.where(lane == 2, e1, jnp.where(lane == 3, e2, 0.0))))


def _merge_kernel(x_ref, ys_ref, hof_ref, hob_ref, hgg_ref, rof_ref, rob_ref, rgg_ref, gz_ref,
                  ml_ref, mc_ref, g2_ref, wglu_ref, wbs_ref, wbh_ref, wbr_ref, wout_ref, ones_ref,
                  wr_ref, br_ref, xo_ref, h2_ref, rt_ref, *, tm, tiles_per_b, n_ctx):
    d = x_ref.shape[1]
    is_ctx = _is_ctx_rows(tm, tiles_per_b, n_ctx)
    y = jax.nn.gelu(ys_ref[...], approximate=True)
    y = y * jax.nn.sigmoid(jnp.dot(y.astype(BF16), wglu_ref[...], preferred_element_type=F32))
    t_s5 = jnp.dot(y.astype(BF16), wbs_ref[...], preferred_element_type=F32)
    yh = _head_norm_gate(hof_ref[...] + hob_ref[...], hgg_ref[...], ones_ref[...])
    t_hg = jnp.dot(yh.astype(BF16), wbh_ref[...], preferred_element_type=F32)
    yr = _head_norm_gate(rof_ref[...] + rob_ref[...], rgg_ref[...], ones_ref[...])
    t_ret = jnp.dot(yr.astype(BF16), wbr_ref[...], preferred_element_type=F32)
    merged = (jax.nn.sigmoid(gz_ref[:, 0:d]) * t_s5 + jax.nn.sigmoid(gz_ref[:, d:2 * d]) * t_hg
              + jax.nn.sigmoid(gz_ref[:, 2 * d:3 * d]) * t_ret)
    mix = jnp.dot(merged.astype(BF16), wout_ref[...], preferred_element_type=F32)
    xn = x_ref[...] + _mod_rows(ml_ref, mc_ref, 2, is_ctx) * mix
    xo_ref[...] = xn
    h2 = _rmsnorm_rows(xn, g2_ref[...])
    h2 = h2 * (1.0 + _mod_rows(ml_ref, mc_ref, 4, is_ctx)) + _mod_rows(ml_ref, mc_ref, 3, is_ctx)
    h2_ref[...] = h2
    h_hi = h2.astype(BF16)
    h_lo = (h2 - h_hi.astype(F32)).astype(BF16)
    p_hi = jnp.dot(h_hi, wr_ref[...], preferred_element_type=F32)
    p_lo = jnp.dot(h_lo, wr_ref[:, 0:LANES], preferred_element_type=F32)
    rt_ref[...] = _route(p_hi[:, 0:LANES] + p_hi[:, LANES:2 * LANES] + p_lo + br_ref[...])


def _merge(x2, ys5, hg_of, hg_ob, hg2, ret_of, ret_ob, ret2, gz, mod_l, mod_c, g2, wts, t_len, n_ctx):
    m, d = x2.shape
    w_s5 = ys5.shape[1]
    w_h = hg_of.shape[1]
    tm = _row_tile(t_len)
    tpb = t_len // tm
    wglu, wbs, wbh, wbr, wout, ones_bd, wr, br = wts

    def rows(width, col=0):
        return pl.BlockSpec((tm, width), lambda i: (i, col))

    def const(a):
        nd = a.ndim
        return pl.BlockSpec(a.shape, lambda i: (0,) * nd)

    kern = functools.partial(_merge_kernel, tm=tm, tiles_per_b=tpb, n_ctx=n_ctx)
    return pl.pallas_call(
        kern,
        grid=(m // tm,),
        in_specs=[rows(d), rows(w_s5), rows(w_h), rows(w_h), rows(w_h, 4), rows(w_h), rows(w_h), rows(w_h, 3),
                  rows(3 * d),
                  pl.BlockSpec((1, N_MOD, d), lambda i: (i // tpb, 0, 0)),
                  pl.BlockSpec((1, N_MOD, d), lambda i: (0, 0, 0)),
                  const(g2), const(wglu), const(wbs), const(wbh), const(wbr), const(wout), const(ones_bd),
                  const(wr), const(br)],
        out_specs=[rows(d), rows(d), rows(LANES)],
        out_shape=[jax.ShapeDtypeStruct((m, d), F32), jax.ShapeDtypeStruct((m, d), F32),
                   jax.ShapeDtypeStruct((m, LANES), F32)],
        compiler_params=_cparams("parallel"),
    )(x2, ys5, hg_of, hg_ob, hg2, ret_of, ret_ob, ret2, gz, mod_l, mod_c, g2, wglu, wbs, wbh, wbr, wout,
      ones_bd, wr, br)


def _ffn_kernel(lay_ref, be_ref, nu_ref, x_ref, wg_ref, wu_ref, wd_ref, o_ref, wg_scr, wu_scr, wd_scr):
    i = pl.program_id(0)
    prev = be_ref[jnp.maximum(i - 1, 0)]

    @pl.when((i == 0) | (be_ref[i] != prev))
    def _():
        wg_scr[...] = wg_ref[0, 0].astype(BF16)
        wu_scr[...] = wu_ref[0, 0].astype(BF16)
        wd_scr[...] = wd_ref[0, 0].astype(BF16)

    @pl.when(i < nu_ref[0])
    def _():
        xb = x_ref[...].astype(BF16)
        hid = (jax.nn.silu(jnp.dot(xb, wg_scr[...], preferred_element_type=F32))
               * jnp.dot(xb, wu_scr[...], preferred_element_type=F32))
        o_ref[...] = jnp.dot(hid.astype(BF16), wd_scr[...], preferred_element_type=F32)

    @pl.when(i >= nu_ref[0])
    def _():
        o_ref[...] = jnp.zeros_like(o_ref)


def _expert_ffn(layer, blk_e, n_used, xg, wg, wu, wd):
    n_slots, d = xg.shape
    f = wg.shape[3]
    bm = MOE_BLOCK
    grid_spec = pltpu.PrefetchScalarGridSpec(
        num_scalar_prefetch=3,
        grid=(n_slots // bm,),
        in_specs=[
            pl.BlockSpec((bm, d), lambda i, lay, be, nu: (i, 0)),
            pl.BlockSpec((1, 1, d, f), lambda i, lay, be, nu: (lay[0], be[i], 0, 0)),
            pl.BlockSpec((1, 1, d, f), lambda i, lay, be, nu: (lay[0], be[i], 0, 0)),
            pl.BlockSpec((1, 1, f, d), lambda i, lay, be, nu: (lay[0], be[i], 0, 0)),
        ],
        out_specs=pl.BlockSpec((bm, d), lambda i, lay, be, nu: (i, 0)),
        scratch_shapes=[pltpu.VMEM((d, f), BF16), pltpu.VMEM((d, f), BF16), pltpu.VMEM((f, d), BF16)],
    )
    return pl.pallas_call(
        _ffn_kernel,
        grid_spec=grid_spec,
        out_shape=jax.ShapeDtypeStruct((n_slots, d), F32),
        compiler_params=_cparams("arbitrary"),
    )(layer, blk_e, n_used, xg, wg, wu, wd)


def _combine_kernel(x_ref, y0_ref, y1_ref, rt_ref, ml_ref, mc_ref, o_ref, *, tm, tiles_per_b, n_ctx):
    is_ctx = _is_ctx_rows(tm, tiles_per_b, n_ctx)
    rt = rt_ref[...]
    y = rt[:, 0:1] * y0_ref[...] + rt[:, 1:2] * y1_ref[...]
    o_ref[...] = x_ref[...] + _mod_rows(ml_ref, mc_ref, 5, is_ctx) * y


def _combine(x2, y0, y1, route, mod_l, mod_c, t_len, n_ctx):
    m, d = x2.shape
    tm = _row_tile(t_len)
    tpb = t_len // tm
    kern = functools.partial(_combine_kernel, tm=tm, tiles_per_b=tpb, n_ctx=n_ctx)
    return pl.pallas_call(
        kern,
        grid=(m // tm,),
        in_specs=[pl.BlockSpec((tm, d), lambda i: (i, 0))] * 3
        + [pl.BlockSpec((tm, LANES), lambda i: (i, 0)),
           pl.BlockSpec((1, N_MOD, d), lambda i: (i // tpb, 0, 0)),
           pl.BlockSpec((1, N_MOD, d), lambda i: (0, 0, 0))],
        out_specs=pl.BlockSpec((tm, d), lambda i: (i, 0)),
        out_shape=jax.ShapeDtypeStruct((m, d), F32),
        compiler_params=_cparams("parallel"),
    )(x2, y0, y1, route, mod_l, mod_c)


def _final_norm_kernel(x_ref, g_ref, o_ref):
    o_ref[0] = _rmsnorm_rows(x_ref[0], g_ref[...])


def _final_norm(x3, g, n_ctx):
    bsz, t_len, d = x3.shape
    n_lat = t_len - n_ctx
    tm = n_ctx
    off = n_ctx // tm
    return pl.pallas_call(
        _final_norm_kernel,
        grid=(bsz, n_lat // tm),
        in_specs=[pl.BlockSpec((1, tm, d), lambda b, j: (b, j + off, 0)),
                  pl.BlockSpec((1, d), lambda b, j: (0, 0))],
        out_specs=pl.BlockSpec((1, tm, d), lambda b, j: (b, j, 0)),
        out_shape=jax.ShapeDtypeStruct((bsz, n_lat, d), F32),
        compiler_params=_cparams("parallel", "parallel"),
    )(x3, g.reshape(1, d))


def _dispatch_indices(eid, n_tok):
    a = n_tok * TOP_K
    bm = MOE_BLOCK
    flat_e = eid.reshape(a)
    experts = jnp.arange(N_EXPERTS, dtype=jnp.int32)
    onehot = flat_e[:, None] == experts[None, :]
    counts = jnp.sum(onehot, axis=0, dtype=jnp.int32)
    padded = (counts + bm - 1) // bm * bm
    pad_end = jnp.cumsum(padded)
    pad_start = pad_end - padded
    start = jnp.cumsum(counts) - counts
    order = jnp.argsort(flat_e).astype(jnp.int32)
    inv = jnp.argsort(order).astype(jnp.int32)
    shift = jnp.sum(jnp.where(onehot, (pad_start - start)[None, :], 0), axis=1, dtype=jnp.int32)
    pos = (inv + shift).reshape(n_tok, TOP_K)
    n_blocks = -(-a // bm) + N_EXPERTS
    blk_start = jnp.arange(n_blocks, dtype=jnp.int32) * bm
    blk_e = jnp.minimum(jnp.sum(pad_end[None, :] <= blk_start[:, None], axis=1, dtype=jnp.int32), N_EXPERTS - 1)
    slot = jnp.arange(n_blocks * bm, dtype=jnp.int32)
    slot_shift = jnp.repeat((pad_start - start)[blk_e], bm)
    src = jnp.clip(slot - slot_shift, 0, a - 1)
    slot_tok = order.at[src].get(mode='promise_in_bounds') // TOP_K
    n_used = (pad_end[-1] // bm).astype(jnp.int32).reshape(1)
    return slot_tok, pos, blk_e, n_used


def _moe(layer, h2, route, wg, wu, wd):
    n_tok = h2.shape[0]
    eid = route[:, 2:4].astype(jnp.int32)
    slot_tok, pos, blk_e, n_used = _dispatch_indices(eid, n_tok)
    xg = h2.at[slot_tok].get(mode='promise_in_bounds')
    yb = _expert_ffn(layer, blk_e, n_used, xg, wg, wu, wd)
    return yb.at[pos[:, 0]].get(mode='promise_in_bounds'), yb.at[pos[:, 1]].get(mode='promise_in_bounds')


def kernel(x, c, ctx, c_ctx, w_ada, b_ada, norm1_g, norm2_g, w_in, s5_lam_re, s5_lam_im, s5_log_dt, s5_b_re, s5_b_im, s5_c_re, s5_c_im, s5_d, s5_w_glu, hgrn_lb_raw, w_branch_s5, w_branch_hgrn, w_branch_ret, w_out, moe_w_group, moe_b_group, moe_w_expert, moe_b_expert, moe_w_gate, moe_w_up, moe_w_down, final_norm_g):
    bsz, n_lat, d = x.shape
    n_ctx = ctx.shape[1]
    t_len = n_ctx + n_lat
    depth = w_ada.shape[0]
    m = bsz * t_len
    w_s5 = s5_d.shape[1]
    w_hg = w_branch_hgrn.shape[1]
    w_ret = w_branch_ret.shape[1]
    n_heads = w_ret // HEAD_DIM

    pad_rows = (-(bsz + 1)) % SUBLANES
    cc = jnp.concatenate([c, c_ctx[None, :], jnp.zeros((pad_rows, d), F32)], axis=0)
    mods = _ada_tables(cc, w_ada, b_ada).reshape(depth, bsz + 1 + pad_rows, N_MOD, d)

    p_lb = jax.nn.softmax(hgrn_lb_raw.astype(F32), axis=0)
    lower_bounds = jnp.cumsum(p_lb, axis=0) - p_lb[0:1]
    ret_consts = _ret_constants(n_heads, t_len, n_ctx)
    ones_bd = jnp.kron(jnp.eye(w_hg // HEAD_DIM, dtype=F32), jnp.ones((HEAD_DIM, HEAD_DIM), F32)).astype(BF16)

    xs = jnp.concatenate([ctx, x], axis=1).reshape(m, d)
    off_hg = w_s5
    off_ret = off_hg + 5 * w_hg
    off_gz = off_ret + 4 * w_ret
    for l in range(depth):
        mod_l = mods[l, :bsz]
        mod_c = mods[l, bsz:bsz + 1]
        wl = w_in[l].astype(BF16)
        w_parts = (wl[:, :off_hg], wl[:, off_hg:off_ret], wl[:, off_ret:off_gz], wl[:, off_gz:])
        u, hg, ret, gz = _project(xs, norm1_g[l], mod_l, mod_c, w_parts, t_len, n_ctx)

        s5w = _s5_weights(s5_lam_re[l], s5_lam_im[l], s5_log_dt[l], s5_b_re[l], s5_b_im[l],
                          s5_c_re[l], s5_c_im[l], s5_d[l])
        ys5 = _s5_mix(u.reshape(bsz, t_len, w_s5), *s5w, n_ctx).reshape(m, w_s5)
        hg_of, hg_ob = _hg_mix(hg.reshape(bsz, t_len, 5 * w_hg), lower_bounds[l], n_ctx)
        ret_of, ret_ob = _ret_mix(ret.reshape(bsz, t_len, 4 * w_ret), ret_consts, n_ctx)

        wr = jnp.concatenate([moe_w_group[l], moe_w_expert[l],
                              jnp.zeros((d, LANES - N_GROUPS - N_EXPERTS), F32)], axis=1)
        br = jnp.concatenate([moe_b_group[l], moe_b_expert[l],
                              jnp.zeros((LANES - N_GROUPS - N_EXPERTS,), F32)])[None, :]
        wr_hi = wr.astype(BF16)
        wr = jnp.concatenate([wr_hi, (wr - wr_hi.astype(F32)).astype(BF16)], axis=1)
        wts = (s5_w_glu[l].astype(BF16), w_branch_s5[l].astype(BF16), w_branch_hgrn[l].astype(BF16),
               w_branch_ret[l].astype(BF16), w_out[l].astype(BF16), ones_bd, wr, br)
        xs, h2, route = _merge(xs, ys5, hg_of.reshape(m, w_hg), hg_ob.reshape(m, w_hg), hg,
                               ret_of.reshape(m, w_ret), ret_ob.reshape(m, w_ret), ret, gz,
                               mod_l, mod_c, norm2_g[l].reshape(1, d), wts, t_len, n_ctx)
        y0, y1 = _moe(jnp.full((1,), l, jnp.int32), h2, route, moe_w_gate, moe_w_up, moe_w_down)
        xs = _combine(xs, y0, y1, route, mod_l, mod_c, t_len, n_ctx)
    return _final_norm(xs.reshape(bsz, t_len, d), final_norm_g, n_ctx)
```

```python
import functools

import jax
import jax.numpy as jnp
import numpy as np
from jax import lax
from jax.experimental import pallas as pl
from jax.experimental.pallas import tpu as pltpu

F32 = jnp.float32
BF16 = jnp.bfloat16
HIGHEST = lax.Precision.HIGHEST

LANES = 128
SUBLANES = 8
VMEM_LIMIT = 56 * 1024 * 1024

EPS = 1e-6
N_MOD = 6
GRID_W = 64
HEAD_DIM = 64
PAIR = 2 * HEAD_DIM
S5_GROUP = 16
S5_STATE = 64
S5_CHUNK = 16
HG_CHUNK = 64
HG_HALF = HG_CHUNK // 2
HG_BLOCK = 256
RET_BLOCK = 256
ROPE_BASE = 10000.0
N_GROUPS = 4
EXPERTS_PER_GROUP = 8
N_EXPERTS = N_GROUPS * EXPERTS_PER_GROUP
TOP_K = 2
MOE_BLOCK = 512


def _cparams(*sem):
    return pltpu.CompilerParams(dimension_semantics=sem, vmem_limit_bytes=VMEM_LIMIT)


def _row_tile(t_len):
    for tm in (544, 512, 384, 272, 256, 128, 64, 32, 16, 8):
        if t_len % tm == 0:
            return tm
    raise ValueError(f"unsupported stream length {t_len}")


def _ada_kernel(c_ref, w_ref, b_ref, o_ref):
    a = jax.nn.silu(c_ref[...]).astype(BF16)
    o_ref[0] = jnp.dot(a, w_ref[0].astype(BF16), preferred_element_type=F32) + b_ref[0]


def _ada_tables(cc, w_ada, b_ada):
    depth, d, n = w_ada.shape
    rows = cc.shape[0]
    tn = 1536 if n % 1536 == 0 else n
    return pl.pallas_call(
        _ada_kernel,
        grid=(depth, n // tn),
        in_specs=[
            pl.BlockSpec((rows, d), lambda l, j: (0, 0)),
            pl.BlockSpec((1, d, tn), lambda l, j: (l, 0, j)),
            pl.BlockSpec((1, 1, tn), lambda l, j: (l, 0, j)),
        ],
        out_specs=pl.BlockSpec((1, rows, tn), lambda l, j: (l, 0, j)),
        out_shape=jax.ShapeDtypeStruct((depth, rows, n), F32),
        compiler_params=_cparams("parallel", "parallel"),
    )(cc, w_ada, b_ada.reshape(depth, 1, n))


def _mod_rows(ml_ref, mc_ref, idx, is_ctx):
    return jnp.where(is_ctx, mc_ref[0, idx:idx + 1, :], ml_ref[0, idx:idx + 1, :])


def _is_ctx_rows(tm, tiles_per_b, n_ctx):
    jt = pl.program_id(0) % tiles_per_b
    row = jt * tm + lax.broadcasted_iota(jnp.int32, (tm, 1), 0)
    return row < n_ctx


def _store_token_tiles(ref, val):
    n_rows = val.shape[0]
    for k in range(val.shape[1] // LANES):
        ref[pl.ds(k, n_rows, stride=SUBLANES), :] = val[:, LANES * k:LANES * (k + 1)]


def _load_token_tiles(ref):
    n_rows = ref.shape[0] // SUBLANES
    return jnp.concatenate([ref[pl.ds(k, n_rows, stride=SUBLANES), :] for k in range(SUBLANES)], axis=1)


def _rmsnorm_rows(x, g):
    return x * lax.rsqrt(jnp.mean(x * x, axis=-1, keepdims=True) + EPS) * g


def _proj_kernel(x_ref, g_ref, ml_ref, mc_ref, wu_ref, wh_ref, wr_ref, wg_ref,
                 u_ref, hg_ref, ret_ref, gz_ref, *, tm, tiles_per_b, n_ctx):
    is_ctx = _is_ctx_rows(tm, tiles_per_b, n_ctx)
    y = _rmsnorm_rows(x_ref[...], g_ref[...])
    y = y * (1.0 + _mod_rows(ml_ref, mc_ref, 1, is_ctx)) + _mod_rows(ml_ref, mc_ref, 0, is_ctx)
    h = y.astype(BF16)
    u_ref[...] = jnp.dot(h, wu_ref[...], preferred_element_type=F32)
    hg_ref[...] = jnp.dot(h, wh_ref[...], preferred_element_type=F32)
    ret_ref[...] = jnp.dot(h, wr_ref[...], preferred_element_type=F32)
    gz_ref[...] = jnp.dot(h, wg_ref[...], preferred_element_type=F32).astype(gz_ref.dtype)


def _project(x2, g, mod_l, mod_c, w_parts, t_len, n_ctx):
    m, d = x2.shape
    tm = _row_tile(t_len)
    tpb = t_len // tm
    kern = functools.partial(_proj_kernel, tm=tm, tiles_per_b=tpb, n_ctx=n_ctx)

    def const(a):
        return pl.BlockSpec(a.shape, lambda i: (0, 0), pipeline_mode=pl.Buffered(1))

    out_dtypes = (F32, F32, F32, BF16)
    return pl.pallas_call(
        kern,
        grid=(m // tm,),
        in_specs=[
            pl.BlockSpec((tm, d), lambda i: (i, 0)),
            pl.BlockSpec((1, d), lambda i: (0, 0)),
            pl.BlockSpec((1, N_MOD, d), lambda i: (i // tpb, 0, 0)),
            pl.BlockSpec((1, N_MOD, d), lambda i: (0, 0, 0)),
        ] + [const(w) for w in w_parts],
        out_specs=[pl.BlockSpec((tm, w.shape[1]), lambda i: (i, 0)) for w in w_parts],
        out_shape=[jax.ShapeDtypeStruct((m, w.shape[1]), dt) for w, dt in zip(w_parts, out_dtypes)],
        compiler_params=_cparams("parallel"),
    )(x2, g.reshape(1, d), mod_l, mod_c, *w_parts)


def _s5_weights(lam_re, lam_im, log_dt, b_re, b_im, c_re, c_im, d_skip):
    n_dir, g_n, p_n = lam_re.shape
    h_n = b_re.shape[-1]
    cn = S5_CHUNK
    lr, li = lam_re.astype(F32), lam_im.astype(F32)
    dt = jnp.exp(log_dt.astype(F32))[..., None]
    mag = jnp.exp(lr * dt)
    ar = mag * jnp.cos(li * dt)
    ai = mag * jnp.sin(li * dt)
    den = lr * lr + li * li
    zr = ((ar - 1.0) * lr + ai * li) / den
    zi = (ai * lr - (ar - 1.0) * li) / den
    bbr = zr[..., None] * b_re - zi[..., None] * b_im
    bbi = zr[..., None] * b_im + zi[..., None] * b_re
    j = jnp.arange(cn + 1, dtype=F32)[:, None, None, None]
    pmag = jnp.exp(lr[None] * dt[None] * j)
    pang = li[None] * dt[None] * j
    pr = pmag * jnp.cos(pang)
    pi = pmag * jnp.sin(pang)
    wr = pr[..., None] * bbr[None] - pi[..., None] * bbi[None]
    wi = pr[..., None] * bbi[None] + pi[..., None] * bbr[None]
    kj = (jnp.einsum('dghp,jdgpk->jdghk', c_re, wr, precision=HIGHEST)
          - jnp.einsum('dghp,jdgpk->jdghk', c_im, wi, precision=HIGHEST))
    s_idx = jnp.arange(cn)[:, None]
    t_idx = jnp.arange(cn)[None, :]
    lag = t_idx - s_idx
    kf = kj[:cn, 0][jnp.clip(lag, 0, cn - 1)]
    kb = kj[:cn, 1][jnp.clip(-lag, 0, cn - 1)]
    kf = jnp.where((lag >= 0)[:, :, None, None, None], kf, 0.0)
    kb = jnp.where((lag <= 0)[:, :, None, None, None], kb, 0.0)
    dsk = d_skip.astype(F32).reshape(g_n, h_n)
    eye_h = jnp.eye(h_n, dtype=F32)
    diag = (lag == 0)[:, :, None, None, None] * (dsk[None, None, :, :, None] * eye_h[None, None, None])
    m_full = kf + kb + diag
    m_mat = m_full.transpose(2, 0, 4, 1, 3).reshape(g_n, cn * h_n, cn * h_n)
    wf_r = wr[:cn, 0][::-1]
    wf_i = wi[:cn, 0][::-1]
    wb_r = wr[:cn, 1]
    wb_i = wi[:cn, 1]
    w_in = jnp.stack([wf_r, wf_i, wb_r, wb_i], axis=0)
    w_in = w_in.transpose(2, 1, 4, 0, 3).reshape(g_n, cn * h_n, 4 * p_n)
    def out_map(d, powers):
        prd, pid = pr[powers, d], pi[powers, d]
        w_re = c_re[d][None] * prd[:, :, None, :] - c_im[d][None] * pid[:, :, None, :]
        w_im = c_re[d][None] * pid[:, :, None, :] + c_im[d][None] * prd[:, :, None, :]
        return w_re, -w_im
    t_arr = jnp.arange(cn)
    of_r, of_i = out_map(0, t_arr + 1)
    ob_r, ob_i = out_map(1, cn - t_arr)
    w_out = jnp.stack([of_r, of_i, ob_r, ob_i], axis=0)
    w_out = w_out.transpose(2, 0, 4, 1, 3).reshape(g_n, 4 * p_n, cn * h_n)
    slot = jax.nn.one_hot(jnp.arange(g_n) % 2, 2, dtype=F32)
    w_in = w_in.reshape(g_n, cn * h_n, 4, 1, p_n) * slot[:, None, None, :, None]
    w_in = w_in.reshape(g_n, cn * h_n, 8 * p_n)
    w_out = w_out.reshape(g_n, 4, 1, p_n, cn * h_n) * slot[:, None, :, None, None]
    w_out = w_out.reshape(g_n, 8 * p_n, cn * h_n)
    a_mat = jnp.stack([pr[cn, 0].reshape(-1), pi[cn, 0].reshape(-1),
                       pr[cn, 1].reshape(-1), pi[cn, 1].reshape(-1)], axis=0)
    w1 = jnp.concatenate([m_mat, w_in], axis=-1).astype(BF16)
    return w1, w_out.astype(BF16), a_mat


def _s5_fold_perm():
    j, g8, h = np.meshgrid(np.arange(8), np.arange(8), np.arange(S5_GROUP), indexing='ij')
    src = (j * 8 + g8) * S5_GROUP + h
    dst = (g8 * 8 + j) * S5_GROUP + h
    perm = np.zeros((8 * LANES, 8 * LANES), np.float32)
    perm[src.reshape(-1), dst.reshape(-1)] = 1.0
    return jnp.asarray(perm, dtype=BF16)


def _s5_kernel(u_ref, perm_ref, w1_ref, wo_ref, a_ref, y_ref, ug_scr, yg_scr, xfr, xfi, xbr, xbi,
               *, n_chunks, n_ctx_chunks, n_groups):
    cn = S5_CHUNK
    blk = cn * S5_GROUP
    tok_per_slab = LANES // S5_GROUP
    for c in range(cn // tok_per_slab):
        slabs = [u_ref[0, pl.ds(tok_per_slab * c + j, n_chunks, stride=cn), :] for j in range(tok_per_slab)]
        s = jnp.concatenate(slabs, axis=1).astype(BF16)
        up = jnp.dot(s, perm_ref[...], preferred_element_type=F32).astype(BF16)
        for g in range(n_groups):
            ug_scr[g, :, LANES * c:LANES * (c + 1)] = up[:, LANES * g:LANES * (g + 1)]
    xs = (xfr, xfi, xbr, xbi)
    for g in range(n_groups):
        z = jnp.dot(ug_scr[g], w1_ref[g], preferred_element_type=F32)
        yg_scr[g] = z[:, :blk]
        lanes = slice(LANES * (g // 2), LANES * (g // 2 + 1))
        for k, xr in enumerate(xs):
            piece = z[:, blk + LANES * k:blk + LANES * (k + 1)]
            if g % 2 == 0:
                xr[:, lanes] = piece
            else:
                xr[:, lanes] += piece
    arf, aif, arb, aib = a_ref[0:1, :], a_ref[1:2, :], a_ref[2:3, :], a_ref[3:4, :]

    def body(i, carry):
        fr, fi, br, bi = carry
        nb = jnp.where(i < n_ctx_chunks, n_ctx_chunks - 1 - i, n_chunks - 1 - (i - n_ctx_chunks))
        x_fr, x_fi = xfr[pl.ds(i, 1), :], xfi[pl.ds(i, 1), :]
        x_br, x_bi = xbr[pl.ds(nb, 1), :], xbi[pl.ds(nb, 1), :]
        xfr[pl.ds(i, 1), :] = fr
        xfi[pl.ds(i, 1), :] = fi
        xbr[pl.ds(nb, 1), :] = br
        xbi[pl.ds(nb, 1), :] = bi
        return (arf * fr - aif * fi + x_fr, arf * fi + aif * fr + x_fi,
                arb * br - aib * bi + x_br, arb * bi + aib * br + x_bi)

    zero = jnp.zeros((1, xfr.shape[1]), F32)
    lax.fori_loop(0, n_chunks, body, (zero, zero, zero, zero))
    for g in range(n_groups):
        lanes = slice(LANES * (g // 2), LANES * (g // 2 + 1))
        hp = jnp.concatenate([xr[:, lanes] for xr in xs], axis=1).astype(BF16)
        yg_scr[g] += jnp.dot(hp, wo_ref[g], preferred_element_type=F32)
    for c in range(cn // tok_per_slab):
        ycat = jnp.concatenate([yg_scr[g, :, LANES * c:LANES * (c + 1)] for g in range(n_groups)], axis=1)
        hi = ycat.astype(BF16)
        lo = (ycat - hi.astype(F32)).astype(BF16)
        r = _dot_nt(hi, perm_ref[...]) + _dot_nt(lo, perm_ref[...])
        for j in range(tok_per_slab):
            y_ref[0, pl.ds(tok_per_slab * c + j, n_chunks, stride=cn), :] = r[:, LANES * j:LANES * (j + 1)]


def _s5_mix(u, w1, w_out, a_mat, n_ctx):
    bsz, t_len, width = u.shape
    g_n = width // S5_GROUP
    cn = S5_CHUNK
    n_chunks = t_len // cn
    blk = cn * S5_GROUP
    perm = _s5_fold_perm()
    gs = LANES // S5_GROUP
    state_w = gs * S5_STATE
    kern = functools.partial(_s5_kernel, n_chunks=n_chunks, n_ctx_chunks=n_ctx // cn, n_groups=gs)
    return pl.pallas_call(
        kern,
        grid=(bsz, g_n // gs),
        in_specs=[pl.BlockSpec((1, t_len, LANES), lambda b, s: (b, 0, s)),
                  pl.BlockSpec(perm.shape, lambda b, s: (0, 0), pipeline_mode=pl.Buffered(1)),
                  pl.BlockSpec((gs,) + w1.shape[1:], lambda b, s: (s, 0, 0)),
                  pl.BlockSpec((gs,) + w_out.shape[1:], lambda b, s: (s, 0, 0)),
                  pl.BlockSpec((4, state_w), lambda b, s: (0, s))],
        out_specs=pl.BlockSpec((1, t_len, LANES), lambda b, s: (b, 0, s)),
        out_shape=jax.ShapeDtypeStruct((bsz, t_len, width), F32),
        scratch_shapes=[pltpu.VMEM((gs, n_chunks, blk), BF16), pltpu.VMEM((gs, n_chunks, blk), F32)]
        + [pltpu.VMEM((n_chunks, state_w), F32)] * 4,
        compiler_params=_cparams("parallel", "parallel"),
    )(u, perm, w1, w_out, a_mat)


def _head0_lanes(shape):
    return lax.broadcasted_iota(jnp.int32, shape, len(shape) - 1) < HEAD_DIM


def _stack_heads(q):
    m0 = _head0_lanes(q.shape)
    return jnp.concatenate([jnp.where(m0, q, 0.0), jnp.where(m0, 0.0, q)], axis=-2)


def _unstack_heads(o2):
    c = o2.shape[-2] // 2
    top, bottom = o2[..., :c, :], o2[..., c:, :]
    return jnp.where(_head0_lanes(top.shape), top, bottom)


def _pair_block_mask():
    r = lax.broadcasted_iota(jnp.int32, (PAIR, PAIR), 0) < HEAD_DIM
    c = lax.broadcasted_iota(jnp.int32, (PAIR, PAIR), 1) < HEAD_DIM
    return r == c


def _dot_nt(a, b):
    return lax.dot_general(a, b, (((1,), (1,)), ((), ())), preferred_element_type=F32)


def _dot_tn(a, b):
    return lax.dot_general(a, b, (((0,), (0,)), ((), ())), preferred_element_type=F32)


def _bwd_block(j, n_ctx_blocks, n_blocks):
    return jnp.where(j < n_ctx_blocks, n_ctx_blocks - 1 - j, n_blocks - 1 - (j - n_ctx_blocks))


def _bdot_nt(a, b):
    return lax.dot_general(a, b, (((2,), (2,)), ((0,), (0,))), preferred_element_type=F32)


def _bdot_nn(a, b):
    return lax.dot_general(a, b, (((2,), (1,)), ((0,), (0,))), preferred_element_type=F32)


def _hg_block(q, z, v, lb, ht, reverse):
    rows = q.shape[0]
    c = HG_CHUNK
    hh = HG_HALF
    n = rows // c
    f = lb + (1.0 - lb) * jax.nn.sigmoid(z)
    logf = jnp.log(f)
    kk = 1.0 - f
    ri = lax.broadcasted_iota(jnp.int32, (rows, rows), 0)
    ci = lax.broadcasted_iota(jnp.int32, (rows, rows), 1)
    same_chunk = (ri // c) == (ci // c)
    tri = jnp.where(same_chunk & ((ci >= ri) if reverse else (ci <= ri)), 1.0, 0.0).astype(BF16)
    l1 = logf.astype(BF16)
    r1 = logf - l1.astype(F32)
    l2 = r1.astype(BF16)
    l3 = (r1 - l2.astype(F32)).astype(BF16)
    b3 = jnp.dot(tri, jnp.concatenate([l1, l2, l3], axis=1), preferred_element_type=F32)
    b = b3[:, 0:PAIR] + b3[:, PAIR:2 * PAIR] + b3[:, 2 * PAIR:3 * PAIR]
    b4, q4, k4 = (a.reshape(n, c, PAIR) for a in (b, q, kk))
    vb = v.astype(BF16).reshape(n, c, PAIR)
    if reverse:
        first, second = slice(hh, c), slice(0, hh)
        r = b4[:, hh:hh + 1]
        bend = b4[:, 0:1]
    else:
        first, second = slice(0, hh), slice(hh, c)
        r = b4[:, hh - 1:hh]
        bend = b4[:, c - 1:c]
    b1, b2 = b4[:, first], b4[:, second]
    qd1 = q4[:, first] * jnp.exp(b1)
    qd2 = q4[:, second] * jnp.exp(b2 - r)
    k1d = k4[:, first] * jnp.exp(-b1)
    k2 = k4 * jnp.exp(r - b4)
    s1 = _bdot_nt(_stack_heads(qd1).astype(BF16), k1d.astype(BF16))
    s2 = _bdot_nt(_stack_heads(qd2).astype(BF16), k2.astype(BF16))
    t1 = lax.broadcasted_iota(jnp.int32, (n, 2 * hh, hh), 1) % hh
    c1 = lax.broadcasted_iota(jnp.int32, (n, 2 * hh, hh), 2)
    t2 = lax.broadcasted_iota(jnp.int32, (n, 2 * hh, c), 1) % hh
    c2 = lax.broadcasted_iota(jnp.int32, (n, 2 * hh, c), 2)
    if reverse:
        s1 = jnp.where(c1 >= t1, s1, 0.0)
        s2 = jnp.where(c2 >= t2, s2, 0.0)
    else:
        s1 = jnp.where(c1 <= t1, s1, 0.0)
        s2 = jnp.where(c2 <= t2 + hh, s2, 0.0)
    o1 = _unstack_heads(_bdot_nn(s1.astype(BF16), vb[:, first]))
    o2 = _unstack_heads(_bdot_nn(s2.astype(BF16), vb))
    o = jnp.concatenate([o2, o1] if reverse else [o1, o2], axis=1)
    ke = (k4 * jnp.exp(bend - b4)).astype(BF16)
    dec = jnp.exp(bend)
    bmask = _pair_block_mask()
    states = [None] * n
    for ch in (range(n - 1, -1, -1) if reverse else range(n)):
        states[ch] = ht.astype(BF16)
        ht = ht * dec[ch] + jnp.where(bmask, _dot_tn(vb[ch], ke[ch]), 0.0)
    o = o + _bdot_nt((q4 * jnp.exp(b4)).astype(BF16), jnp.stack(states, axis=0))
    return o.reshape(rows, PAIR), ht


def _hg_kernel(qf_ref, zf_ref, vf_ref, qb_ref, zb_ref, vb_ref, lb_ref, of_ref, ob_ref, hf_scr, hb_scr, *, n_pairs):
    @pl.when(pl.program_id(1) == 0)
    def _():
        hf_scr[...] = jnp.zeros_like(hf_scr)
        hb_scr[...] = jnp.zeros_like(hb_scr)

    for p in range(n_pairs):
        sl = slice(p * PAIR, (p + 1) * PAIR)
        o, hn = _hg_block(qf_ref[0, :, sl], zf_ref[0, :, sl], vf_ref[0, :, sl], lb_ref[0:1, sl], hf_scr[p], False)
        of_ref[0, :, sl] = o
        hf_scr[p] = hn
        o, hn = _hg_block(qb_ref[0, :, sl], zb_ref[0, :, sl], vb_ref[0, :, sl], lb_ref[1:2, sl], hb_scr[p], True)
        ob_ref[0, :, sl] = o
        hb_scr[p] = hn


def _hg_mix(hg, lb, n_ctx):
    bsz, t_len, total = hg.shape
    w = total // 5
    n_pairs = w // PAIR
    c = HG_BLOCK
    nc, ncc = t_len // c, n_ctx // c
    bw = functools.partial(_bwd_block, n_ctx_blocks=ncc, n_blocks=nc)

    def fspec(col):
        return pl.BlockSpec((1, c, w), lambda b, j: (b, j, col))

    def bspec(col):
        return pl.BlockSpec((1, c, w), lambda b, j: (b, bw(j), col))

    kern = functools.partial(_hg_kernel, n_pairs=n_pairs)
    return pl.pallas_call(
        kern,
        grid=(bsz, nc),
        in_specs=[fspec(0), fspec(1), fspec(3), bspec(0), bspec(2), bspec(3),
                  pl.BlockSpec((2, w), lambda b, j: (0, 0))],
        out_specs=[pl.BlockSpec((1, c, w), lambda b, j: (b, j, 0)),
                   pl.BlockSpec((1, c, w), lambda b, j: (b, bw(j), 0))],
        out_shape=[jax.ShapeDtypeStruct((bsz, t_len, w), F32)] * 2,
        scratch_shapes=[pltpu.VMEM((n_pairs, PAIR, PAIR), F32)] * 2,
        compiler_params=_cparams("parallel", "arbitrary"),
    )(hg, hg, hg, hg, hg, hg, lb)


def _rope_rows(x, cos, sin):
    hd2 = HEAD_DIM // 2
    lane = lax.broadcasted_iota(jnp.int32, x.shape, 1) % HEAD_DIM
    swapped = jnp.where(lane < hd2, pltpu.roll(x, PAIR - hd2, 1), pltpu.roll(x, hd2, 1))
    return x * cos + swapped * sin


def _ret_kernel(qf_ref, kf_ref, vf_ref, cf_ref, sf_ref, qb_ref, kb_ref, vb_ref, cb_ref, sb_ref,
                d_ref, gqf_ref, gkf_ref, gqb_ref, gkb_ref, gh_ref, of_ref, ob_ref, hf_scr, hb_scr, *, n_pairs):
    @pl.when(pl.program_id(1) == 0)
    def _():
        hf_scr[...] = jnp.zeros_like(hf_scr)
        hb_scr[...] = jnp.zeros_like(hb_scr)

    kscale = HEAD_DIM ** -0.5
    bmask = _pair_block_mask()
    for p in range(n_pairs):
        sl = slice(p * PAIR, (p + 1) * PAIR)
        q = _rope_rows(qf_ref[0, :, sl], cf_ref[...], sf_ref[...])
        k = _rope_rows(kf_ref[0, :, sl] * kscale, cf_ref[...], sf_ref[...])
        vb16 = vf_ref[0, :, sl].astype(BF16)
        s = _dot_nt(_stack_heads(q).astype(BF16), k.astype(BF16)) * d_ref[p]
        o = _unstack_heads(jnp.dot(s.astype(BF16), vb16, preferred_element_type=F32))
        ht = hf_scr[p]
        o = o + _dot_nt((q * gqf_ref[:, sl]).astype(BF16), ht.astype(BF16))
        upd = _dot_tn(vb16, (k * gkf_ref[:, sl]).astype(BF16))
        hf_scr[p] = ht * gh_ref[p] + jnp.where(bmask, upd, 0.0)
        of_ref[0, :, sl] = o

        q = _rope_rows(qb_ref[0, :, sl], cb_ref[...], sb_ref[...])
        k = _rope_rows(kb_ref[0, :, sl] * kscale, cb_ref[...], sb_ref[...])
        vb16 = vb_ref[0, :, sl].astype(BF16)
        ht = hb_scr[p]
        ob_ref[0, :, sl] = _dot_nt((q * gqb_ref[:, sl]).astype(BF16), ht.astype(BF16))
        upd = _dot_tn(vb16, (k * gkb_ref[:, sl]).astype(BF16))
        hb_scr[p] = ht * gh_ref[p] + jnp.where(bmask, upd, 0.0)


def _ret_constants(n_heads, t_len, n_ctx):
    c = RET_BLOCK
    log_gamma = jnp.log(1.0 - 2.0 ** (-5.0 - jnp.arange(n_heads, dtype=F32)))
    t = jnp.arange(c, dtype=F32)
    dist = jnp.abs(t[:, None] - t[None, :])
    dmat = jnp.exp(log_gamma[:, None, None] * dist[None])
    dmat = dmat.reshape(n_heads // 2, 2 * c, c)
    lg_lane = jnp.repeat(log_gamma, HEAD_DIM)[None, :]
    gqf = jnp.exp(lg_lane * (t[:, None] + 1.0))
    gkf = jnp.exp(lg_lane * (c - 1.0 - t[:, None]))
    gqb = jnp.exp(lg_lane * (c - t[:, None]))
    gkb = jnp.exp(lg_lane * t[:, None])
    gh = jnp.exp(lg_lane * float(c)).reshape(n_heads // 2, 1, PAIR)
    gh = jnp.broadcast_to(gh, (n_heads // 2, PAIR, PAIR))
    n_lat = t_len - n_ctx
    rows = n_lat // GRID_W
    row = jnp.repeat(jnp.arange(rows, dtype=F32), GRID_W)
    col = jnp.broadcast_to(jnp.arange(GRID_W, dtype=F32)[None, :], (rows, GRID_W)).reshape(-1)
    n_freq = HEAD_DIM // 4
    inv_freq = ROPE_BASE ** (-jnp.arange(n_freq, dtype=F32) / n_freq)
    ang = jnp.concatenate([row[:, None] * inv_freq, col[:, None] * inv_freq], axis=-1)
    cos_l, sin_l = jnp.cos(ang), jnp.sin(ang)
    cos_h = jnp.concatenate([cos_l, cos_l], axis=-1)
    sin_h = jnp.concatenate([-sin_l, sin_l], axis=-1)
    cos_t = jnp.concatenate([jnp.ones((n_ctx, HEAD_DIM), F32), cos_h], axis=0)
    sin_t = jnp.concatenate([jnp.zeros((n_ctx, HEAD_DIM), F32), sin_h], axis=0)
    cos_t = jnp.concatenate([cos_t, cos_t], axis=-1)
    sin_t = jnp.concatenate([sin_t, sin_t], axis=-1)
    return dmat, gqf, gkf, gqb, gkb, gh, cos_t, sin_t


def _ret_mix(ret, consts, n_ctx):
    dmat, gqf, gkf, gqb, gkb, gh, cos_t, sin_t = consts
    bsz, t_len, total = ret.shape
    w = total // 4
    n_pairs = w // PAIR
    c = RET_BLOCK
    nb, ncb = t_len // c, n_ctx // c
    bw = functools.partial(_bwd_block, n_ctx_blocks=ncb, n_blocks=nb)

    def fspec(col):
        return pl.BlockSpec((1, c, w), lambda b, j: (b, j, col))

    def bspec(col):
        return pl.BlockSpec((1, c, w), lambda b, j: (b, bw(j), col))

    def const(a):
        nd = a.ndim
        return pl.BlockSpec(a.shape, lambda b, j: (0,) * nd)

    rope_f = pl.BlockSpec((c, PAIR), lambda b, j: (j, 0))
    rope_b = pl.BlockSpec((c, PAIR), lambda b, j: (bw(j), 0))
    kern = functools.partial(_ret_kernel, n_pairs=n_pairs)
    return pl.pallas_call(
        kern,
        grid=(bsz, nb),
        in_specs=[fspec(0), fspec(1), fspec(2), rope_f, rope_f, bspec(0), bspec(1), bspec(2), rope_b, rope_b,
                  const(dmat), const(gqf), const(gkf), const(gqb), const(gkb), const(gh)],
        out_specs=[pl.BlockSpec((1, c, w), lambda b, j: (b, j, 0)),
                   pl.BlockSpec((1, c, w), lambda b, j: (b, bw(j), 0))],
        out_shape=[jax.ShapeDtypeStruct((bsz, t_len, w), F32)] * 2,
        scratch_shapes=[pltpu.VMEM((n_pairs, PAIR, PAIR), F32)] * 2,
        compiler_params=_cparams("parallel", "arbitrary"),
    )(ret, ret, ret, cos_t, sin_t, ret, ret, ret, cos_t, sin_t, dmat, gqf, gkf, gqb, gkb, gh)


def _head_norm_gate(o, g, ones_bd):
    sq = o * o
    hi = sq.astype(BF16)
    lo = (sq - hi.astype(F32)).astype(BF16)
    ss = jnp.dot(hi, ones_bd, preferred_element_type=F32) + jnp.dot(lo, ones_bd, preferred_element_type=F32)
    return o * lax.rsqrt(ss * (1.0 / HEAD_DIM) + EPS) * jax.nn.silu(g)


def _route(lg):
    lane = lax.broadcasted_iota(jnp.int32, lg.shape, 1).astype(F32)
    big = float(1 << 20)
    neg = -jnp.inf
    gmask = lane < N_GROUPS
    lgm = jnp.where(gmask, lg, neg)
    gmax = jnp.max(lgm, axis=1, keepdims=True)
    gsum = jnp.sum(jnp.where(gmask, jnp.exp(lgm - gmax), 0.0), axis=1, keepdims=True)
    gp = 1.0 / gsum
    gi = jnp.min(jnp.where(gmask & (lgm == gmax), lane, big), axis=1, keepdims=True)
    lo_lane = N_GROUPS + EXPERTS_PER_GROUP * gi
    emask = (lane >= lo_lane) & (lane < lo_lane + EXPERTS_PER_GROUP)
    l1 = jnp.max(jnp.where(emask, lg, neg), axis=1, keepdims=True)
    i1 = jnp.min(jnp.where(emask & (lg == l1), lane, big), axis=1, keepdims=True)
    mask2 = emask & (lane != i1)
    l2 = jnp.max(jnp.where(mask2, lg, neg), axis=1, keepdims=True)
    i2 = jnp.min(jnp.where(mask2 & (lg == l2), lane, big), axis=1, keepdims=True)
    e21 = jnp.exp(l2 - l1)
    w1 = gp / (1.0 + e21)
    w2 = gp * e21 / (1.0 + e21)
    e1 = i1 - N_GROUPS
    e2 = i2 - N_GROUPS
    return jnp.where(lane == 0, w1, jnp.where(lane == 1, w2, jnp.where(lane == 2, e1, jnp.where(lane == 3, e2, 0.0))))


def _merge_kernel(x_ref, ys_ref, hof_ref, hob_ref, hgg_ref, rof_ref, rob_ref, rgg_ref, gz_ref,
                  ml_ref, mc_ref, g2_ref, wglu_ref, wbs_ref, wbh_ref, wbr_ref, wout_ref, ones_ref,
                  wr_ref, br_ref, xo_ref, h2_ref, rt_ref, *, tm, tiles_per_b, n_ctx):
    d = x_ref.shape[1]
    is_ctx = _is_ctx_rows(tm, tiles_per_b, n_ctx)
    y = jax.nn.gelu(ys_ref[...], approximate=True)
    y = y * jax.nn.sigmoid(jnp.dot(y.astype(BF16), wglu_ref[...], preferred_element_type=F32))
    t_s5 = jnp.dot(y.astype(BF16), wbs_ref[...], preferred_element_type=F32)
    yh = _head_norm_gate(hof_ref[...] + hob_ref[...], hgg_ref[...], ones_ref[...])
    t_hg = jnp.dot(yh.astype(BF16), wbh_ref[...], preferred_element_type=F32)
    yr = _head_norm_gate(rof_ref[...] + rob_ref[...], rgg_ref[...], ones_ref[...])
    t_ret = jnp.dot(yr.astype(BF16), wbr_ref[...], preferred_element_type=F32)
    merged = (jax.nn.sigmoid(gz_ref[:, 0:d]) * t_s5 + jax.nn.sigmoid(gz_ref[:, d:2 * d]) * t_hg
              + jax.nn.sigmoid(gz_ref[:, 2 * d:3 * d]) * t_ret)
    mix = jnp.dot(merged.astype(BF16), wout_ref[...], preferred_element_type=F32)
    xn = x_ref[...] + _mod_rows(ml_ref, mc_ref, 2, is_ctx) * mix
    xo_ref[...] = xn
    h2 = _rmsnorm_rows(xn, g2_ref[...])
    h2 = h2 * (1.0 + _mod_rows(ml_ref, mc_ref, 4, is_ctx)) + _mod_rows(ml_ref, mc_ref, 3, is_ctx)
    _store_token_tiles(h2_ref, h2)
    h_hi = h2.astype(BF16)
    h_lo = (h2 - h_hi.astype(F32)).astype(BF16)
    p_hi = jnp.dot(h_hi, wr_ref[...], preferred_element_type=F32)
    p_lo = jnp.dot(h_lo, wr_ref[:, 0:LANES], preferred_element_type=F32)
    rt_ref[...] = _route(p_hi[:, 0:LANES] + p_hi[:, LANES:2 * LANES] + p_lo + br_ref[...])


def _merge(x2, ys5, hg_of, hg_ob, hg2, ret_of, ret_ob, ret2, gz, mod_l, mod_c, g2, wts, t_len, n_ctx):
    m, d = x2.shape
    w_s5 = ys5.shape[1]
    w_h = hg_of.shape[1]
    tm = _row_tile(t_len)
    tpb = t_len // tm
    wglu, wbs, wbh, wbr, wout, ones_bd, wr, br = wts

    def rows(width, col=0):
        return pl.BlockSpec((tm, width), lambda i: (i, col))

    def const(a):
        nd = a.ndim
        return pl.BlockSpec(a.shape, lambda i: (0,) * nd)

    kern = functools.partial(_merge_kernel, tm=tm, tiles_per_b=tpb, n_ctx=n_ctx)
    return pl.pallas_call(
        kern,
        grid=(m // tm,),
        in_specs=[rows(d), rows(w_s5), rows(w_h), rows(w_h), rows(w_h, 4), rows(w_h), rows(w_h), rows(w_h, 3),
                  rows(3 * d),
                  pl.BlockSpec((1, N_MOD, d), lambda i: (i // tpb, 0, 0)),
                  pl.BlockSpec((1, N_MOD, d), lambda i: (0, 0, 0)),
                  const(g2), const(wglu), const(wbs), const(wbh), const(wbr), const(wout), const(ones_bd),
                  const(wr), const(br)],
        out_specs=[rows(d), pl.BlockSpec((tm * SUBLANES, LANES), lambda i: (i, 0)), rows(LANES)],
        out_shape=[jax.ShapeDtypeStruct((m, d), F32), jax.ShapeDtypeStruct((m * SUBLANES, LANES), F32),
                   jax.ShapeDtypeStruct((m, LANES), F32)],
        compiler_params=_cparams("parallel"),
    )(x2, ys5, hg_of, hg_ob, hg2, ret_of, ret_ob, ret2, gz, mod_l, mod_c, g2, wglu, wbs, wbh, wbr, wout,
      ones_bd, wr, br)


def _ffn_kernel(lay_ref, be_ref, nu_ref, x_ref, wg_ref, wu_ref, wd_ref, o_ref, wg_scr, wu_scr, wd_scr):
    i = pl.program_id(0)
    prev = be_ref[jnp.maximum(i - 1, 0)]

    @pl.when((i == 0) | (be_ref[i] != prev))
    def _():
        wg_scr[...] = wg_ref[0, 0].astype(BF16)
        wu_scr[...] = wu_ref[0, 0].astype(BF16)
        wd_scr[...] = wd_ref[0, 0].astype(BF16)

    @pl.when(i < nu_ref[0])
    def _():
        xb = _load_token_tiles(x_ref).astype(BF16)
        hid = (jax.nn.silu(jnp.dot(xb, wg_scr[...], preferred_element_type=F32))
               * jnp.dot(xb, wu_scr[...], preferred_element_type=F32))
        _store_token_tiles(o_ref, jnp.dot(hid.astype(BF16), wd_scr[...], preferred_element_type=F32))

    @pl.when(i >= nu_ref[0])
    def _():
        o_ref[...] = jnp.zeros_like(o_ref)


def _expert_ffn(layer, blk_e, n_used, xg, wg, wu, wd):
    d, f = wg.shape[2], wg.shape[3]
    n_slots = xg.shape[0] // SUBLANES
    bm = MOE_BLOCK
    tile_rows = pl.BlockSpec((bm * SUBLANES, LANES), lambda i, lay, be, nu: (i, 0))
    grid_spec = pltpu.PrefetchScalarGridSpec(
        num_scalar_prefetch=3,
        grid=(n_slots // bm,),
        in_specs=[
            tile_rows,
            pl.BlockSpec((1, 1, d, f), lambda i, lay, be, nu: (lay[0], be[i], 0, 0)),
            pl.BlockSpec((1, 1, d, f), lambda i, lay, be, nu: (lay[0], be[i], 0, 0)),
            pl.BlockSpec((1, 1, f, d), lambda i, lay, be, nu: (lay[0], be[i], 0, 0)),
        ],
        out_specs=tile_rows,
        scratch_shapes=[pltpu.VMEM((d, f), BF16), pltpu.VMEM((d, f), BF16), pltpu.VMEM((f, d), BF16)],
    )
    return pl.pallas_call(
        _ffn_kernel,
        grid_spec=grid_spec,
        out_shape=jax.ShapeDtypeStruct((n_slots * SUBLANES, LANES), F32),
        compiler_params=_cparams("arbitrary"),
    )(layer, blk_e, n_used, xg, wg, wu, wd)


def _combine_kernel(x_ref, y0_ref, y1_ref, rt_ref, ml_ref, mc_ref, o_ref, *, tm, tiles_per_b, n_ctx):
    is_ctx = _is_ctx_rows(tm, tiles_per_b, n_ctx)
    rt = rt_ref[...]
    y = rt[:, 0:1] * _load_token_tiles(y0_ref) + rt[:, 1:2] * _load_token_tiles(y1_ref)
    o_ref[...] = x_ref[...] + _mod_rows(ml_ref, mc_ref, 5, is_ctx) * y


def _combine(x2, y0, y1, route, mod_l, mod_c, t_len, n_ctx):
    m, d = x2.shape
    tm = _row_tile(t_len)
    tpb = t_len // tm
    kern = functools.partial(_combine_kernel, tm=tm, tiles_per_b=tpb, n_ctx=n_ctx)
    return pl.pallas_call(
        kern,
        grid=(m // tm,),
        in_specs=[pl.BlockSpec((tm, d), lambda i: (i, 0))]
        + [pl.BlockSpec((tm * SUBLANES, LANES), lambda i: (i, 0))] * 2
        + [pl.BlockSpec((tm, LANES), lambda i: (i, 0)),
           pl.BlockSpec((1, N_MOD, d), lambda i: (i // tpb, 0, 0)),
           pl.BlockSpec((1, N_MOD, d), lambda i: (0, 0, 0))],
        out_specs=pl.BlockSpec((tm, d), lambda i: (i, 0)),
        out_shape=jax.ShapeDtypeStruct((m, d), F32),
        compiler_params=_cparams("parallel"),
    )(x2, y0, y1, route, mod_l, mod_c)


def _final_norm_kernel(x_ref, g_ref, o_ref):
    o_ref[0] = _rmsnorm_rows(x_ref[0], g_ref[...])


def _final_norm(x3, g, n_ctx):
    bsz, t_len, d = x3.shape
    n_lat = t_len - n_ctx
    tm = n_ctx
    off = n_ctx // tm
    return pl.pallas_call(
        _final_norm_kernel,
        grid=(bsz, n_lat // tm),
        in_specs=[pl.BlockSpec((1, tm, d), lambda b, j: (b, j + off, 0)),
                  pl.BlockSpec((1, d), lambda b, j: (0, 0))],
        out_specs=pl.BlockSpec((1, tm, d), lambda b, j: (b, j, 0)),
        out_shape=jax.ShapeDtypeStruct((bsz, n_lat, d), F32),
        compiler_params=_cparams("parallel", "parallel"),
    )(x3, g.reshape(1, d))


def _dispatch_indices(eid, n_tok):
    a = n_tok * TOP_K
    bm = MOE_BLOCK
    flat_e = eid.reshape(a)
    experts = jnp.arange(N_EXPERTS, dtype=jnp.int32)
    onehot = flat_e[:, None] == experts[None, :]
    counts = jnp.sum(onehot, axis=0, dtype=jnp.int32)
    padded = (counts + bm - 1) // bm * bm
    pad_end = jnp.cumsum(padded)
    pad_start = pad_end - padded
    start = jnp.cumsum(counts) - counts
    order = jnp.argsort(flat_e).astype(jnp.int32)
    inv = jnp.argsort(order).astype(jnp.int32)
    shift = jnp.sum(jnp.where(onehot, (pad_start - start)[None, :], 0), axis=1, dtype=jnp.int32)
    pos = (inv + shift).reshape(n_tok, TOP_K)
    n_blocks = -(-a // bm) + N_EXPERTS
    blk_start = jnp.arange(n_blocks, dtype=jnp.int32) * bm
    blk_e = jnp.minimum(jnp.sum(pad_end[None, :] <= blk_start[:, None], axis=1, dtype=jnp.int32), N_EXPERTS - 1)
    slot = jnp.arange(n_blocks * bm, dtype=jnp.int32)
    slot_shift = jnp.repeat((pad_start - start)[blk_e], bm)
    src = jnp.clip(slot - slot_shift, 0, a - 1)
    slot_tok = order.at[src].get(mode='promise_in_bounds') // TOP_K
    n_used = (pad_end[-1] // bm).astype(jnp.int32).reshape(1)
    return slot_tok, pos, blk_e, n_used


def _gather_tiles(a, idx):
    tiles = a.reshape(-1, SUBLANES, LANES).at[idx].get(mode='promise_in_bounds')
    return tiles.reshape(-1, LANES)


def _moe(layer, h2t, route, wg, wu, wd):
    n_tok = route.shape[0]
    eid = route[:, 2:4].astype(jnp.int32)
    slot_tok, pos, blk_e, n_used = _dispatch_indices(eid, n_tok)
    yb = _expert_ffn(layer, blk_e, n_used, _gather_tiles(h2t, slot_tok), wg, wu, wd)
    return _gather_tiles(yb, pos[:, 0]), _gather_tiles(yb, pos[:, 1])


def kernel(x, c, ctx, c_ctx, w_ada, b_ada, norm1_g, norm2_g, w_in, s5_lam_re, s5_lam_im, s5_log_dt, s5_b_re, s5_b_im, s5_c_re, s5_c_im, s5_d, s5_w_glu, hgrn_lb_raw, w_branch_s5, w_branch_hgrn, w_branch_ret, w_out, moe_w_group, moe_b_group, moe_w_expert, moe_b_expert, moe_w_gate, moe_w_up, moe_w_down, final_norm_g):
    bsz, n_lat, d = x.shape
    n_ctx = ctx.shape[1]
    t_len = n_ctx + n_lat
    depth = w_ada.shape[0]
    m = bsz * t_len
    w_s5 = s5_d.shape[1]
    w_hg = w_branch_hgrn.shape[1]
    w_ret = w_branch_ret.shape[1]
    n_heads = w_ret // HEAD_DIM

    pad_rows = (-(bsz + 1)) % SUBLANES
    cc = jnp.concatenate([c, c_ctx[None, :], jnp.zeros((pad_rows, d), F32)], axis=0)
    mods = _ada_tables(cc, w_ada, b_ada).reshape(depth, bsz + 1 + pad_rows, N_MOD, d)

    p_lb = jax.nn.softmax(hgrn_lb_raw.astype(F32), axis=0)
    lower_bounds = jnp.cumsum(p_lb, axis=0) - p_lb[0:1]
    ret_consts = _ret_constants(n_heads, t_len, n_ctx)
    ones_bd = jnp.kron(jnp.eye(w_hg // HEAD_DIM, dtype=F32), jnp.ones((HEAD_DIM, HEAD_DIM), F32)).astype(BF16)

    xs = jnp.concatenate([ctx, x], axis=1).reshape(m, d)
    off_hg = w_s5
    off_ret = off_hg + 5 * w_hg
    off_gz = off_ret + 4 * w_ret
    for l in range(depth):
        mod_l = mods[l, :bsz]
        mod_c = mods[l, bsz:bsz + 1]
        wl = w_in[l].astype(BF16)
        w_parts = (wl[:, :off_hg], wl[:, off_hg:off_ret], wl[:, off_ret:off_gz], wl[:, off_gz:])
        u, hg, ret, gz = _project(xs, norm1_g[l], mod_l, mod_c, w_parts, t_len, n_ctx)

        s5w = _s5_weights(s5_lam_re[l], s5_lam_im[l], s5_log_dt[l], s5_b_re[l], s5_b_im[l],
                          s5_c_re[l], s5_c_im[l], s5_d[l])
        ys5 = _s5_mix(u.reshape(bsz, t_len, w_s5), *s5w, n_ctx).reshape(m, w_s5)
        hg_of, hg_ob = _hg_mix(hg.reshape(bsz, t_len, 5 * w_hg), lower_bounds[l], n_ctx)
        ret_of, ret_ob = _ret_mix(ret.reshape(bsz, t_len, 4 * w_ret), ret_consts, n_ctx)

        wr = jnp.concatenate([moe_w_group[l], moe_w_expert[l],
                              jnp.zeros((d, LANES - N_GROUPS - N_EXPERTS), F32)], axis=1)
        br = jnp.concatenate([moe_b_group[l], moe_b_expert[l],
                              jnp.zeros((LANES - N_GROUPS - N_EXPERTS,), F32)])[None, :]
        wr_hi = wr.astype(BF16)
        wr = jnp.concatenate([wr_hi, (wr - wr_hi.astype(F32)).astype(BF16)], axis=1)
        wts = (s5_w_glu[l].astype(BF16), w_branch_s5[l].astype(BF16), w_branch_hgrn[l].astype(BF16),
               w_branch_ret[l].astype(BF16), w_out[l].astype(BF16), ones_bd, wr, br)
        xs, h2, route = _merge(xs, ys5, hg_of.reshape(m, w_hg), hg_ob.reshape(m, w_hg), hg,
                               ret_of.reshape(m, w_ret), ret_ob.reshape(m, w_ret), ret, gz,
                               mod_l, mod_c, norm2_g[l].reshape(1, d), wts, t_len, n_ctx)
        y0, y1 = _moe(jnp.full((1,), l, jnp.int32), h2, route, moe_w_gate, moe_w_up, moe_w_down)
        xs = _combine(xs, y0, y1, route, mod_l, mod_c, t_len, n_ctx)
    return _final_norm(xs.reshape(bsz, t_len, d), final_norm_g, n_ctx)
```

```python
import functools

import jax
import jax.numpy as jnp
import numpy as np
from jax import lax
from jax.experimental import pallas as pl
from jax.experimental.pallas import tpu as pltpu

F32 = jnp.float32
BF16 = jnp.bfloat16
HIGHEST = lax.Precision.HIGHEST

LANES = 128
SUBLANES = 8
VMEM_LIMIT = 56 * 1024 * 1024

EPS = 1e-6
N_MOD = 6
GRID_W = 64
HEAD_DIM = 64
PAIR = 2 * HEAD_DIM
S5_GROUP = 16
S5_STATE = 64
S5_CHUNK = 16
HG_CHUNK = 64
HG_HALF = HG_CHUNK // 2
HG_BLOCK = 256
RET_BLOCK = 256
ROPE_BASE = 10000.0
N_GROUPS = 4
EXPERTS_PER_GROUP = 8
N_EXPERTS = N_GROUPS * EXPERTS_PER_GROUP
TOP_K = 2
MOE_BLOCK = 512
ROW_GROUP = 8


def _cparams(*sem):
    return pltpu.CompilerParams(dimension_semantics=sem, vmem_limit_bytes=VMEM_LIMIT)


def _row_tile(t_len):
    for tm in (544, 512, 384, 272, 256, 128, 64, 32, 16, 8):
        if t_len % tm == 0:
            return tm
    raise ValueError(f"unsupported stream length {t_len}")


def _ada_kernel(c_ref, w_ref, b_ref, o_ref):
    a = jax.nn.silu(c_ref[...]).astype(BF16)
    o_ref[0] = jnp.dot(a, w_ref[0].astype(BF16), preferred_element_type=F32) + b_ref[0]


def _ada_tables(cc, w_ada, b_ada):
    depth, d, n = w_ada.shape
    rows = cc.shape[0]
    tn = 1536 if n % 1536 == 0 else n
    return pl.pallas_call(
        _ada_kernel,
        grid=(depth, n // tn),
        in_specs=[
            pl.BlockSpec((rows, d), lambda l, j: (0, 0)),
            pl.BlockSpec((1, d, tn), lambda l, j: (l, 0, j)),
            pl.BlockSpec((1, 1, tn), lambda l, j: (l, 0, j)),
        ],
        out_specs=pl.BlockSpec((1, rows, tn), lambda l, j: (l, 0, j)),
        out_shape=jax.ShapeDtypeStruct((depth, rows, n), F32),
        compiler_params=_cparams("parallel", "parallel"),
    )(cc, w_ada, b_ada.reshape(depth, 1, n))


def _mod_rows(ml_ref, mc_ref, idx, is_ctx):
    return jnp.where(is_ctx, mc_ref[0, idx:idx + 1, :], ml_ref[0, idx:idx + 1, :])


def _is_ctx_rows(tm, tiles_per_b, n_ctx):
    jt = pl.program_id(0) % tiles_per_b
    row = jt * tm + lax.broadcasted_iota(jnp.int32, (tm, 1), 0)
    return row < n_ctx


def _store_token_tiles(ref, val):
    n_rows = val.shape[0]
    for k in range(val.shape[1] // LANES):
        ref[pl.ds(k, n_rows, stride=SUBLANES), :] = val[:, LANES * k:LANES * (k + 1)]


def _load_token_tiles(ref):
    n_rows = ref.shape[0] // SUBLANES
    return jnp.concatenate([ref[pl.ds(k, n_rows, stride=SUBLANES), :] for k in range(SUBLANES)], axis=1)


def _rmsnorm_rows(x, g):
    return x * lax.rsqrt(jnp.mean(x * x, axis=-1, keepdims=True) + EPS) * g


def _proj_kernel(x_ref, g_ref, ml_ref, mc_ref, wu_ref, wh_ref, wr_ref, wg_ref,
                 u_ref, hg_ref, ret_ref, gz_ref, *, tm, tiles_per_b, n_ctx):
    is_ctx = _is_ctx_rows(tm, tiles_per_b, n_ctx)
    y = _rmsnorm_rows(x_ref[...], g_ref[...])
    y = y * (1.0 + _mod_rows(ml_ref, mc_ref, 1, is_ctx)) + _mod_rows(ml_ref, mc_ref, 0, is_ctx)
    h = y.astype(BF16)
    u_ref[...] = jnp.dot(h, wu_ref[...], preferred_element_type=F32)
    hg_ref[...] = jnp.dot(h, wh_ref[...], preferred_element_type=F32)
    ret_ref[...] = jnp.dot(h, wr_ref[...], preferred_element_type=F32)
    gz_ref[...] = jnp.dot(h, wg_ref[...], preferred_element_type=F32).astype(gz_ref.dtype)


def _project(x2, g, mod_l, mod_c, w_parts, t_len, n_ctx):
    m, d = x2.shape
    tm = _row_tile(t_len)
    tpb = t_len // tm
    kern = functools.partial(_proj_kernel, tm=tm, tiles_per_b=tpb, n_ctx=n_ctx)

    def const(a):
        return pl.BlockSpec(a.shape, lambda i: (0, 0), pipeline_mode=pl.Buffered(1))

    out_dtypes = (F32, F32, F32, BF16)
    return pl.pallas_call(
        kern,
        grid=(m // tm,),
        in_specs=[
            pl.BlockSpec((tm, d), lambda i: (i, 0)),
            pl.BlockSpec((1, d), lambda i: (0, 0)),
            pl.BlockSpec((1, N_MOD, d), lambda i: (i // tpb, 0, 0)),
            pl.BlockSpec((1, N_MOD, d), lambda i: (0, 0, 0)),
        ] + [const(w) for w in w_parts],
        out_specs=[pl.BlockSpec((tm, w.shape[1]), lambda i: (i, 0)) for w in w_parts],
        out_shape=[jax.ShapeDtypeStruct((m, w.shape[1]), dt) for w, dt in zip(w_parts, out_dtypes)],
        compiler_params=_cparams("parallel"),
    )(x2, g.reshape(1, d), mod_l, mod_c, *w_parts)


def _s5_weights(lam_re, lam_im, log_dt, b_re, b_im, c_re, c_im, d_skip):
    n_dir, g_n, p_n = lam_re.shape
    h_n = b_re.shape[-1]
    cn = S5_CHUNK
    lr, li = lam_re.astype(F32), lam_im.astype(F32)
    dt = jnp.exp(log_dt.astype(F32))[..., None]
    mag = jnp.exp(lr * dt)
    ar = mag * jnp.cos(li * dt)
    ai = mag * jnp.sin(li * dt)
    den = lr * lr + li * li
    zr = ((ar - 1.0) * lr + ai * li) / den
    zi = (ai * lr - (ar - 1.0) * li) / den
    bbr = zr[..., None] * b_re - zi[..., None] * b_im
    bbi = zr[..., None] * b_im + zi[..., None] * b_re
    j = jnp.arange(cn + 1, dtype=F32)[:, None, None, None]
    pmag = jnp.exp(lr[None] * dt[None] * j)
    pang = li[None] * dt[None] * j
    pr = pmag * jnp.cos(pang)
    pi = pmag * jnp.sin(pang)
    wr = pr[..., None] * bbr[None] - pi[..., None] * bbi[None]
    wi = pr[..., None] * bbi[None] + pi[..., None] * bbr[None]
    kj = (jnp.einsum('dghp,jdgpk->jdghk', c_re, wr, precision=HIGHEST)
          - jnp.einsum('dghp,jdgpk->jdghk', c_im, wi, precision=HIGHEST))
    s_idx = jnp.arange(cn)[:, None]
    t_idx = jnp.arange(cn)[None, :]
    lag = t_idx - s_idx
    kf = kj[:cn, 0][jnp.clip(lag, 0, cn - 1)]
    kb = kj[:cn, 1][jnp.clip(-lag, 0, cn - 1)]
    kf = jnp.where((lag >= 0)[:, :, None, None, None], kf, 0.0)
    kb = jnp.where((lag <= 0)[:, :, None, None, None], kb, 0.0)
    dsk = d_skip.astype(F32).reshape(g_n, h_n)
    eye_h = jnp.eye(h_n, dtype=F32)
    diag = (lag == 0)[:, :, None, None, None] * (dsk[None, None, :, :, None] * eye_h[None, None, None])
    m_full = kf + kb + diag
    m_mat = m_full.transpose(2, 0, 4, 1, 3).reshape(g_n, cn * h_n, cn * h_n)
    wf_r = wr[:cn, 0][::-1]
    wf_i = wi[:cn, 0][::-1]
    wb_r = wr[:cn, 1]
    wb_i = wi[:cn, 1]
    w_in = jnp.stack([wf_r, wf_i, wb_r, wb_i], axis=0)
    w_in = w_in.transpose(2, 1, 4, 0, 3).reshape(g_n, cn * h_n, 4 * p_n)
    def out_map(d, powers):
        prd, pid = pr[powers, d], pi[powers, d]
        w_re = c_re[d][None] * prd[:, :, None, :] - c_im[d][None] * pid[:, :, None, :]
        w_im = c_re[d][None] * pid[:, :, None, :] + c_im[d][None] * prd[:, :, None, :]
        return w_re, -w_im
    t_arr = jnp.arange(cn)
    of_r, of_i = out_map(0, t_arr + 1)
    ob_r, ob_i = out_map(1, cn - t_arr)
    w_out = jnp.stack([of_r, of_i, ob_r, ob_i], axis=0)
    w_out = w_out.transpose(2, 0, 4, 1, 3).reshape(g_n, 4 * p_n, cn * h_n)
    slot = jax.nn.one_hot(jnp.arange(g_n) % 2, 2, dtype=F32)
    w_in = w_in.reshape(g_n, cn * h_n, 4, 1, p_n) * slot[:, None, None, :, None]
    w_in = w_in.reshape(g_n, cn * h_n, 8 * p_n)
    w_out = w_out.reshape(g_n, 4, 1, p_n, cn * h_n) * slot[:, None, :, None, None]
    w_out = w_out.reshape(g_n, 8 * p_n, cn * h_n)
    a_mat = jnp.stack([pr[cn, 0].reshape(-1), pi[cn, 0].reshape(-1),
                       pr[cn, 1].reshape(-1), pi[cn, 1].reshape(-1)], axis=0)
    w1 = jnp.concatenate([m_mat, w_in], axis=-1).astype(BF16)
    return w1, w_out.astype(BF16), a_mat


def _s5_fold_perm():
    j, g8, h = np.meshgrid(np.arange(8), np.arange(8), np.arange(S5_GROUP), indexing='ij')
    src = (j * 8 + g8) * S5_GROUP + h
    dst = (g8 * 8 + j) * S5_GROUP + h
    perm = np.zeros((8 * LANES, 8 * LANES), np.float32)
    perm[src.reshape(-1), dst.reshape(-1)] = 1.0
    return jnp.asarray(perm, dtype=BF16)


def _s5_kernel(u_ref, perm_ref, w1_ref, wo_ref, a_ref, y_ref, ug_scr, yg_scr, xfr, xfi, xbr, xbi,
               *, n_chunks, n_ctx_chunks, n_groups):
    cn = S5_CHUNK
    blk = cn * S5_GROUP
    tok_per_slab = LANES // S5_GROUP
    for c in range(cn // tok_per_slab):
        slabs = [u_ref[0, pl.ds(tok_per_slab * c + j, n_chunks, stride=cn), :] for j in range(tok_per_slab)]
        s = jnp.concatenate(slabs, axis=1).astype(BF16)
        up = jnp.dot(s, perm_ref[...], preferred_element_type=F32).astype(BF16)
        for g in range(n_groups):
            ug_scr[g, :, LANES * c:LANES * (c + 1)] = up[:, LANES * g:LANES * (g + 1)]
    xs = (xfr, xfi, xbr, xbi)
    for g in range(n_groups):
        z = jnp.dot(ug_scr[g], w1_ref[g], preferred_element_type=F32)
        yg_scr[g] = z[:, :blk]
        lanes = slice(LANES * (g // 2), LANES * (g // 2 + 1))
        for k, xr in enumerate(xs):
            piece = z[:, blk + LANES * k:blk + LANES * (k + 1)]
            if g % 2 == 0:
                xr[:, lanes] = piece
            else:
                xr[:, lanes] += piece
    arf, aif, arb, aib = a_ref[0:1, :], a_ref[1:2, :], a_ref[2:3, :], a_ref[3:4, :]

    def body(i, carry):
        fr, fi, br, bi = carry
        nb = jnp.where(i < n_ctx_chunks, n_ctx_chunks - 1 - i, n_chunks - 1 - (i - n_ctx_chunks))
        x_fr, x_fi = xfr[pl.ds(i, 1), :], xfi[pl.ds(i, 1), :]
        x_br, x_bi = xbr[pl.ds(nb, 1), :], xbi[pl.ds(nb, 1), :]
        xfr[pl.ds(i, 1), :] = fr
        xfi[pl.ds(i, 1), :] = fi
        xbr[pl.ds(nb, 1), :] = br
        xbi[pl.ds(nb, 1), :] = bi
        return (arf * fr - aif * fi + x_fr, arf * fi + aif * fr + x_fi,
                arb * br - aib * bi + x_br, arb * bi + aib * br + x_bi)

    zero = jnp.zeros((1, xfr.shape[1]), F32)
    lax.fori_loop(0, n_chunks, body, (zero, zero, zero, zero))
    for g in range(n_groups):
        lanes = slice(LANES * (g // 2), LANES * (g // 2 + 1))
        hp = jnp.concatenate([xr[:, lanes] for xr in xs], axis=1).astype(BF16)
        yg_scr[g] += jnp.dot(hp, wo_ref[g], preferred_element_type=F32)
    for c in range(cn // tok_per_slab):
        ycat = jnp.concatenate([yg_scr[g, :, LANES * c:LANES * (c + 1)] for g in range(n_groups)], axis=1)
        hi = ycat.astype(BF16)
        lo = (ycat - hi.astype(F32)).astype(BF16)
        r = _dot_nt(hi, perm_ref[...]) + _dot_nt(lo, perm_ref[...])
        for j in range(tok_per_slab):
            y_ref[0, pl.ds(tok_per_slab * c + j, n_chunks, stride=cn), :] = r[:, LANES * j:LANES * (j + 1)]


def _s5_mix(u, w1, w_out, a_mat, n_ctx):
    bsz, t_len, width = u.shape
    g_n = width // S5_GROUP
    cn = S5_CHUNK
    n_chunks = t_len // cn
    blk = cn * S5_GROUP
    perm = _s5_fold_perm()
    gs = LANES // S5_GROUP
    state_w = gs * S5_STATE
    kern = functools.partial(_s5_kernel, n_chunks=n_chunks, n_ctx_chunks=n_ctx // cn, n_groups=gs)
    return pl.pallas_call(
        kern,
        grid=(bsz, g_n // gs),
        in_specs=[pl.BlockSpec((1, t_len, LANES), lambda b, s: (b, 0, s)),
                  pl.BlockSpec(perm.shape, lambda b, s: (0, 0), pipeline_mode=pl.Buffered(1)),
                  pl.BlockSpec((gs,) + w1.shape[1:], lambda b, s: (s, 0, 0)),
                  pl.BlockSpec((gs,) + w_out.shape[1:], lambda b, s: (s, 0, 0)),
                  pl.BlockSpec((4, state_w), lambda b, s: (0, s))],
        out_specs=pl.BlockSpec((1, t_len, LANES), lambda b, s: (b, 0, s)),
        out_shape=jax.ShapeDtypeStruct((bsz, t_len, width), F32),
        scratch_shapes=[pltpu.VMEM((gs, n_chunks, blk), BF16), pltpu.VMEM((gs, n_chunks, blk), F32)]
        + [pltpu.VMEM((n_chunks, state_w), F32)] * 4,
        compiler_params=_cparams("parallel", "parallel"),
    )(u, perm, w1, w_out, a_mat)


def _head0_lanes(shape):
    return lax.broadcasted_iota(jnp.int32, shape, len(shape) - 1) < HEAD_DIM


def _stack_heads(q):
    m0 = _head0_lanes(q.shape)
    return jnp.concatenate([jnp.where(m0, q, 0.0), jnp.where(m0, 0.0, q)], axis=-2)


def _unstack_heads(o2):
    c = o2.shape[-2] // 2
    top, bottom = o2[..., :c, :], o2[..., c:, :]
    return jnp.where(_head0_lanes(top.shape), top, bottom)


def _pair_block_mask():
    r = lax.broadcasted_iota(jnp.int32, (PAIR, PAIR), 0) < HEAD_DIM
    c = lax.broadcasted_iota(jnp.int32, (PAIR, PAIR), 1) < HEAD_DIM
    return r == c


def _dot_nt(a, b):
    return lax.dot_general(a, b, (((1,), (1,)), ((), ())), preferred_element_type=F32)


def _dot_tn(a, b):
    return lax.dot_general(a, b, (((0,), (0,)), ((), ())), preferred_element_type=F32)


def _bwd_block(j, n_ctx_blocks, n_blocks):
    return jnp.where(j < n_ctx_blocks, n_ctx_blocks - 1 - j, n_blocks - 1 - (j - n_ctx_blocks))


def _bdot_nt(a, b):
    return lax.dot_general(a, b, (((2,), (2,)), ((0,), (0,))), preferred_element_type=F32)


def _bdot_nn(a, b):
    return lax.dot_general(a, b, (((2,), (1,)), ((0,), (0,))), preferred_element_type=F32)


def _hg_block(q, z, v, lb, ht, reverse):
    rows = q.shape[0]
    c = HG_CHUNK
    hh = HG_HALF
    n = rows // c
    f = lb + (1.0 - lb) * jax.nn.sigmoid(z)
    logf = jnp.log(f)
    kk = 1.0 - f
    ri = lax.broadcasted_iota(jnp.int32, (rows, rows), 0)
    ci = lax.broadcasted_iota(jnp.int32, (rows, rows), 1)
    same_chunk = (ri // c) == (ci // c)
    tri = jnp.where(same_chunk & ((ci >= ri) if reverse else (ci <= ri)), 1.0, 0.0).astype(BF16)
    l1 = logf.astype(BF16)
    r1 = logf - l1.astype(F32)
    l2 = r1.astype(BF16)
    l3 = (r1 - l2.astype(F32)).astype(BF16)
    b3 = jnp.dot(tri, jnp.concatenate([l1, l2, l3], axis=1), preferred_element_type=F32)
    b = b3[:, 0:PAIR] + b3[:, PAIR:2 * PAIR] + b3[:, 2 * PAIR:3 * PAIR]
    b4, q4, k4 = (a.reshape(n, c, PAIR) for a in (b, q, kk))
    vb = v.astype(BF16).reshape(n, c, PAIR)
    if reverse:
        first, second = slice(hh, c), slice(0, hh)
        r = b4[:, hh:hh + 1]
        bend = b4[:, 0:1]
    else:
        first, second = slice(0, hh), slice(hh, c)
        r = b4[:, hh - 1:hh]
        bend = b4[:, c - 1:c]
    b1, b2 = b4[:, first], b4[:, second]
    qd1 = q4[:, first] * jnp.exp(b1)
    qd2 = q4[:, second] * jnp.exp(b2 - r)
    k1d = k4[:, first] * jnp.exp(-b1)
    k2 = k4 * jnp.exp(r - b4)
    s1 = _bdot_nt(_stack_heads(qd1).astype(BF16), k1d.astype(BF16))
    s2 = _bdot_nt(_stack_heads(qd2).astype(BF16), k2.astype(BF16))
    t1 = lax.broadcasted_iota(jnp.int32, (n, 2 * hh, hh), 1) % hh
    c1 = lax.broadcasted_iota(jnp.int32, (n, 2 * hh, hh), 2)
    t2 = lax.broadcasted_iota(jnp.int32, (n, 2 * hh, c), 1) % hh
    c2 = lax.broadcasted_iota(jnp.int32, (n, 2 * hh, c), 2)
    if reverse:
        s1 = jnp.where(c1 >= t1, s1, 0.0)
        s2 = jnp.where(c2 >= t2, s2, 0.0)
    else:
        s1 = jnp.where(c1 <= t1, s1, 0.0)
        s2 = jnp.where(c2 <= t2 + hh, s2, 0.0)
    o1 = _unstack_heads(_bdot_nn(s1.astype(BF16), vb[:, first]))
    o2 = _unstack_heads(_bdot_nn(s2.astype(BF16), vb))
    o = jnp.concatenate([o2, o1] if reverse else [o1, o2], axis=1)
    ke = (k4 * jnp.exp(bend - b4)).astype(BF16)
    dec = jnp.exp(bend)
    bmask = _pair_block_mask()
    states = [None] * n
    for ch in (range(n - 1, -1, -1) if reverse else range(n)):
        states[ch] = ht.astype(BF16)
        ht = ht * dec[ch] + jnp.where(bmask, _dot_tn(vb[ch], ke[ch]), 0.0)
    o = o + _bdot_nt((q4 * jnp.exp(b4)).astype(BF16), jnp.stack(states, axis=0))
    return o.reshape(rows, PAIR), ht


def _hg_kernel(qf_ref, zf_ref, vf_ref, qb_ref, zb_ref, vb_ref, lb_ref, of_ref, ob_ref, hf_scr, hb_scr, *, n_pairs):
    @pl.when(pl.program_id(1) == 0)
    def _():
        hf_scr[...] = jnp.zeros_like(hf_scr)
        hb_scr[...] = jnp.zeros_like(hb_scr)

    for p in range(n_pairs):
        sl = slice(p * PAIR, (p + 1) * PAIR)
        o, hn = _hg_block(qf_ref[0, :, sl], zf_ref[0, :, sl], vf_ref[0, :, sl], lb_ref[0:1, sl], hf_scr[p], False)
        of_ref[0, :, sl] = o
        hf_scr[p] = hn
        o, hn = _hg_block(qb_ref[0, :, sl], zb_ref[0, :, sl], vb_ref[0, :, sl], lb_ref[1:2, sl], hb_scr[p], True)
        ob_ref[0, :, sl] = o
        hb_scr[p] = hn


def _hg_mix(hg, lb, n_ctx):
    bsz, t_len, total = hg.shape
    w = total // 5
    n_pairs = w // PAIR
    c = HG_BLOCK
    nc, ncc = t_len // c, n_ctx // c
    bw = functools.partial(_bwd_block, n_ctx_blocks=ncc, n_blocks=nc)

    def fspec(col):
        return pl.BlockSpec((1, c, w), lambda b, j: (b, j, col))

    def bspec(col):
        return pl.BlockSpec((1, c, w), lambda b, j: (b, bw(j), col))

    kern = functools.partial(_hg_kernel, n_pairs=n_pairs)
    return pl.pallas_call(
        kern,
        grid=(bsz, nc),
        in_specs=[fspec(0), fspec(1), fspec(3), bspec(0), bspec(2), bspec(3),
                  pl.BlockSpec((2, w), lambda b, j: (0, 0))],
        out_specs=[pl.BlockSpec((1, c, w), lambda b, j: (b, j, 0)),
                   pl.BlockSpec((1, c, w), lambda b, j: (b, bw(j), 0))],
        out_shape=[jax.ShapeDtypeStruct((bsz, t_len, w), F32)] * 2,
        scratch_shapes=[pltpu.VMEM((n_pairs, PAIR, PAIR), F32)] * 2,
        compiler_params=_cparams("parallel", "arbitrary"),
    )(hg, hg, hg, hg, hg, hg, lb)


def _rope_rows(x, cos, sin):
    hd2 = HEAD_DIM // 2
    lane = lax.broadcasted_iota(jnp.int32, x.shape, 1) % HEAD_DIM
    swapped = jnp.where(lane < hd2, pltpu.roll(x, PAIR - hd2, 1), pltpu.roll(x, hd2, 1))
    return x * cos + swapped * sin


def _ret_kernel(qf_ref, kf_ref, vf_ref, cf_ref, sf_ref, qb_ref, kb_ref, vb_ref, cb_ref, sb_ref,
                d_ref, gqf_ref, gkf_ref, gqb_ref, gkb_ref, gh_ref, of_ref, ob_ref, hf_scr, hb_scr, *, n_pairs):
    @pl.when(pl.program_id(1) == 0)
    def _():
        hf_scr[...] = jnp.zeros_like(hf_scr)
        hb_scr[...] = jnp.zeros_like(hb_scr)

    kscale = HEAD_DIM ** -0.5
    bmask = _pair_block_mask()
    for p in range(n_pairs):
        sl = slice(p * PAIR, (p + 1) * PAIR)
        q = _rope_rows(qf_ref[0, :, sl], cf_ref[...], sf_ref[...])
        k = _rope_rows(kf_ref[0, :, sl] * kscale, cf_ref[...], sf_ref[...])
        vb16 = vf_ref[0, :, sl].astype(BF16)
        s = _dot_nt(_stack_heads(q).astype(BF16), k.astype(BF16)) * d_ref[p]
        o = _unstack_heads(jnp.dot(s.astype(BF16), vb16, preferred_element_type=F32))
        ht = hf_scr[p]
        o = o + _dot_nt((q * gqf_ref[:, sl]).astype(BF16), ht.astype(BF16))
        upd = _dot_tn(vb16, (k * gkf_ref[:, sl]).astype(BF16))
        hf_scr[p] = ht * gh_ref[p] + jnp.where(bmask, upd, 0.0)
        of_ref[0, :, sl] = o

        q = _rope_rows(qb_ref[0, :, sl], cb_ref[...], sb_ref[...])
        k = _rope_rows(kb_ref[0, :, sl] * kscale, cb_ref[...], sb_ref[...])
        vb16 = vb_ref[0, :, sl].astype(BF16)
        ht = hb_scr[p]
        ob_ref[0, :, sl] = _dot_nt((q * gqb_ref[:, sl]).astype(BF16), ht.astype(BF16))
        upd = _dot_tn(vb16, (k * gkb_ref[:, sl]).astype(BF16))
        hb_scr[p] = ht * gh_ref[p] + jnp.where(bmask, upd, 0.0)


def _ret_constants(n_heads, t_len, n_ctx):
    c = RET_BLOCK
    log_gamma = jnp.log(1.0 - 2.0 ** (-5.0 - jnp.arange(n_heads, dtype=F32)))
    t = jnp.arange(c, dtype=F32)
    dist = jnp.abs(t[:, None] - t[None, :])
    dmat = jnp.exp(log_gamma[:, None, None] * dist[None])
    dmat = dmat.reshape(n_heads // 2, 2 * c, c)
    lg_lane = jnp.repeat(log_gamma, HEAD_DIM)[None, :]
    gqf = jnp.exp(lg_lane * (t[:, None] + 1.0))
    gkf = jnp.exp(lg_lane * (c - 1.0 - t[:, None]))
    gqb = jnp.exp(lg_lane * (c - t[:, None]))
    gkb = jnp.exp(lg_lane * t[:, None])
    gh = jnp.exp(lg_lane * float(c)).reshape(n_heads // 2, 1, PAIR)
    gh = jnp.broadcast_to(gh, (n_heads // 2, PAIR, PAIR))
    n_lat = t_len - n_ctx
    rows = n_lat // GRID_W
    row = jnp.repeat(jnp.arange(rows, dtype=F32), GRID_W)
    col = jnp.broadcast_to(jnp.arange(GRID_W, dtype=F32)[None, :], (rows, GRID_W)).reshape(-1)
    n_freq = HEAD_DIM // 4
    inv_freq = ROPE_BASE ** (-jnp.arange(n_freq, dtype=F32) / n_freq)
    ang = jnp.concatenate([row[:, None] * inv_freq, col[:, None] * inv_freq], axis=-1)
    cos_l, sin_l = jnp.cos(ang), jnp.sin(ang)
    cos_h = jnp.concatenate([cos_l, cos_l], axis=-1)
    sin_h = jnp.concatenate([-sin_l, sin_l], axis=-1)
    cos_t = jnp.concatenate([jnp.ones((n_ctx, HEAD_DIM), F32), cos_h], axis=0)
    sin_t = jnp.concatenate([jnp.zeros((n_ctx, HEAD_DIM), F32), sin_h], axis=0)
    cos_t = jnp.concatenate([cos_t, cos_t], axis=-1)
    sin_t = jnp.concatenate([sin_t, sin_t], axis=-1)
    return dmat, gqf, gkf, gqb, gkb, gh, cos_t, sin_t


def _ret_mix(ret, consts, n_ctx):
    dmat, gqf, gkf, gqb, gkb, gh, cos_t, sin_t = consts
    bsz, t_len, total = ret.shape
    w = total // 4
    n_pairs = w // PAIR
    c = RET_BLOCK
    nb, ncb = t_len // c, n_ctx // c
    bw = functools.partial(_bwd_block, n_ctx_blocks=ncb, n_blocks=nb)

    def fspec(col):
        return pl.BlockSpec((1, c, w), lambda b, j: (b, j, col))

    def bspec(col):
        return pl.BlockSpec((1, c, w), lambda b, j: (b, bw(j), col))

    def const(a):
        nd = a.ndim
        return pl.BlockSpec(a.shape, lambda b, j: (0,) * nd)

    rope_f = pl.BlockSpec((c, PAIR), lambda b, j: (j, 0))
    rope_b = pl.BlockSpec((c, PAIR), lambda b, j: (bw(j), 0))
    kern = functools.partial(_ret_kernel, n_pairs=n_pairs)
    return pl.pallas_call(
        kern,
        grid=(bsz, nb),
        in_specs=[fspec(0), fspec(1), fspec(2), rope_f, rope_f, bspec(0), bspec(1), bspec(2), rope_b, rope_b,
                  const(dmat), const(gqf), const(gkf), const(gqb), const(gkb), const(gh)],
        out_specs=[pl.BlockSpec((1, c, w), lambda b, j: (b, j, 0)),
                   pl.BlockSpec((1, c, w), lambda b, j: (b, bw(j), 0))],
        out_shape=[jax.ShapeDtypeStruct((bsz, t_len, w), F32)] * 2,
        scratch_shapes=[pltpu.VMEM((n_pairs, PAIR, PAIR), F32)] * 2,
        compiler_params=_cparams("parallel", "arbitrary"),
    )(ret, ret, ret, cos_t, sin_t, ret, ret, ret, cos_t, sin_t, dmat, gqf, gkf, gqb, gkb, gh)


def _head_norm_gate(o, g, ones_bd):
    sq = o * o
    hi = sq.astype(BF16)
    lo = (sq - hi.astype(F32)).astype(BF16)
    ss = jnp.dot(hi, ones_bd, preferred_element_type=F32) + jnp.dot(lo, ones_bd, preferred_element_type=F32)
    return o * lax.rsqrt(ss * (1.0 / HEAD_DIM) + EPS) * jax.nn.silu(g)


def _route(lg):
    lane = lax.broadcasted_iota(jnp.int32, lg.shape, 1).astype(F32)
    big = float(1 << 20)
    neg = -jnp.inf
    gmask = lane < N_GROUPS
    lgm = jnp.where(gmask, lg, neg)
    gmax = jnp.max(lgm, axis=1, keepdims=True)
    gsum = jnp.sum(jnp.where(gmask, jnp.exp(lgm - gmax), 0.0), axis=1, keepdims=True)
    gp = 1.0 / gsum
    gi = jnp.min(jnp.where(gmask & (lgm == gmax), lane, big), axis=1, keepdims=True)
    lo_lane = N_GROUPS + EXPERTS_PER_GROUP * gi
    emask = (lane >= lo_lane) & (lane < lo_lane + EXPERTS_PER_GROUP)
    l1 = jnp.max(jnp.where(emask, lg, neg), axis=1, keepdims=True)
    i1 = jnp.min(jnp.where(emask & (lg == l1), lane, big), axis=1, keepdims=True)
    mask2 = emask & (lane != i1)
    l2 = jnp.max(jnp.where(mask2, lg, neg), axis=1, keepdims=True)
    i2 = jnp.min(jnp.where(mask2 & (lg == l2), lane, big), axis=1, keepdims=True)
    e21 = jnp.exp(l2 - l1)
    w1 = gp / (1.0 + e21)
    w2 = gp * e21 / (1.0 + e21)
    e1 = i1 - N_GROUPS
    e2 = i2 - N_GROUPS
    return jnp.where(lane == 0, w1, jnp.where(lane == 1, w2, jnp.where(lane == 2, e1, jnp.where(lane == 3, e2, 0.0))))


def _merge_kernel(x_ref, ys_ref, hof_ref, hob_ref, hgg_ref, rof_ref, rob_ref, rgg_ref, gz_ref,
                  ml_ref, mc_ref, g2_ref, wglu_ref, wbs_ref, wbh_ref, wbr_ref, wout_ref, ones_ref,
                  wr_ref, br_ref, xo_ref, h2_ref, rt_ref, *, tm, tiles_per_b, n_ctx):
    d = x_ref.shape[1]
    is_ctx = _is_ctx_rows(tm, tiles_per_b, n_ctx)
    y = jax.nn.gelu(ys_ref[...], approximate=True)
    y = y * jax.nn.sigmoid(jnp.dot(y.astype(BF16), wglu_ref[...], preferred_element_type=F32))
    t_s5 = jnp.dot(y.astype(BF16), wbs_ref[...], preferred_element_type=F32)
    yh = _head_norm_gate(hof_ref[...] + hob_ref[...], hgg_ref[...], ones_ref[...])
    t_hg = jnp.dot(yh.astype(BF16), wbh_ref[...], preferred_element_type=F32)
    yr = _head_norm_gate(rof_ref[...] + rob_ref[...], rgg_ref[...], ones_ref[...])
    t_ret = jnp.dot(yr.astype(BF16), wbr_ref[...], preferred_element_type=F32)
    merged = (jax.nn.sigmoid(gz_ref[:, 0:d]) * t_s5 + jax.nn.sigmoid(gz_ref[:, d:2 * d]) * t_hg
              + jax.nn.sigmoid(gz_ref[:, 2 * d:3 * d]) * t_ret)
    mix = jnp.dot(merged.astype(BF16), wout_ref[...], preferred_element_type=F32)
    xn = x_ref[...] + _mod_rows(ml_ref, mc_ref, 2, is_ctx) * mix
    xo_ref[...] = xn
    h2 = _rmsnorm_rows(xn, g2_ref[...])
    h2 = h2 * (1.0 + _mod_rows(ml_ref, mc_ref, 4, is_ctx)) + _mod_rows(ml_ref, mc_ref, 3, is_ctx)
    _store_token_tiles(h2_ref, h2)
    h_hi = h2.astype(BF16)
    h_lo = (h2 - h_hi.astype(F32)).astype(BF16)
    p_hi = jnp.dot(h_hi, wr_ref[...], preferred_element_type=F32)
    p_lo = jnp.dot(h_lo, wr_ref[:, 0:LANES], preferred_element_type=F32)
    rt_ref[...] = _route(p_hi[:, 0:LANES] + p_hi[:, LANES:2 * LANES] + p_lo + br_ref[...])


def _merge(x2, ys5, hg_of, hg_ob, hg2, ret_of, ret_ob, ret2, gz, mod_l, mod_c, g2, wts, t_len, n_ctx):
    m, d = x2.shape
    w_s5 = ys5.shape[1]
    w_h = hg_of.shape[1]
    tm = _row_tile(t_len)
    tpb = t_len // tm
    wglu, wbs, wbh, wbr, wout, ones_bd, wr, br = wts

    def rows(width, col=0):
        return pl.BlockSpec((tm, width), lambda i: (i, col))

    def const(a):
        nd = a.ndim
        return pl.BlockSpec(a.shape, lambda i: (0,) * nd)

    kern = functools.partial(_merge_kernel, tm=tm, tiles_per_b=tpb, n_ctx=n_ctx)
    return pl.pallas_call(
        kern,
        grid=(m // tm,),
        in_specs=[rows(d), rows(w_s5), rows(w_h), rows(w_h), rows(w_h, 4), rows(w_h), rows(w_h), rows(w_h, 3),
                  rows(3 * d),
                  pl.BlockSpec((1, N_MOD, d), lambda i: (i // tpb, 0, 0)),
                  pl.BlockSpec((1, N_MOD, d), lambda i: (0, 0, 0)),
                  const(g2), const(wglu), const(wbs), const(wbh), const(wbr), const(wout), const(ones_bd),
                  const(wr), const(br)],
        out_specs=[rows(d), pl.BlockSpec((tm * SUBLANES, LANES), lambda i: (i, 0)), rows(LANES)],
        out_shape=[jax.ShapeDtypeStruct((m, d), F32), jax.ShapeDtypeStruct((m * SUBLANES, LANES), F32),
                   jax.ShapeDtypeStruct((m, LANES), F32)],
        compiler_params=_cparams("parallel"),
    )(x2, ys5, hg_of, hg_ob, hg2, ret_of, ret_ob, ret2, gz, mod_l, mod_c, g2, wglu, wbs, wbh, wbr, wout,
      ones_bd, wr, br)


def _ffn_kernel(lay_ref, be_ref, nu_ref, bs_ref, cnt_ref, ord_ref, h2_hbm, wg_ref, wu_ref, wd_ref, y2_hbm,
                wg_scr, wu_scr, wd_scr, xbuf, obuf, gsem, ssem):
    i = pl.program_id(0)
    nu = nu_ref[0]
    slot = i % 2
    tile = SUBLANES

    def gather_copy(rows_src, rows_dst, sl):
        return pltpu.make_async_copy(h2_hbm.at[rows_src, :], xbuf.at[sl, rows_dst, :], gsem.at[sl])

    def scatter_copy(rows_src, rows_dst, sl):
        return pltpu.make_async_copy(obuf.at[sl, rows_src, :], y2_hbm.at[rows_dst, :], ssem.at[sl])

    n_assign = ord_ref.shape[0]

    def n_groups(blk):
        return lax.shift_right_logical(cnt_ref[blk] + (ROW_GROUP - 1), ROW_GROUP.bit_length() - 1)

    def start_gather(blk, sl):
        base = bs_ref[blk]

        def body(g, carry):
            for j in range(ROW_GROUP):
                r = g * ROW_GROUP + j
                tok = lax.shift_right_logical(ord_ref[jnp.minimum(base + r, n_assign - 1)], 1)
                gather_copy(pl.ds(pl.multiple_of(tok * tile, tile), tile),
                            pl.ds(pl.multiple_of(r * tile, tile), tile), sl).start()
            return carry

        lax.fori_loop(0, n_groups(blk), body, 0)

    def wait_gather(blk, sl):
        n = pl.multiple_of(n_groups(blk) * (ROW_GROUP * tile), tile)
        gather_copy(pl.ds(0, n), pl.ds(0, n), sl).wait()

    def start_scatter(blk, sl):
        base = bs_ref[blk]
        cnt = cnt_ref[blk]

        def body(g, carry):
            for j in range(ROW_GROUP):
                r = g * ROW_GROUP + j
                a = ord_ref[jnp.minimum(base + r, n_assign - 1)]
                dst = jnp.where(r < cnt, a, n_assign + sl * ROW_GROUP + j)
                scatter_copy(pl.ds(pl.multiple_of(r * tile, tile), tile),
                             pl.ds(pl.multiple_of(dst * tile, tile), tile), sl).start()
            return carry

        lax.fori_loop(0, n_groups(blk), body, 0)

    def wait_scatter(blk, sl):
        n = pl.multiple_of(n_groups(blk) * (ROW_GROUP * tile), tile)
        scatter_copy(pl.ds(0, n), pl.ds(0, n), sl).wait()

    @pl.when(i == 0)
    def _():
        xbuf[...] = jnp.zeros_like(xbuf)

    @pl.when((i == 0) & (nu > 0))
    def _():
        start_gather(0, 0)

    @pl.when(i + 1 < nu)
    def _():
        start_gather(i + 1, 1 - slot)

    prev = be_ref[jnp.maximum(i - 1, 0)]

    @pl.when((i < nu) & ((i == 0) | (be_ref[i] != prev)))
    def _():
        wg_scr[...] = wg_ref[0, 0].astype(BF16)
        wu_scr[...] = wu_ref[0, 0].astype(BF16)
        wd_scr[...] = wd_ref[0, 0].astype(BF16)

    @pl.when(i < nu)
    def _():
        wait_gather(i, slot)
        xb = _load_token_tiles(xbuf.at[slot]).astype(BF16)
        hid = (jax.nn.silu(jnp.dot(xb, wg_scr[...], preferred_element_type=F32))
               * jnp.dot(xb, wu_scr[...], preferred_element_type=F32))
        y = jnp.dot(hid.astype(BF16), wd_scr[...], preferred_element_type=F32)
        _store_token_tiles(obuf.at[slot], y)
        start_scatter(i, slot)

    @pl.when((i >= 1) & (i < nu + 1))
    def _():
        wait_scatter(i - 1, 1 - slot)

    @pl.when((i == pl.num_programs(0) - 1) & (i < nu))
    def _():
        wait_scatter(i, slot)


def _expert_ffn(layer, blk_e, n_used, blk_start, blk_cnt, order, h2t, wg, wu, wd, n_tok):
    d, f = wg.shape[2], wg.shape[3]
    bm = MOE_BLOCK
    n_blocks = blk_e.shape[0] - 1

    def wspec(shape):
        return pl.BlockSpec((1, 1) + shape, lambda i, lay, be, *_: (lay[0], be[i], 0, 0))

    grid_spec = pltpu.PrefetchScalarGridSpec(
        num_scalar_prefetch=6,
        grid=(n_blocks,),
        in_specs=[pl.BlockSpec(memory_space=pl.ANY), wspec((d, f)), wspec((d, f)), wspec((f, d))],
        out_specs=pl.BlockSpec(memory_space=pl.ANY),
        scratch_shapes=[pltpu.VMEM((d, f), BF16), pltpu.VMEM((d, f), BF16), pltpu.VMEM((f, d), BF16),
                        pltpu.VMEM((2, bm * SUBLANES, LANES), F32), pltpu.VMEM((2, bm * SUBLANES, LANES), F32),
                        pltpu.SemaphoreType.DMA((2,)), pltpu.SemaphoreType.DMA((2,))],
    )
    return pl.pallas_call(
        _ffn_kernel,
        grid_spec=grid_spec,
        out_shape=jax.ShapeDtypeStruct(((n_tok * TOP_K + 2 * ROW_GROUP) * SUBLANES, LANES), F32),
        compiler_params=_cparams("arbitrary"),
    )(layer, blk_e, n_used, blk_start, blk_cnt, order, h2t, wg, wu, wd)


def _combine_kernel(x_ref, y2_ref, rt_ref, ml_ref, mc_ref, o_ref, *, tm, tiles_per_b, n_ctx):
    is_ctx = _is_ctx_rows(tm, tiles_per_b, n_ctx)
    rt = rt_ref[...]
    step = TOP_K * SUBLANES
    y0 = jnp.concatenate([y2_ref[pl.ds(k, tm, stride=step), :] for k in range(SUBLANES)], axis=1)
    y1 = jnp.concatenate([y2_ref[pl.ds(SUBLANES + k, tm, stride=step), :] for k in range(SUBLANES)], axis=1)
    y = rt[:, 0:1] * y0 + rt[:, 1:2] * y1
    o_ref[...] = x_ref[...] + _mod_rows(ml_ref, mc_ref, 5, is_ctx) * y


def _combine(x2, y2, route, mod_l, mod_c, t_len, n_ctx):
    m, d = x2.shape
    tm = _row_tile(t_len)
    tpb = t_len // tm
    kern = functools.partial(_combine_kernel, tm=tm, tiles_per_b=tpb, n_ctx=n_ctx)
    return pl.pallas_call(
        kern,
        grid=(m // tm,),
        in_specs=[pl.BlockSpec((tm, d), lambda i: (i, 0)),
                  pl.BlockSpec((tm * TOP_K * SUBLANES, LANES), lambda i: (i, 0)),
                  pl.BlockSpec((tm, LANES), lambda i: (i, 0)),
                  pl.BlockSpec((1, N_MOD, d), lambda i: (i // tpb, 0, 0)),
                  pl.BlockSpec((1, N_MOD, d), lambda i: (0, 0, 0))],
        out_specs=pl.BlockSpec((tm, d), lambda i: (i, 0)),
        out_shape=jax.ShapeDtypeStruct((m, d), F32),
        compiler_params=_cparams("parallel"),
    )(x2, y2, route, mod_l, mod_c)


def _final_norm_kernel(x_ref, g_ref, o_ref):
    o_ref[0] = _rmsnorm_rows(x_ref[0], g_ref[...])


def _final_norm(x3, g, n_ctx):
    bsz, t_len, d = x3.shape
    n_lat = t_len - n_ctx
    tm = n_ctx
    off = n_ctx // tm
    return pl.pallas_call(
        _final_norm_kernel,
        grid=(bsz, n_lat // tm),
        in_specs=[pl.BlockSpec((1, tm, d), lambda b, j: (b, j + off, 0)),
                  pl.BlockSpec((1, d), lambda b, j: (0, 0))],
        out_specs=pl.BlockSpec((1, tm, d), lambda b, j: (b, j, 0)),
        out_shape=jax.ShapeDtypeStruct((bsz, n_lat, d), F32),
        compiler_params=_cparams("parallel", "parallel"),
    )(x3, g.reshape(1, d))


def _dispatch_tables(eid, n_tok):
    a = n_tok * TOP_K
    bm = MOE_BLOCK
    flat_e = eid.reshape(a)
    experts = jnp.arange(N_EXPERTS, dtype=jnp.int32)
    counts = jnp.sum(flat_e[:, None] == experts[None, :], axis=0, dtype=jnp.int32)
    nblk_e = (counts + bm - 1) // bm
    blk_end = jnp.cumsum(nblk_e)
    blk_first = blk_end - nblk_e
    start = jnp.cumsum(counts) - counts
    order = jnp.argsort(flat_e).astype(jnp.int32)
    n_blocks = -(-a // bm) + N_EXPERTS
    blk = jnp.arange(n_blocks + 1, dtype=jnp.int32)
    blk_e = jnp.minimum(jnp.sum(blk_end[None, :] <= blk[:, None], axis=1, dtype=jnp.int32), N_EXPERTS - 1)
    within = (blk - blk_first[blk_e]) * bm
    n_used = blk_end[-1].reshape(1)
    blk_start = jnp.clip(start[blk_e] + within, 0, a - 1)
    blk_cnt = jnp.where(blk < n_used[0], jnp.clip(counts[blk_e] - within, 0, bm), 0)
    return order, blk_e, blk_start, blk_cnt, n_used


def _moe(layer, h2t, route, wg, wu, wd):
    n_tok = route.shape[0]
    eid = route[:, 2:4].astype(jnp.int32)
    order, blk_e, blk_start, blk_cnt, n_used = _dispatch_tables(eid, n_tok)
    return _expert_ffn(layer, blk_e, n_used, blk_start, blk_cnt, order, h2t, wg, wu, wd, n_tok)


def kernel(x, c, ctx, c_ctx, w_ada, b_ada, norm1_g, norm2_g, w_in, s5_lam_re, s5_lam_im, s5_log_dt, s5_b_re, s5_b_im, s5_c_re, s5_c_im, s5_d, s5_w_glu, hgrn_lb_raw, w_branch_s5, w_branch_hgrn, w_branch_ret, w_out, moe_w_group, moe_b_group, moe_w_expert, moe_b_expert, moe_w_gate, moe_w_up, moe_w_down, final_norm_g):
    bsz, n_lat, d = x.shape
    n_ctx = ctx.shape[1]
    t_len = n_ctx + n_lat
    depth = w_ada.shape[0]
    m = bsz * t_len
    w_s5 = s5_d.shape[1]
    w_hg = w_branch_hgrn.shape[1]
    w_ret = w_branch_ret.shape[1]
    n_heads = w_ret // HEAD_DIM

    pad_rows = (-(bsz + 1)) % SUBLANES
    cc = jnp.concatenate([c, c_ctx[None, :], jnp.zeros((pad_rows, d), F32)], axis=0)
    mods = _ada_tables(cc, w_ada, b_ada).reshape(depth, bsz + 1 + pad_rows, N_MOD, d)

    p_lb = jax.nn.softmax(hgrn_lb_raw.astype(F32), axis=0)
    lower_bounds = jnp.cumsum(p_lb, axis=0) - p_lb[0:1]
    ret_consts = _ret_constants(n_heads, t_len, n_ctx)
    ones_bd = jnp.kron(jnp.eye(w_hg // HEAD_DIM, dtype=F32), jnp.ones((HEAD_DIM, HEAD_DIM), F32)).astype(BF16)

    xs = jnp.concatenate([ctx, x], axis=1).reshape(m, d)
    off_hg = w_s5
    off_ret = off_hg + 5 * w_hg
    off_gz = off_ret + 4 * w_ret
    for l in range(depth):
        mod_l = mods[l, :bsz]
        mod_c = mods[l, bsz:bsz + 1]
        wl = w_in[l].astype(BF16)
        w_parts = (wl[:, :off_hg], wl[:, off_hg:off_ret], wl[:, off_ret:off_gz], wl[:, off_gz:])
        u, hg, ret, gz = _project(xs, norm1_g[l], mod_l, mod_c, w_parts, t_len, n_ctx)

        s5w = _s5_weights(s5_lam_re[l], s5_lam_im[l], s5_log_dt[l], s5_b_re[l], s5_b_im[l],
                          s5_c_re[l], s5_c_im[l], s5_d[l])
        ys5 = _s5_mix(u.reshape(bsz, t_len, w_s5), *s5w, n_ctx).reshape(m, w_s5)
        hg_of, hg_ob = _hg_mix(hg.reshape(bsz, t_len, 5 * w_hg), lower_bounds[l], n_ctx)
        ret_of, ret_ob = _ret_mix(ret.reshape(bsz, t_len, 4 * w_ret), ret_consts, n_ctx)

        wr = jnp.concatenate([moe_w_group[l], moe_w_expert[l],
                              jnp.zeros((d, LANES - N_GROUPS - N_EXPERTS), F32)], axis=1)
        br = jnp.concatenate([moe_b_group[l], moe_b_expert[l],
                              jnp.zeros((LANES - N_GROUPS - N_EXPERTS,), F32)])[None, :]
        wr_hi = wr.astype(BF16)
        wr = jnp.concatenate([wr_hi, (wr - wr_hi.astype(F32)).astype(BF16)], axis=1)
        wts = (s5_w_glu[l].astype(BF16), w_branch_s5[l].astype(BF16), w_branch_hgrn[l].astype(BF16),
               w_branch_ret[l].astype(BF16), w_out[l].astype(BF16), ones_bd, wr, br)
        xs, h2, route = _merge(xs, ys5, hg_of.reshape(m, w_hg), hg_ob.reshape(m, w_hg), hg,
                               ret_of.reshape(m, w_ret), ret_ob.reshape(m, w_ret), ret, gz,
                               mod_l, mod_c, norm2_g[l].reshape(1, d), wts, t_len, n_ctx)
        y2 = _moe(jnp.full((1,), l, jnp.int32), h2, route, moe_w_gate, moe_w_up, moe_w_down)
        xs = _combine(xs, y2, route, mod_l, mod_c, t_len, n_ctx)
    return _final_norm(xs.reshape(bsz, t_len, d), final_norm_g, n_ctx)
```

```python
import functools

import jax
import jax.numpy as jnp
import numpy as np
from jax import lax
from jax.experimental import pallas as pl
from jax.experimental.pallas import tpu as pltpu

F32 = jnp.float32
BF16 = jnp.bfloat16
HIGHEST = lax.Precision.HIGHEST

LANES = 128
SUBLANES = 8
VMEM_LIMIT = 56 * 1024 * 1024

EPS = 1e-6
N_MOD = 6
GRID_W = 64
HEAD_DIM = 64
PAIR = 2 * HEAD_DIM
S5_GROUP = 16
S5_STATE = 64
S5_CHUNK = 16
HG_CHUNK = 64
HG_HALF = HG_CHUNK // 2
HG_BLOCK = 256
RET_BLOCK = 256
ROPE_BASE = 10000.0
N_GROUPS = 4
EXPERTS_PER_GROUP = 8
N_EXPERTS = N_GROUPS * EXPERTS_PER_GROUP
TOP_K = 2
MOE_BLOCK = 512
ROW_GROUP = 8


def _cparams(*sem):
    return pltpu.CompilerParams(dimension_semantics=sem, vmem_limit_bytes=VMEM_LIMIT)


def _row_tile(t_len):
    for tm in (544, 512, 384, 272, 256, 128, 64, 32, 16, 8):
        if t_len % tm == 0:
            return tm
    raise ValueError(f"unsupported stream length {t_len}")


def _ada_kernel(c_ref, w_ref, b_ref, o_ref):
    a = jax.nn.silu(c_ref[...]).astype(BF16)
    o_ref[0] = jnp.dot(a, w_ref[0].astype(BF16), preferred_element_type=F32) + b_ref[0]


def _ada_tables(cc, w_ada, b_ada):
    depth, d, n = w_ada.shape
    rows = cc.shape[0]
    tn = 1536 if n % 1536 == 0 else n
    return pl.pallas_call(
        _ada_kernel,
        grid=(depth, n // tn),
        in_specs=[
            pl.BlockSpec((rows, d), lambda l, j: (0, 0)),
            pl.BlockSpec((1, d, tn), lambda l, j: (l, 0, j)),
            pl.BlockSpec((1, 1, tn), lambda l, j: (l, 0, j)),
        ],
        out_specs=pl.BlockSpec((1, rows, tn), lambda l, j: (l, 0, j)),
        out_shape=jax.ShapeDtypeStruct((depth, rows, n), F32),
        compiler_params=_cparams("parallel", "parallel"),
    )(cc, w_ada, b_ada.reshape(depth, 1, n))


def _mod_rows(ml_ref, mc_ref, idx, is_ctx):
    return jnp.where(is_ctx, mc_ref[0, idx:idx + 1, :], ml_ref[0, idx:idx + 1, :])


def _is_ctx_rows(tm, tiles_per_b, n_ctx):
    jt = pl.program_id(0) % tiles_per_b
    row = jt * tm + lax.broadcasted_iota(jnp.int32, (tm, 1), 0)
    return row < n_ctx


def _store_token_tiles(ref, val):
    n_rows = val.shape[0]
    for k in range(val.shape[1] // LANES):
        ref[pl.ds(k, n_rows, stride=SUBLANES), :] = val[:, LANES * k:LANES * (k + 1)]


def _load_token_tiles(ref):
    n_rows = ref.shape[0] // SUBLANES
    return jnp.concatenate([ref[pl.ds(k, n_rows, stride=SUBLANES), :] for k in range(SUBLANES)], axis=1)


def _rmsnorm_rows(x, g):
    return x * lax.rsqrt(jnp.mean(x * x, axis=-1, keepdims=True) + EPS) * g


def _proj_kernel(x_ref, g_ref, ml_ref, mc_ref, w_ref, *out_refs, tm, tiles_per_b, n_ctx, offsets):
    is_ctx = _is_ctx_rows(tm, tiles_per_b, n_ctx)
    y = _rmsnorm_rows(x_ref[...], g_ref[...])
    y = y * (1.0 + _mod_rows(ml_ref, mc_ref, 1, is_ctx)) + _mod_rows(ml_ref, mc_ref, 0, is_ctx)
    h = y.astype(BF16)
    for k, o_ref in enumerate(out_refs):
        w = w_ref[0, :, offsets[k]:offsets[k + 1]]
        o_ref[...] = jnp.dot(h, w, preferred_element_type=F32).astype(o_ref.dtype)


def _project(x2, g, mod_l, mod_c, w_all, layer, offsets, t_len, n_ctx):
    m, d = x2.shape
    tm = _row_tile(t_len)
    tpb = t_len // tm
    kern = functools.partial(_proj_kernel, tm=tm, tiles_per_b=tpb, n_ctx=n_ctx, offsets=offsets)
    widths = [offsets[k + 1] - offsets[k] for k in range(len(offsets) - 1)]
    out_dtypes = (F32, F32, F32, BF16)
    return pl.pallas_call(
        kern,
        grid=(m // tm,),
        in_specs=[
            pl.BlockSpec((tm, d), lambda i: (i, 0)),
            pl.BlockSpec((1, d), lambda i: (0, 0)),
            pl.BlockSpec((1, N_MOD, d), lambda i: (i // tpb, 0, 0)),
            pl.BlockSpec((1, N_MOD, d), lambda i: (0, 0, 0)),
            pl.BlockSpec((1,) + w_all.shape[1:], lambda i: (layer, 0, 0), pipeline_mode=pl.Buffered(1)),
        ],
        out_specs=[pl.BlockSpec((tm, w), lambda i: (i, 0)) for w in widths],
        out_shape=[jax.ShapeDtypeStruct((m, w), dt) for w, dt in zip(widths, out_dtypes)],
        compiler_params=_cparams("parallel"),
    )(x2, g.reshape(1, d), mod_l, mod_c, w_all)


def _s5_weights(lam_re, lam_im, log_dt, b_re, b_im, c_re, c_im, d_skip):
    n_dir, g_n, p_n = lam_re.shape
    h_n = b_re.shape[-1]
    cn = S5_CHUNK
    lr, li = lam_re.astype(F32), lam_im.astype(F32)
    dt = jnp.exp(log_dt.astype(F32))[..., None]
    mag = jnp.exp(lr * dt)
    ar = mag * jnp.cos(li * dt)
    ai = mag * jnp.sin(li * dt)
    den = lr * lr + li * li
    zr = ((ar - 1.0) * lr + ai * li) / den
    zi = (ai * lr - (ar - 1.0) * li) / den
    bbr = zr[..., None] * b_re - zi[..., None] * b_im
    bbi = zr[..., None] * b_im + zi[..., None] * b_re
    j = jnp.arange(cn + 1, dtype=F32)[:, None, None, None]
    pmag = jnp.exp(lr[None] * dt[None] * j)
    pang = li[None] * dt[None] * j
    pr = pmag * jnp.cos(pang)
    pi = pmag * jnp.sin(pang)
    wr = pr[..., None] * bbr[None] - pi[..., None] * bbi[None]
    wi = pr[..., None] * bbi[None] + pi[..., None] * bbr[None]
    kj = (jnp.einsum('dghp,jdgpk->jdghk', c_re, wr, precision=HIGHEST)
          - jnp.einsum('dghp,jdgpk->jdghk', c_im, wi, precision=HIGHEST))
    s_idx = jnp.arange(cn)[:, None]
    t_idx = jnp.arange(cn)[None, :]
    lag = t_idx - s_idx
    kf = kj[:cn, 0][jnp.clip(lag, 0, cn - 1)]
    kb = kj[:cn, 1][jnp.clip(-lag, 0, cn - 1)]
    kf = jnp.where((lag >= 0)[:, :, None, None, None], kf, 0.0)
    kb = jnp.where((lag <= 0)[:, :, None, None, None], kb, 0.0)
    dsk = d_skip.astype(F32).reshape(g_n, h_n)
    eye_h = jnp.eye(h_n, dtype=F32)
    diag = (lag == 0)[:, :, None, None, None] * (dsk[None, None, :, :, None] * eye_h[None, None, None])
    m_full = kf + kb + diag
    m_mat = m_full.transpose(2, 0, 4, 1, 3).reshape(g_n, cn * h_n, cn * h_n)
    wf_r = wr[:cn, 0][::-1]
    wf_i = wi[:cn, 0][::-1]
    wb_r = wr[:cn, 1]
    wb_i = wi[:cn, 1]
    w_in = jnp.stack([wf_r, wf_i, wb_r, wb_i], axis=0)
    w_in = w_in.transpose(2, 1, 4, 0, 3).reshape(g_n, cn * h_n, 4 * p_n)
    def out_map(d, powers):
        prd, pid = pr[powers, d], pi[powers, d]
        w_re = c_re[d][None] * prd[:, :, None, :] - c_im[d][None] * pid[:, :, None, :]
        w_im = c_re[d][None] * pid[:, :, None, :] + c_im[d][None] * prd[:, :, None, :]
        return w_re, -w_im
    t_arr = jnp.arange(cn)
    of_r, of_i = out_map(0, t_arr + 1)
    ob_r, ob_i = out_map(1, cn - t_arr)
    w_out = jnp.stack([of_r, of_i, ob_r, ob_i], axis=0)
    w_out = w_out.transpose(2, 0, 4, 1, 3).reshape(g_n, 4 * p_n, cn * h_n)
    slot = jax.nn.one_hot(jnp.arange(g_n) % 2, 2, dtype=F32)
    w_in = w_in.reshape(g_n, cn * h_n, 4, 1, p_n) * slot[:, None, None, :, None]
    w_in = w_in.reshape(g_n, cn * h_n, 8 * p_n)
    w_out = w_out.reshape(g_n, 4, 1, p_n, cn * h_n) * slot[:, None, :, None, None]
    w_out = w_out.reshape(g_n, 8 * p_n, cn * h_n)
    a_mat = jnp.stack([pr[cn, 0].reshape(-1), pi[cn, 0].reshape(-1),
                       pr[cn, 1].reshape(-1), pi[cn, 1].reshape(-1)], axis=0)
    w1 = jnp.concatenate([m_mat, w_in], axis=-1).astype(BF16)
    return w1, w_out.astype(BF16), a_mat


def _s5_fold_perm():
    j, g8, h = np.meshgrid(np.arange(8), np.arange(8), np.arange(S5_GROUP), indexing='ij')
    src = (j * 8 + g8) * S5_GROUP + h
    dst = (g8 * 8 + j) * S5_GROUP + h
    perm = np.zeros((8 * LANES, 8 * LANES), np.float32)
    perm[src.reshape(-1), dst.reshape(-1)] = 1.0
    return jnp.asarray(perm, dtype=BF16)


def _s5_kernel(u_ref, perm_ref, w1_ref, wo_ref, a_ref, y_ref, ug_scr, yg_scr, xfr, xfi, xbr, xbi,
               *, n_chunks, n_ctx_chunks, n_groups):
    cn = S5_CHUNK
    blk = cn * S5_GROUP
    tok_per_slab = LANES // S5_GROUP
    for c in range(cn // tok_per_slab):
        slabs = [u_ref[0, pl.ds(tok_per_slab * c + j, n_chunks, stride=cn), :] for j in range(tok_per_slab)]
        s = jnp.concatenate(slabs, axis=1).astype(BF16)
        up = jnp.dot(s, perm_ref[...], preferred_element_type=F32).astype(BF16)
        for g in range(n_groups):
            ug_scr[g, :, LANES * c:LANES * (c + 1)] = up[:, LANES * g:LANES * (g + 1)]
    xs = (xfr, xfi, xbr, xbi)
    for g in range(n_groups):
        z = jnp.dot(ug_scr[g], w1_ref[g], preferred_element_type=F32)
        yg_scr[g] = z[:, :blk]
        lanes = slice(LANES * (g // 2), LANES * (g // 2 + 1))
        for k, xr in enumerate(xs):
            piece = z[:, blk + LANES * k:blk + LANES * (k + 1)]
            if g % 2 == 0:
                xr[:, lanes] = piece
            else:
                xr[:, lanes] += piece
    arf, aif, arb, aib = a_ref[0:1, :], a_ref[1:2, :], a_ref[2:3, :], a_ref[3:4, :]

    def body(i, carry):
        fr, fi, br, bi = carry
        nb = jnp.where(i < n_ctx_chunks, n_ctx_chunks - 1 - i, n_chunks - 1 - (i - n_ctx_chunks))
        x_fr, x_fi = xfr[pl.ds(i, 1), :], xfi[pl.ds(i, 1), :]
        x_br, x_bi = xbr[pl.ds(nb, 1), :], xbi[pl.ds(nb, 1), :]
        xfr[pl.ds(i, 1), :] = fr
        xfi[pl.ds(i, 1), :] = fi
        xbr[pl.ds(nb, 1), :] = br
        xbi[pl.ds(nb, 1), :] = bi
        return (arf * fr - aif * fi + x_fr, arf * fi + aif * fr + x_fi,
                arb * br - aib * bi + x_br, arb * bi + aib * br + x_bi)

    zero = jnp.zeros((1, xfr.shape[1]), F32)
    lax.fori_loop(0, n_chunks, body, (zero, zero, zero, zero))
    for g in range(n_groups):
        lanes = slice(LANES * (g // 2), LANES * (g // 2 + 1))
        hp = jnp.concatenate([xr[:, lanes] for xr in xs], axis=1).astype(BF16)
        yg_scr[g] += jnp.dot(hp, wo_ref[g], preferred_element_type=F32)
    for c in range(cn // tok_per_slab):
        ycat = jnp.concatenate([yg_scr[g, :, LANES * c:LANES * (c + 1)] for g in range(n_groups)], axis=1)
        hi = ycat.astype(BF16)
        lo = (ycat - hi.astype(F32)).astype(BF16)
        r = _dot_nt(hi, perm_ref[...]) + _dot_nt(lo, perm_ref[...])
        for j in range(tok_per_slab):
            y_ref[0, pl.ds(tok_per_slab * c + j, n_chunks, stride=cn), :] = r[:, LANES * j:LANES * (j + 1)]


def _s5_mix(u, w1, w_out, a_mat, n_ctx):
    bsz, t_len, width = u.shape
    g_n = width // S5_GROUP
    cn = S5_CHUNK
    n_chunks = t_len // cn
    blk = cn * S5_GROUP
    perm = _s5_fold_perm()
    gs = LANES // S5_GROUP
    state_w = gs * S5_STATE
    kern = functools.partial(_s5_kernel, n_chunks=n_chunks, n_ctx_chunks=n_ctx // cn, n_groups=gs)
    return pl.pallas_call(
        kern,
        grid=(bsz, g_n // gs),
        in_specs=[pl.BlockSpec((1, t_len, LANES), lambda b, s: (b, 0, s)),
                  pl.BlockSpec(perm.shape, lambda b, s: (0, 0), pipeline_mode=pl.Buffered(1)),
                  pl.BlockSpec((gs,) + w1.shape[1:], lambda b, s: (s, 0, 0)),
                  pl.BlockSpec((gs,) + w_out.shape[1:], lambda b, s: (s, 0, 0)),
                  pl.BlockSpec((4, state_w), lambda b, s: (0, s))],
        out_specs=pl.BlockSpec((1, t_len, LANES), lambda b, s: (b, 0, s)),
        out_shape=jax.ShapeDtypeStruct((bsz, t_len, width), F32),
        scratch_shapes=[pltpu.VMEM((gs, n_chunks, blk), BF16), pltpu.VMEM((gs, n_chunks, blk), F32)]
        + [pltpu.VMEM((n_chunks, state_w), F32)] * 4,
        compiler_params=_cparams("parallel", "parallel"),
    )(u, perm, w1, w_out, a_mat)


def _head0_lanes(shape):
    return lax.broadcasted_iota(jnp.int32, shape, len(shape) - 1) < HEAD_DIM


def _stack_heads(q):
    m0 = _head0_lanes(q.shape)
    return jnp.concatenate([jnp.where(m0, q, 0.0), jnp.where(m0, 0.0, q)], axis=-2)


def _unstack_heads(o2):
    c = o2.shape[-2] // 2
    top, bottom = o2[..., :c, :], o2[..., c:, :]
    return jnp.where(_head0_lanes(top.shape), top, bottom)


def _pair_block_mask():
    r = lax.broadcasted_iota(jnp.int32, (PAIR, PAIR), 0) < HEAD_DIM
    c = lax.broadcasted_iota(jnp.int32, (PAIR, PAIR), 1) < HEAD_DIM
    return r == c


def _dot_nt(a, b):
    return lax.dot_general(a, b, (((1,), (1,)), ((), ())), preferred_element_type=F32)


def _dot_tn(a, b):
    return lax.dot_general(a, b, (((0,), (0,)), ((), ())), preferred_element_type=F32)


def _bwd_block(j, n_ctx_blocks, n_blocks):
    return jnp.where(j < n_ctx_blocks, n_ctx_blocks - 1 - j, n_blocks - 1 - (j - n_ctx_blocks))


def _bdot_nt(a, b):
    return lax.dot_general(a, b, (((2,), (2,)), ((0,), (0,))), preferred_element_type=F32)


def _bdot_nn(a, b):
    return lax.dot_general(a, b, (((2,), (1,)), ((0,), (0,))), preferred_element_type=F32)


def _hg_block(q, z, v, lb, ht, reverse):
    rows = q.shape[0]
    c = HG_CHUNK
    hh = HG_HALF
    n = rows // c
    f = lb + (1.0 - lb) * jax.nn.sigmoid(z)
    logf = jnp.log(f)
    kk = 1.0 - f
    ri = lax.broadcasted_iota(jnp.int32, (n, c, c), 1)
    ci = lax.broadcasted_iota(jnp.int32, (n, c, c), 2)
    tri = jnp.where((ci >= ri) if reverse else (ci <= ri), 1.0, 0.0).astype(BF16)
    l1 = logf.astype(BF16)
    r1 = logf - l1.astype(F32)
    l2 = r1.astype(BF16)
    l3 = (r1 - l2.astype(F32)).astype(BF16)
    b3 = _bdot_nn(tri, jnp.concatenate([l1, l2, l3], axis=1).reshape(n, c, 3 * PAIR))
    b4 = b3[..., 0:PAIR] + b3[..., PAIR:2 * PAIR] + b3[..., 2 * PAIR:3 * PAIR]
    q4, k4 = q.reshape(n, c, PAIR), kk.reshape(n, c, PAIR)
    vb = v.astype(BF16).reshape(n, c, PAIR)
    if reverse:
        first, second = slice(hh, c), slice(0, hh)
        r = b4[:, hh:hh + 1]
        bend = b4[:, 0:1]
    else:
        first, second = slice(0, hh), slice(hh, c)
        r = b4[:, hh - 1:hh]
        bend = b4[:, c - 1:c]
    b1, b2 = b4[:, first], b4[:, second]
    qd1 = q4[:, first] * jnp.exp(b1)
    qd2 = q4[:, second] * jnp.exp(b2 - r)
    k1d = k4[:, first] * jnp.exp(-b1)
    k2 = k4 * jnp.exp(r - b4)
    s1 = _bdot_nt(_stack_heads(qd1).astype(BF16), k1d.astype(BF16))
    s2 = _bdot_nt(_stack_heads(qd2).astype(BF16), k2.astype(BF16))
    t1 = lax.broadcasted_iota(jnp.int32, (n, 2 * hh, hh), 1) % hh
    c1 = lax.broadcasted_iota(jnp.int32, (n, 2 * hh, hh), 2)
    t2 = lax.broadcasted_iota(jnp.int32, (n, 2 * hh, c), 1) % hh
    c2 = lax.broadcasted_iota(jnp.int32, (n, 2 * hh, c), 2)
    if reverse:
        s1 = jnp.where(c1 >= t1, s1, 0.0)
        s2 = jnp.where(c2 >= t2, s2, 0.0)
    else:
        s1 = jnp.where(c1 <= t1, s1, 0.0)
        s2 = jnp.where(c2 <= t2 + hh, s2, 0.0)
    o1 = _unstack_heads(_bdot_nn(s1.astype(BF16), vb[:, first]))
    o2 = _unstack_heads(_bdot_nn(s2.astype(BF16), vb))
    o = jnp.concatenate([o2, o1] if reverse else [o1, o2], axis=1)
    ke = (k4 * jnp.exp(bend - b4)).astype(BF16)
    dec = jnp.exp(bend)
    bmask = _pair_block_mask()
    states = [None] * n
    for ch in (range(n - 1, -1, -1) if reverse else range(n)):
        states[ch] = ht.astype(BF16)
        ht = ht * dec[ch] + jnp.where(bmask, _dot_tn(vb[ch], ke[ch]), 0.0)
    o = o + _bdot_nt((q4 * jnp.exp(b4)).astype(BF16), jnp.stack(states, axis=0))
    return o.reshape(rows, PAIR), ht


def _hg_kernel(qf_ref, zf_ref, vf_ref, qb_ref, zb_ref, vb_ref, lb_ref, of_ref, ob_ref, hf_scr, hb_scr, *, n_pairs):
    @pl.when(pl.program_id(1) == 0)
    def _():
        hf_scr[...] = jnp.zeros_like(hf_scr)
        hb_scr[...] = jnp.zeros_like(hb_scr)

    for p in range(n_pairs):
        sl = slice(p * PAIR, (p + 1) * PAIR)
        o, hn = _hg_block(qf_ref[0, :, sl], zf_ref[0, :, sl], vf_ref[0, :, sl], lb_ref[0:1, sl], hf_scr[p], False)
        of_ref[0, :, sl] = o
        hf_scr[p] = hn
        o, hn = _hg_block(qb_ref[0, :, sl], zb_ref[0, :, sl], vb_ref[0, :, sl], lb_ref[1:2, sl], hb_scr[p], True)
        ob_ref[0, :, sl] = o
        hb_scr[p] = hn


def _hg_mix(hg, lb, n_ctx):
    bsz, t_len, total = hg.shape
    w = total // 5
    n_pairs = w // PAIR
    c = HG_BLOCK
    nc, ncc = t_len // c, n_ctx // c
    bw = functools.partial(_bwd_block, n_ctx_blocks=ncc, n_blocks=nc)

    def fspec(col):
        return pl.BlockSpec((1, c, w), lambda b, j: (b, j, col))

    def bspec(col):
        return pl.BlockSpec((1, c, w), lambda b, j: (b, bw(j), col))

    kern = functools.partial(_hg_kernel, n_pairs=n_pairs)
    return pl.pallas_call(
        kern,
        grid=(bsz, nc),
        in_specs=[fspec(0), fspec(1), fspec(3), bspec(0), bspec(2), bspec(3),
                  pl.BlockSpec((2, w), lambda b, j: (0, 0))],
        out_specs=[pl.BlockSpec((1, c, w), lambda b, j: (b, j, 0)),
                   pl.BlockSpec((1, c, w), lambda b, j: (b, bw(j), 0))],
        out_shape=[jax.ShapeDtypeStruct((bsz, t_len, w), F32)] * 2,
        scratch_shapes=[pltpu.VMEM((n_pairs, PAIR, PAIR), F32)] * 2,
        compiler_params=_cparams("parallel", "arbitrary"),
    )(hg, hg, hg, hg, hg, hg, lb)


def _rope_rows(x, cos, sin):
    hd2 = HEAD_DIM // 2
    lane = lax.broadcasted_iota(jnp.int32, x.shape, 1) % HEAD_DIM
    swapped = jnp.where(lane < hd2, pltpu.roll(x, PAIR - hd2, 1), pltpu.roll(x, hd2, 1))
    return x * cos + swapped * sin


def _ret_kernel(qf_ref, kf_ref, vf_ref, cf_ref, sf_ref, qb_ref, kb_ref, vb_ref, cb_ref, sb_ref,
                d_ref, gqf_ref, gkf_ref, gqb_ref, gkb_ref, gh_ref, of_ref, ob_ref, hf_scr, hb_scr, *, n_pairs):
    @pl.when(pl.program_id(1) == 0)
    def _():
        hf_scr[...] = jnp.zeros_like(hf_scr)
        hb_scr[...] = jnp.zeros_like(hb_scr)

    kscale = HEAD_DIM ** -0.5
    bmask = _pair_block_mask()
    for p in range(n_pairs):
        sl = slice(p * PAIR, (p + 1) * PAIR)
        q = _rope_rows(qf_ref[0, :, sl], cf_ref[...], sf_ref[...])
        k = _rope_rows(kf_ref[0, :, sl] * kscale, cf_ref[...], sf_ref[...])
        vb16 = vf_ref[0, :, sl].astype(BF16)
        s = _dot_nt(_stack_heads(q).astype(BF16), k.astype(BF16)) * d_ref[p]
        o = _unstack_heads(jnp.dot(s.astype(BF16), vb16, preferred_element_type=F32))
        ht = hf_scr[p]
        o = o + _dot_nt((q * gqf_ref[:, sl]).astype(BF16), ht.astype(BF16))
        upd = _dot_tn(vb16, (k * gkf_ref[:, sl]).astype(BF16))
        hf_scr[p] = ht * gh_ref[p] + jnp.where(bmask, upd, 0.0)
        of_ref[0, :, sl] = o

        q = _rope_rows(qb_ref[0, :, sl], cb_ref[...], sb_ref[...])
        k = _rope_rows(kb_ref[0, :, sl] * kscale, cb_ref[...], sb_ref[...])
        vb16 = vb_ref[0, :, sl].astype(BF16)
        ht = hb_scr[p]
        ob_ref[0, :, sl] = _dot_nt((q * gqb_ref[:, sl]).astype(BF16), ht.astype(BF16))
        upd = _dot_tn(vb16, (k * gkb_ref[:, sl]).astype(BF16))
        hb_scr[p] = ht * gh_ref[p] + jnp.where(bmask, upd, 0.0)


def _ret_constants(n_heads, t_len, n_ctx):
    c = RET_BLOCK
    log_gamma = jnp.log(1.0 - 2.0 ** (-5.0 - jnp.arange(n_heads, dtype=F32)))
    t = jnp.arange(c, dtype=F32)
    dist = jnp.abs(t[:, None] - t[None, :])
    dmat = jnp.exp(log_gamma[:, None, None] * dist[None])
    dmat = dmat.reshape(n_heads // 2, 2 * c, c)
    lg_lane = jnp.repeat(log_gamma, HEAD_DIM)[None, :]
    gqf = jnp.exp(lg_lane * (t[:, None] + 1.0))
    gkf = jnp.exp(lg_lane * (c - 1.0 - t[:, None]))
    gqb = jnp.exp(lg_lane * (c - t[:, None]))
    gkb = jnp.exp(lg_lane * t[:, None])
    gh = jnp.exp(lg_lane * float(c)).reshape(n_heads // 2, 1, PAIR)
    gh = jnp.broadcast_to(gh, (n_heads // 2, PAIR, PAIR))
    n_lat = t_len - n_ctx
    rows = n_lat // GRID_W
    row = jnp.repeat(jnp.arange(rows, dtype=F32), GRID_W)
    col = jnp.broadcast_to(jnp.arange(GRID_W, dtype=F32)[None, :], (rows, GRID_W)).reshape(-1)
    n_freq = HEAD_DIM // 4
    inv_freq = ROPE_BASE ** (-jnp.arange(n_freq, dtype=F32) / n_freq)
    ang = jnp.concatenate([row[:, None] * inv_freq, col[:, None] * inv_freq], axis=-1)
    cos_l, sin_l = jnp.cos(ang), jnp.sin(ang)
    cos_h = jnp.concatenate([cos_l, cos_l], axis=-1)
    sin_h = jnp.concatenate([-sin_l, sin_l], axis=-1)
    cos_t = jnp.concatenate([jnp.ones((n_ctx, HEAD_DIM), F32), cos_h], axis=0)
    sin_t = jnp.concatenate([jnp.zeros((n_ctx, HEAD_DIM), F32), sin_h], axis=0)
    cos_t = jnp.concatenate([cos_t, cos_t], axis=-1)
    sin_t = jnp.concatenate([sin_t, sin_t], axis=-1)
    return dmat, gqf, gkf, gqb, gkb, gh, cos_t, sin_t


def _ret_mix(ret, consts, n_ctx):
    dmat, gqf, gkf, gqb, gkb, gh, cos_t, sin_t = consts
    bsz, t_len, total = ret.shape
    w = total // 4
    n_pairs = w // PAIR
    c = RET_BLOCK
    nb, ncb = t_len // c, n_ctx // c
    bw = functools.partial(_bwd_block, n_ctx_blocks=ncb, n_blocks=nb)

    def fspec(col):
        return pl.BlockSpec((1, c, w), lambda b, j: (b, j, col))

    def bspec(col):
        return pl.BlockSpec((1, c, w), lambda b, j: (b, bw(j), col))

    def const(a):
        nd = a.ndim
        return pl.BlockSpec(a.shape, lambda b, j: (0,) * nd)

    rope_f = pl.BlockSpec((c, PAIR), lambda b, j: (j, 0))
    rope_b = pl.BlockSpec((c, PAIR), lambda b, j: (bw(j), 0))
    kern = functools.partial(_ret_kernel, n_pairs=n_pairs)
    return pl.pallas_call(
        kern,
        grid=(bsz, nb),
        in_specs=[fspec(0), fspec(1), fspec(2), rope_f, rope_f, bspec(0), bspec(1), bspec(2), rope_b, rope_b,
                  const(dmat), const(gqf), const(gkf), const(gqb), const(gkb), const(gh)],
        out_specs=[pl.BlockSpec((1, c, w), lambda b, j: (b, j, 0)),
                   pl.BlockSpec((1, c, w), lambda b, j: (b, bw(j), 0))],
        out_shape=[jax.ShapeDtypeStruct((bsz, t_len, w), F32)] * 2,
        scratch_shapes=[pltpu.VMEM((n_pairs, PAIR, PAIR), F32)] * 2,
        compiler_params=_cparams("parallel", "arbitrary"),
    )(ret, ret, ret, cos_t, sin_t, ret, ret, ret, cos_t, sin_t, dmat, gqf, gkf, gqb, gkb, gh)


def _head_norm_gate(o, g, ones_bd):
    sq = o * o
    hi = sq.astype(BF16)
    lo = (sq - hi.astype(F32)).astype(BF16)
    ss = jnp.dot(hi, ones_bd, preferred_element_type=F32) + jnp.dot(lo, ones_bd, preferred_element_type=F32)
    return o * lax.rsqrt(ss * (1.0 / HEAD_DIM) + EPS) * jax.nn.silu(g)


def _route(lg):
    lane = lax.broadcasted_iota(jnp.int32, lg.shape, 1).astype(F32)
    big = float(1 << 20)
    neg = -jnp.inf
    gmask = lane < N_GROUPS
    lgm = jnp.where(gmask, lg, neg)
    gmax = jnp.max(lgm, axis=1, keepdims=True)
    gsum = jnp.sum(jnp.where(gmask, jnp.exp(lgm - gmax), 0.0), axis=1, keepdims=True)
    gp = 1.0 / gsum
    gi = jnp.min(jnp.where(gmask & (lgm == gmax), lane, big), axis=1, keepdims=True)
    lo_lane = N_GROUPS + EXPERTS_PER_GROUP * gi
    emask = (lane >= lo_lane) & (lane < lo_lane + EXPERTS_PER_GROUP)
    l1 = jnp.max(jnp.where(emask, lg, neg), axis=1, keepdims=True)
    i1 = jnp.min(jnp.where(emask & (lg == l1), lane, big), axis=1, keepdims=True)
    mask2 = emask & (lane != i1)
    l2 = jnp.max(jnp.where(mask2, lg, neg), axis=1, keepdims=True)
    i2 = jnp.min(jnp.where(mask2 & (lg == l2), lane, big), axis=1, keepdims=True)
    e21 = jnp.exp(l2 - l1)
    w1 = gp / (1.0 + e21)
    w2 = gp * e21 / (1.0 + e21)
    e1 = i1 - N_GROUPS
    e2 = i2 - N_GROUPS
    return jnp.where(lane == 0, w1, jnp.where(lane == 1, w2, jnp.where(lane == 2, e1, jnp.where(lane == 3, e2, 0.0))))


def _merge_kernel(x_ref, ys_ref, hof_ref, hob_ref, hgg_ref, rof_ref, rob_ref, rgg_ref, gz_ref,
                  ml_ref, mc_ref, g2_ref, wglu_ref, wbs_ref, wbh_ref, wbr_ref, wout_ref, ones_ref,
                  wr_ref, br_ref, xo_ref, h2_ref, rt_ref, *, tm, tiles_per_b, n_ctx):
    d = x_ref.shape[1]
    is_ctx = _is_ctx_rows(tm, tiles_per_b, n_ctx)
    y = jax.nn.gelu(ys_ref[...], approximate=True)
    y = y * jax.nn.sigmoid(jnp.dot(y.astype(BF16), wglu_ref[...], preferred_element_type=F32))
    t_s5 = jnp.dot(y.astype(BF16), wbs_ref[...], preferred_element_type=F32)
    yh = _head_norm_gate(hof_ref[...] + hob_ref[...], hgg_ref[...], ones_ref[...])
    t_hg = jnp.dot(yh.astype(BF16), wbh_ref[...], preferred_element_type=F32)
    yr = _head_norm_gate(rof_ref[...] + rob_ref[...], rgg_ref[...], ones_ref[...])
    t_ret = jnp.dot(yr.astype(BF16), wbr_ref[...], preferred_element_type=F32)
    merged = (jax.nn.sigmoid(gz_ref[:, 0:d]) * t_s5 + jax.nn.sigmoid(gz_ref[:, d:2 * d]) * t_hg
              + jax.nn.sigmoid(gz_ref[:, 2 * d:3 * d]) * t_ret)
    mix = jnp.dot(merged.astype(BF16), wout_ref[...], preferred_element_type=F32)
    xn = x_ref[...] + _mod_rows(ml_ref, mc_ref, 2, is_ctx) * mix
    xo_ref[...] = xn
    h2 = _rmsnorm_rows(xn, g2_ref[...])
    h2 = h2 * (1.0 + _mod_rows(ml_ref, mc_ref, 4, is_ctx)) + _mod_rows(ml_ref, mc_ref, 3, is_ctx)
    _store_token_tiles(h2_ref, h2)
    h_hi = h2.astype(BF16)
    h_lo = (h2 - h_hi.astype(F32)).astype(BF16)
    p_hi = jnp.dot(h_hi, wr_ref[...], preferred_element_type=F32)
    p_lo = jnp.dot(h_lo, wr_ref[:, 0:LANES], preferred_element_type=F32)
    rt_ref[...] = _route(p_hi[:, 0:LANES] + p_hi[:, LANES:2 * LANES] + p_lo + br_ref[...])


def _merge(x2, ys5, hg_of, hg_ob, hg2, ret_of, ret_ob, ret2, gz, mod_l, mod_c, g2, wts, t_len, n_ctx):
    m, d = x2.shape
    w_s5 = ys5.shape[1]
    w_h = hg_of.shape[1]
    tm = _row_tile(t_len)
    tpb = t_len // tm
    wglu, wbs, wbh, wbr, wout, ones_bd, wr, br = wts

    def rows(width, col=0):
        return pl.BlockSpec((tm, width), lambda i: (i, col))

    def const(a):
        nd = a.ndim
        return pl.BlockSpec(a.shape, lambda i: (0,) * nd)

    kern = functools.partial(_merge_kernel, tm=tm, tiles_per_b=tpb, n_ctx=n_ctx)
    return pl.pallas_call(
        kern,
        grid=(m // tm,),
        in_specs=[rows(d), rows(w_s5), rows(w_h), rows(w_h), rows(w_h, 4), rows(w_h), rows(w_h), rows(w_h, 3),
                  rows(3 * d),
                  pl.BlockSpec((1, N_MOD, d), lambda i: (i // tpb, 0, 0)),
                  pl.BlockSpec((1, N_MOD, d), lambda i: (0, 0, 0)),
                  const(g2), const(wglu), const(wbs), const(wbh), const(wbr), const(wout), const(ones_bd),
                  const(wr), const(br)],
        out_specs=[rows(d), pl.BlockSpec((tm * SUBLANES, LANES), lambda i: (i, 0)), rows(LANES)],
        out_shape=[jax.ShapeDtypeStruct((m, d), F32), jax.ShapeDtypeStruct((m * SUBLANES, LANES), F32),
                   jax.ShapeDtypeStruct((m, LANES), F32)],
        compiler_params=_cparams("parallel"),
    )(x2, ys5, hg_of, hg_ob, hg2, ret_of, ret_ob, ret2, gz, mod_l, mod_c, g2, wglu, wbs, wbh, wbr, wout,
      ones_bd, wr, br)


def _ffn_kernel(lay_ref, be_ref, nu_ref, bs_ref, cnt_ref, src_ref, dst_ref, h2_hbm, wg_ref, wu_ref, wd_ref, y2_hbm,
                wg_scr, wu_scr, wd_scr, xbuf, obuf, gsem, ssem, *, n_spare_rows):
    i = pl.program_id(0)
    nu = nu_ref[0]
    slot = i % 2
    tile = SUBLANES

    def gather_copy(rows_src, rows_dst, sl):
        return pltpu.make_async_copy(h2_hbm.at[rows_src, :], xbuf.at[sl, rows_dst, :], gsem.at[sl])

    def scatter_copy(rows_src, rows_dst, sl):
        return pltpu.make_async_copy(obuf.at[sl, rows_src, :], y2_hbm.at[rows_dst, :], ssem.at[sl])

    def n_groups(blk):
        return lax.shift_right_logical(cnt_ref[blk] + (ROW_GROUP - 1), ROW_GROUP.bit_length() - 1)

    def start_gather(blk, sl):
        base = bs_ref[blk]

        def body(g, carry):
            for j in range(ROW_GROUP):
                r = g * ROW_GROUP + j
                gather_copy(pl.ds(pl.multiple_of(src_ref[base + r], tile), tile),
                            pl.ds(pl.multiple_of(r * tile, tile), tile), sl).start()
            return carry

        lax.fori_loop(0, n_groups(blk), body, 0)

    def wait_gather(blk, sl):
        n = pl.multiple_of(n_groups(blk) * (ROW_GROUP * tile), tile)
        gather_copy(pl.ds(0, n), pl.ds(0, n), sl).wait()

    def start_scatter(blk, sl):
        base = bs_ref[blk]

        def body(g, carry):
            for j in range(ROW_GROUP):
                r = g * ROW_GROUP + j
                scatter_copy(pl.ds(pl.multiple_of(r * tile, tile), tile),
                             pl.ds(pl.multiple_of(dst_ref[base + r], tile), tile), sl).start()
            return carry

        lax.fori_loop(0, n_groups(blk), body, 0)

    def wait_scatter(blk, sl):
        n = pl.multiple_of(n_groups(blk) * (ROW_GROUP * tile), tile)
        scatter_copy(pl.ds(0, n), pl.ds(0, n), sl).wait()

    @pl.when(i == 0)
    def _():
        xbuf[...] = jnp.zeros_like(xbuf)
        first_spare = y2_hbm.shape[0] - n_spare_rows
        clear = scatter_copy(pl.ds(0, n_spare_rows), pl.ds(first_spare, n_spare_rows), 0)
        obuf[0, pl.ds(0, n_spare_rows), :] = jnp.zeros((n_spare_rows, LANES), F32)
        clear.start()
        clear.wait()

    @pl.when((i == 0) & (nu > 0))
    def _():
        start_gather(0, 0)

    @pl.when(i + 1 < nu)
    def _():
        start_gather(i + 1, 1 - slot)

    prev = be_ref[jnp.maximum(i - 1, 0)]

    @pl.when((i < nu) & ((i == 0) | (be_ref[i] != prev)))
    def _():
        wg_scr[...] = wg_ref[0, 0].astype(BF16)
        wu_scr[...] = wu_ref[0, 0].astype(BF16)
        wd_scr[...] = wd_ref[0, 0].astype(BF16)

    @pl.when(i < nu)
    def _():
        wait_gather(i, slot)
        xb = _load_token_tiles(xbuf.at[slot]).astype(BF16)
        hid = (jax.nn.silu(jnp.dot(xb, wg_scr[...], preferred_element_type=F32))
               * jnp.dot(xb, wu_scr[...], preferred_element_type=F32))
        y = jnp.dot(hid.astype(BF16), wd_scr[...], preferred_element_type=F32)
        _store_token_tiles(obuf.at[slot], y)
        start_scatter(i, slot)

    @pl.when((i >= 1) & (i < nu + 1))
    def _():
        wait_scatter(i - 1, 1 - slot)

    @pl.when((i == pl.num_programs(0) - 1) & (i < nu))
    def _():
        wait_scatter(i, slot)


def _expert_ffn(layer, blk_e, n_used, blk_start, blk_cnt, src_rows, dst_rows, h2t, wg, wu, wd, n_tok):
    d, f = wg.shape[2], wg.shape[3]
    bm = MOE_BLOCK
    n_blocks = blk_e.shape[0] - 1
    n_spare = N_EXPERTS * (ROW_GROUP - 1)

    def wspec(shape):
        return pl.BlockSpec((1, 1) + shape, lambda i, lay, be, *_: (lay[0], be[i], 0, 0))

    grid_spec = pltpu.PrefetchScalarGridSpec(
        num_scalar_prefetch=7,
        grid=(n_blocks,),
        in_specs=[pl.BlockSpec(memory_space=pl.ANY), wspec((d, f)), wspec((d, f)), wspec((f, d))],
        out_specs=pl.BlockSpec(memory_space=pl.ANY),
        scratch_shapes=[pltpu.VMEM((d, f), BF16), pltpu.VMEM((d, f), BF16), pltpu.VMEM((f, d), BF16),
                        pltpu.VMEM((2, bm * SUBLANES, LANES), F32), pltpu.VMEM((2, bm * SUBLANES, LANES), F32),
                        pltpu.SemaphoreType.DMA((2,)), pltpu.SemaphoreType.DMA((2,))],
    )
    return pl.pallas_call(
        functools.partial(_ffn_kernel, n_spare_rows=n_spare * SUBLANES),
        grid_spec=grid_spec,
        out_shape=jax.ShapeDtypeStruct(((n_tok * TOP_K + n_spare) * SUBLANES, LANES), F32),
        compiler_params=_cparams("arbitrary"),
    )(layer, blk_e, n_used, blk_start, blk_cnt, src_rows, dst_rows, h2t, wg, wu, wd)


def _combine_kernel(x_ref, y2_ref, rt_ref, ml_ref, mc_ref, o_ref, *, tm, tiles_per_b, n_ctx):
    is_ctx = _is_ctx_rows(tm, tiles_per_b, n_ctx)
    rt = rt_ref[...]
    step = TOP_K * SUBLANES
    y0 = jnp.concatenate([y2_ref[pl.ds(k, tm, stride=step), :] for k in range(SUBLANES)], axis=1)
    y1 = jnp.concatenate([y2_ref[pl.ds(SUBLANES + k, tm, stride=step), :] for k in range(SUBLANES)], axis=1)
    y = rt[:, 0:1] * y0 + rt[:, 1:2] * y1
    o_ref[...] = x_ref[...] + _mod_rows(ml_ref, mc_ref, 5, is_ctx) * y


def _combine(x2, y2, route, mod_l, mod_c, t_len, n_ctx):
    m, d = x2.shape
    tm = _row_tile(t_len)
    tpb = t_len // tm
    kern = functools.partial(_combine_kernel, tm=tm, tiles_per_b=tpb, n_ctx=n_ctx)
    return pl.pallas_call(
        kern,
        grid=(m // tm,),
        in_specs=[pl.BlockSpec((tm, d), lambda i: (i, 0)),
                  pl.BlockSpec((tm * TOP_K * SUBLANES, LANES), lambda i: (i, 0)),
                  pl.BlockSpec((tm, LANES), lambda i: (i, 0)),
                  pl.BlockSpec((1, N_MOD, d), lambda i: (i // tpb, 0, 0)),
                  pl.BlockSpec((1, N_MOD, d), lambda i: (0, 0, 0))],
        out_specs=pl.BlockSpec((tm, d), lambda i: (i, 0)),
        out_shape=jax.ShapeDtypeStruct((m, d), F32),
        compiler_params=_cparams("parallel"),
    )(x2, y2, route, mod_l, mod_c)


def _final_norm_kernel(x_ref, g_ref, o_ref):
    o_ref[0] = _rmsnorm_rows(x_ref[0], g_ref[...])


def _final_norm(x3, g, n_ctx):
    bsz, t_len, d = x3.shape
    n_lat = t_len - n_ctx
    tm = n_ctx
    off = n_ctx // tm
    return pl.pallas_call(
        _final_norm_kernel,
        grid=(bsz, n_lat // tm),
        in_specs=[pl.BlockSpec((1, tm, d), lambda b, j: (b, j + off, 0)),
                  pl.BlockSpec((1, d), lambda b, j: (0, 0))],
        out_specs=pl.BlockSpec((1, tm, d), lambda b, j: (b, j, 0)),
        out_shape=jax.ShapeDtypeStruct((bsz, n_lat, d), F32),
        compiler_params=_cparams("parallel", "parallel"),
    )(x3, g.reshape(1, d))


def _dispatch_tables(eid, n_tok):
    a = n_tok * TOP_K
    bm = MOE_BLOCK
    pad = ROW_GROUP - 1
    flat_e = eid.reshape(a)
    experts = jnp.arange(N_EXPERTS, dtype=jnp.int32)
    counts = jnp.sum(flat_e[:, None] == experts[None, :], axis=0, dtype=jnp.int32)
    nblk_e = (counts + bm - 1) // bm
    blk_end = jnp.cumsum(nblk_e)
    blk_first = blk_end - nblk_e
    order = jnp.argsort(jnp.concatenate([flat_e, jnp.repeat(experts, pad)])).astype(jnp.int32)
    seg = counts + pad
    seg_start = jnp.cumsum(seg) - seg
    src_rows = jnp.where(order < a, lax.shift_right_logical(order, 1), 0) * SUBLANES
    dst_rows = order * SUBLANES
    n_blocks = -(-a // bm) + N_EXPERTS
    blk = jnp.arange(n_blocks + 1, dtype=jnp.int32)
    blk_e = jnp.minimum(jnp.sum(blk_end[None, :] <= blk[:, None], axis=1, dtype=jnp.int32), N_EXPERTS - 1)
    within = (blk - blk_first[blk_e]) * bm
    n_used = blk_end[-1].reshape(1)
    blk_start = jnp.clip(seg_start[blk_e] + within, 0, order.shape[0] - 1)
    blk_cnt = jnp.where(blk < n_used[0], jnp.clip(counts[blk_e] - within, 0, bm), 0)
    return src_rows, dst_rows, blk_e, blk_start, blk_cnt, n_used


def _moe(layer, h2t, route, wg, wu, wd):
    n_tok = route.shape[0]
    eid = route[:, 2:4].astype(jnp.int32)
    src_rows, dst_rows, blk_e, blk_start, blk_cnt, n_used = _dispatch_tables(eid, n_tok)
    return _expert_ffn(layer, blk_e, n_used, blk_start, blk_cnt, src_rows, dst_rows, h2t, wg, wu, wd, n_tok)


def kernel(x, c, ctx, c_ctx, w_ada, b_ada, norm1_g, norm2_g, w_in, s5_lam_re, s5_lam_im, s5_log_dt, s5_b_re, s5_b_im, s5_c_re, s5_c_im, s5_d, s5_w_glu, hgrn_lb_raw, w_branch_s5, w_branch_hgrn, w_branch_ret, w_out, moe_w_group, moe_b_group, moe_w_expert, moe_b_expert, moe_w_gate, moe_w_up, moe_w_down, final_norm_g):
    bsz, n_lat, d = x.shape
    n_ctx = ctx.shape[1]
    t_len = n_ctx + n_lat
    depth = w_ada.shape[0]
    m = bsz * t_len
    w_s5 = s5_d.shape[1]
    w_hg = w_branch_hgrn.shape[1]
    w_ret = w_branch_ret.shape[1]
    n_heads = w_ret // HEAD_DIM

    pad_rows = (-(bsz + 1)) % SUBLANES
    cc = jnp.concatenate([c, c_ctx[None, :], jnp.zeros((pad_rows, d), F32)], axis=0)
    mods = _ada_tables(cc, w_ada, b_ada).reshape(depth, bsz + 1 + pad_rows, N_MOD, d)

    p_lb = jax.nn.softmax(hgrn_lb_raw.astype(F32), axis=0)
    lower_bounds = jnp.cumsum(p_lb, axis=0) - p_lb[0:1]
    ret_consts = _ret_constants(n_heads, t_len, n_ctx)
    ones_bd = jnp.kron(jnp.eye(w_hg // HEAD_DIM, dtype=F32), jnp.ones((HEAD_DIM, HEAD_DIM), F32)).astype(BF16)

    xs = jnp.concatenate([ctx, x], axis=1).reshape(m, d)
    off_hg = w_s5
    off_ret = off_hg + 5 * w_hg
    off_gz = off_ret + 4 * w_ret
    w_in_bf = w_in.astype(BF16)
    for l in range(depth):
        mod_l = mods[l, :bsz]
        mod_c = mods[l, bsz:bsz + 1]
        u, hg, ret, gz = _project(xs, norm1_g[l], mod_l, mod_c, w_in_bf, l,
                                  (0, off_hg, off_ret, off_gz, w_in.shape[2]), t_len, n_ctx)

        s5w = _s5_weights(s5_lam_re[l], s5_lam_im[l], s5_log_dt[l], s5_b_re[l], s5_b_im[l],
                          s5_c_re[l], s5_c_im[l], s5_d[l])
        ys5 = _s5_mix(u.reshape(bsz, t_len, w_s5), *s5w, n_ctx).reshape(m, w_s5)
        hg_of, hg_ob = _hg_mix(hg.reshape(bsz, t_len, 5 * w_hg), lower_bounds[l], n_ctx)
        ret_of, ret_ob = _ret_mix(ret.reshape(bsz, t_len, 4 * w_ret), ret_consts, n_ctx)

        wr = jnp.concatenate([moe_w_group[l], moe_w_expert[l],
                              jnp.zeros((d, LANES - N_GROUPS - N_EXPERTS), F32)], axis=1)
        br = jnp.concatenate([moe_b_group[l], moe_b_expert[l],
                              jnp.zeros((LANES - N_GROUPS - N_EXPERTS,), F32)])[None, :]
        wr_hi = wr.astype(BF16)
        wr = jnp.concatenate([wr_hi, (wr - wr_hi.astype(F32)).astype(BF16)], axis=1)
        wts = (s5_w_glu[l].astype(BF16), w_branch_s5[l].astype(BF16), w_branch_hgrn[l].astype(BF16),
               w_branch_ret[l].astype(BF16), w_out[l].astype(BF16), ones_bd, wr, br)
        xs, h2, route = _merge(xs, ys5, hg_of.reshape(m, w_hg), hg_ob.reshape(m, w_hg), hg,
                               ret_of.reshape(m, w_ret), ret_ob.reshape(m, w_ret), ret, gz,
                               mod_l, mod_c, norm2_g[l].reshape(1, d), wts, t_len, n_ctx)
        y2 = _moe(jnp.full((1,), l, jnp.int32), h2, route, moe_w_gate, moe_w_up, moe_w_down)
        xs = _combine(xs, y2, route, mod_l, mod_c, t_len, n_ctx)
    return _final_norm(xs.reshape(bsz, t_len, d), final_norm_g, n_ctx)
```

```python
import functools

import jax
import jax.numpy as jnp
import numpy as np
from jax import lax
from jax.experimental import pallas as pl
from jax.experimental.pallas import tpu as pltpu

F32 = jnp.float32
BF16 = jnp.bfloat16
HIGHEST = lax.Precision.HIGHEST

LANES = 128
SUBLANES = 8
VMEM_LIMIT = 56 * 1024 * 1024

EPS = 1e-6
N_MOD = 6
GRID_W = 64
HEAD_DIM = 64
PAIR = 2 * HEAD_DIM
S5_GROUP = 16
S5_STATE = 64
S5_CHUNK = 16
HG_CHUNK = 64
HG_HALF = HG_CHUNK // 2
HG_BLOCK = 256
RET_BLOCK = 256
ROPE_BASE = 10000.0
N_GROUPS = 4
EXPERTS_PER_GROUP = 8
N_EXPERTS = N_GROUPS * EXPERTS_PER_GROUP
TOP_K = 2
MOE_BLOCK = 512
ROW_GROUP = 8


def _cparams(*sem):
    return pltpu.CompilerParams(dimension_semantics=sem, vmem_limit_bytes=VMEM_LIMIT)


def _row_tile(t_len):
    for tm in (544, 512, 384, 272, 256, 128, 64, 32, 16, 8):
        if t_len % tm == 0:
            return tm
    raise ValueError(f"unsupported stream length {t_len}")


def _ada_kernel(c_ref, w_ref, b_ref, o_ref):
    a = jax.nn.silu(c_ref[...]).astype(BF16)
    o_ref[0] = jnp.dot(a, w_ref[0].astype(BF16), preferred_element_type=F32) + b_ref[0]


def _ada_tables(cc, w_ada, b_ada):
    depth, d, n = w_ada.shape
    rows = cc.shape[0]
    tn = 1536 if n % 1536 == 0 else n
    return pl.pallas_call(
        _ada_kernel,
        grid=(depth, n // tn),
        in_specs=[
            pl.BlockSpec((rows, d), lambda l, j: (0, 0)),
            pl.BlockSpec((1, d, tn), lambda l, j: (l, 0, j)),
            pl.BlockSpec((1, 1, tn), lambda l, j: (l, 0, j)),
        ],
        out_specs=pl.BlockSpec((1, rows, tn), lambda l, j: (l, 0, j)),
        out_shape=jax.ShapeDtypeStruct((depth, rows, n), F32),
        compiler_params=_cparams("parallel", "parallel"),
    )(cc, w_ada, b_ada.reshape(depth, 1, n))


def _mod_rows(ml_ref, mc_ref, idx, is_ctx):
    return jnp.where(is_ctx, mc_ref[0, idx:idx + 1, :], ml_ref[0, idx:idx + 1, :])


def _is_ctx_rows(tm, tiles_per_b, n_ctx):
    jt = pl.program_id(0) % tiles_per_b
    row = jt * tm + lax.broadcasted_iota(jnp.int32, (tm, 1), 0)
    return row < n_ctx


def _store_token_tiles(ref, val):
    n_rows = val.shape[0]
    for k in range(val.shape[1] // LANES):
        ref[pl.ds(k, n_rows, stride=SUBLANES), :] = val[:, LANES * k:LANES * (k + 1)]


def _load_token_tiles(ref):
    n_rows = ref.shape[0] // SUBLANES
    return jnp.concatenate([ref[pl.ds(k, n_rows, stride=SUBLANES), :] for k in range(SUBLANES)], axis=1)


def _rmsnorm_rows(x, g):
    return x * lax.rsqrt(jnp.mean(x * x, axis=-1, keepdims=True) + EPS) * g


def _proj_kernel(x_ref, g_ref, ml_ref, mc_ref, w_ref, *out_refs, tm, tiles_per_b, n_ctx, offsets):
    is_ctx = _is_ctx_rows(tm, tiles_per_b, n_ctx)
    y = _rmsnorm_rows(x_ref[...], g_ref[...])
    y = y * (1.0 + _mod_rows(ml_ref, mc_ref, 1, is_ctx)) + _mod_rows(ml_ref, mc_ref, 0, is_ctx)
    h = y.astype(BF16)
    for k, o_ref in enumerate(out_refs):
        w = w_ref[0, :, offsets[k]:offsets[k + 1]]
        o_ref[...] = jnp.dot(h, w, preferred_element_type=F32).astype(o_ref.dtype)


def _project(x2, g, mod_l, mod_c, w_all, layer, offsets, t_len, n_ctx):
    m, d = x2.shape
    tm = _row_tile(t_len)
    tpb = t_len // tm
    kern = functools.partial(_proj_kernel, tm=tm, tiles_per_b=tpb, n_ctx=n_ctx, offsets=offsets)
    widths = [offsets[k + 1] - offsets[k] for k in range(len(offsets) - 1)]
    out_dtypes = (F32, F32, F32, BF16)
    return pl.pallas_call(
        kern,
        grid=(m // tm,),
        in_specs=[
            pl.BlockSpec((tm, d), lambda i: (i, 0)),
            pl.BlockSpec((1, d), lambda i: (0, 0)),
            pl.BlockSpec((1, N_MOD, d), lambda i: (i // tpb, 0, 0)),
            pl.BlockSpec((1, N_MOD, d), lambda i: (0, 0, 0)),
            pl.BlockSpec((1,) + w_all.shape[1:], lambda i: (layer, 0, 0), pipeline_mode=pl.Buffered(1)),
        ],
        out_specs=[pl.BlockSpec((tm, w), lambda i: (i, 0)) for w in widths],
        out_shape=[jax.ShapeDtypeStruct((m, w), dt) for w, dt in zip(widths, out_dtypes)],
        compiler_params=_cparams("parallel"),
    )(x2, g.reshape(1, d), mod_l, mod_c, w_all)


def _s5_weights(lam_re, lam_im, log_dt, b_re, b_im, c_re, c_im, d_skip):
    n_dir, g_n, p_n = lam_re.shape
    h_n = b_re.shape[-1]
    cn = S5_CHUNK
    lr, li = lam_re.astype(F32), lam_im.astype(F32)
    dt = jnp.exp(log_dt.astype(F32))[..., None]
    mag = jnp.exp(lr * dt)
    ar = mag * jnp.cos(li * dt)
    ai = mag * jnp.sin(li * dt)
    den = lr * lr + li * li
    zr = ((ar - 1.0) * lr + ai * li) / den
    zi = (ai * lr - (ar - 1.0) * li) / den
    bbr = zr[..., None] * b_re - zi[..., None] * b_im
    bbi = zr[..., None] * b_im + zi[..., None] * b_re
    j = jnp.arange(cn + 1, dtype=F32)[:, None, None, None]
    pmag = jnp.exp(lr[None] * dt[None] * j)
    pang = li[None] * dt[None] * j
    pr = pmag * jnp.cos(pang)
    pi = pmag * jnp.sin(pang)
    wr = pr[..., None] * bbr[None] - pi[..., None] * bbi[None]
    wi = pr[..., None] * bbi[None] + pi[..., None] * bbr[None]
    kj = (jnp.einsum('dghp,jdgpk->jdghk', c_re, wr, precision=HIGHEST)
          - jnp.einsum('dghp,jdgpk->jdghk', c_im, wi, precision=HIGHEST))
    s_idx = jnp.arange(cn)[:, None]
    t_idx = jnp.arange(cn)[None, :]
    lag = t_idx - s_idx
    kf = kj[:cn, 0][jnp.clip(lag, 0, cn - 1)]
    kb = kj[:cn, 1][jnp.clip(-lag, 0, cn - 1)]
    kf = jnp.where((lag >= 0)[:, :, None, None, None], kf, 0.0)
    kb = jnp.where((lag <= 0)[:, :, None, None, None], kb, 0.0)
    dsk = d_skip.astype(F32).reshape(g_n, h_n)
    eye_h = jnp.eye(h_n, dtype=F32)
    diag = (lag == 0)[:, :, None, None, None] * (dsk[None, None, :, :, None] * eye_h[None, None, None])
    m_full = kf + kb + diag
    m_mat = m_full.transpose(2, 0, 4, 1, 3).reshape(g_n, cn * h_n, cn * h_n)
    wf_r = wr[:cn, 0][::-1]
    wf_i = wi[:cn, 0][::-1]
    wb_r = wr[:cn, 1]
    wb_i = wi[:cn, 1]
    w_in = jnp.stack([wf_r, wf_i, wb_r, wb_i], axis=0)
    w_in = w_in.transpose(2, 1, 4, 0, 3).reshape(g_n, cn * h_n, 4 * p_n)
    def out_map(d, powers):
        prd, pid = pr[powers, d], pi[powers, d]
        w_re = c_re[d][None] * prd[:, :, None, :] - c_im[d][None] * pid[:, :, None, :]
        w_im = c_re[d][None] * pid[:, :, None, :] + c_im[d][None] * prd[:, :, None, :]
        return w_re, -w_im
    t_arr = jnp.arange(cn)
    of_r, of_i = out_map(0, t_arr + 1)
    ob_r, ob_i = out_map(1, cn - t_arr)
    w_out = jnp.stack([of_r, of_i, ob_r, ob_i], axis=0)
    w_out = w_out.transpose(2, 0, 4, 1, 3).reshape(g_n, 4 * p_n, cn * h_n)
    slot = jax.nn.one_hot(jnp.arange(g_n) % 2, 2, dtype=F32)
    w_in = w_in.reshape(g_n, cn * h_n, 4, 1, p_n) * slot[:, None, None, :, None]
    w_in = w_in.reshape(g_n, cn * h_n, 8 * p_n)
    w_out = w_out.reshape(g_n, 4, 1, p_n, cn * h_n) * slot[:, None, :, None, None]
    w_out = w_out.reshape(g_n, 8 * p_n, cn * h_n)
    a_mat = jnp.stack([pr[cn, 0].reshape(-1), pi[cn, 0].reshape(-1),
                       pr[cn, 1].reshape(-1), pi[cn, 1].reshape(-1)], axis=0)
    w1 = jnp.concatenate([m_mat, w_in], axis=-1).astype(BF16)
    return w1, w_out.astype(BF16), a_mat


def _s5_fold_perm():
    j, g8, h = np.meshgrid(np.arange(8), np.arange(8), np.arange(S5_GROUP), indexing='ij')
    src = (j * 8 + g8) * S5_GROUP + h
    dst = (g8 * 8 + j) * S5_GROUP + h
    perm = np.zeros((8 * LANES, 8 * LANES), np.float32)
    perm[src.reshape(-1), dst.reshape(-1)] = 1.0
    return jnp.asarray(perm, dtype=BF16)


def _s5_kernel(u_ref, perm_ref, w1_ref, wo_ref, a_ref, y_ref, ug_scr, yg_scr, xfr, xfi, xbr, xbi,
               *, n_chunks, n_ctx_chunks, n_groups):
    cn = S5_CHUNK
    blk = cn * S5_GROUP
    tok_per_slab = LANES // S5_GROUP
    for c in range(cn // tok_per_slab):
        slabs = [u_ref[0, pl.ds(tok_per_slab * c + j, n_chunks, stride=cn), :] for j in range(tok_per_slab)]
        s = jnp.concatenate(slabs, axis=1).astype(BF16)
        up = jnp.dot(s, perm_ref[...], preferred_element_type=F32).astype(BF16)
        for g in range(n_groups):
            ug_scr[g, :, LANES * c:LANES * (c + 1)] = up[:, LANES * g:LANES * (g + 1)]
    xs = (xfr, xfi, xbr, xbi)
    for g in range(n_groups):
        z = jnp.dot(ug_scr[g], w1_ref[g], preferred_element_type=F32)
        yg_scr[g] = z[:, :blk]
        lanes = slice(LANES * (g // 2), LANES * (g // 2 + 1))
        for k, xr in enumerate(xs):
            piece = z[:, blk + LANES * k:blk + LANES * (k + 1)]
            if g % 2 == 0:
                xr[:, lanes] = piece
            else:
                xr[:, lanes] += piece
    arf, aif, arb, aib = a_ref[0:1, :], a_ref[1:2, :], a_ref[2:3, :], a_ref[3:4, :]

    def body(i, carry):
        fr, fi, br, bi = carry
        nb = jnp.where(i < n_ctx_chunks, n_ctx_chunks - 1 - i, n_chunks - 1 - (i - n_ctx_chunks))
        x_fr, x_fi = xfr[pl.ds(i, 1), :], xfi[pl.ds(i, 1), :]
        x_br, x_bi = xbr[pl.ds(nb, 1), :], xbi[pl.ds(nb, 1), :]
        xfr[pl.ds(i, 1), :] = fr
        xfi[pl.ds(i, 1), :] = fi
        xbr[pl.ds(nb, 1), :] = br
        xbi[pl.ds(nb, 1), :] = bi
        return (arf * fr - aif * fi + x_fr, arf * fi + aif * fr + x_fi,
                arb * br - aib * bi + x_br, arb * bi + aib * br + x_bi)

    zero = jnp.zeros((1, xfr.shape[1]), F32)
    lax.fori_loop(0, n_chunks, body, (zero, zero, zero, zero))
    for g in range(n_groups):
        lanes = slice(LANES * (g // 2), LANES * (g // 2 + 1))
        hp = jnp.concatenate([xr[:, lanes] for xr in xs], axis=1).astype(BF16)
        yg_scr[g] += jnp.dot(hp, wo_ref[g], preferred_element_type=F32)
    for c in range(cn // tok_per_slab):
        ycat = jnp.concatenate([yg_scr[g, :, LANES * c:LANES * (c + 1)] for g in range(n_groups)], axis=1)
        hi = ycat.astype(BF16)
        lo = (ycat - hi.astype(F32)).astype(BF16)
        r = _dot_nt(hi, perm_ref[...]) + _dot_nt(lo, perm_ref[...])
        for j in range(tok_per_slab):
            y_ref[0, pl.ds(tok_per_slab * c + j, n_chunks, stride=cn), :] = r[:, LANES * j:LANES * (j + 1)]


def _s5_mix(u, w1, w_out, a_mat, n_ctx):
    bsz, t_len, width = u.shape
    g_n = width // S5_GROUP
    cn = S5_CHUNK
    n_chunks = t_len // cn
    blk = cn * S5_GROUP
    perm = _s5_fold_perm()
    gs = LANES // S5_GROUP
    state_w = gs * S5_STATE
    kern = functools.partial(_s5_kernel, n_chunks=n_chunks, n_ctx_chunks=n_ctx // cn, n_groups=gs)
    return pl.pallas_call(
        kern,
        grid=(bsz, g_n // gs),
        in_specs=[pl.BlockSpec((1, t_len, LANES), lambda b, s: (b, 0, s)),
                  pl.BlockSpec(perm.shape, lambda b, s: (0, 0), pipeline_mode=pl.Buffered(1)),
                  pl.BlockSpec((gs,) + w1.shape[1:], lambda b, s: (s, 0, 0)),
                  pl.BlockSpec((gs,) + w_out.shape[1:], lambda b, s: (s, 0, 0)),
                  pl.BlockSpec((4, state_w), lambda b, s: (0, s))],
        out_specs=pl.BlockSpec((1, t_len, LANES), lambda b, s: (b, 0, s)),
        out_shape=jax.ShapeDtypeStruct((bsz, t_len, width), F32),
        scratch_shapes=[pltpu.VMEM((gs, n_chunks, blk), BF16), pltpu.VMEM((gs, n_chunks, blk), F32)]
        + [pltpu.VMEM((n_chunks, state_w), F32)] * 4,
        compiler_params=_cparams("parallel", "parallel"),
    )(u, perm, w1, w_out, a_mat)


def _head0_lanes(shape):
    return lax.broadcasted_iota(jnp.int32, shape, len(shape) - 1) < HEAD_DIM


def _stack_heads(q):
    m0 = _head0_lanes(q.shape)
    return jnp.concatenate([jnp.where(m0, q, 0.0), jnp.where(m0, 0.0, q)], axis=-2)


def _unstack_heads(o2):
    c = o2.shape[-2] // 2
    top, bottom = o2[..., :c, :], o2[..., c:, :]
    return jnp.where(_head0_lanes(top.shape), top, bottom)


def _pair_block_mask():
    r = lax.broadcasted_iota(jnp.int32, (PAIR, PAIR), 0) < HEAD_DIM
    c = lax.broadcasted_iota(jnp.int32, (PAIR, PAIR), 1) < HEAD_DIM
    return r == c


def _dot_nt(a, b):
    return lax.dot_general(a, b, (((1,), (1,)), ((), ())), preferred_element_type=F32)


def _dot_tn(a, b):
    return lax.dot_general(a, b, (((0,), (0,)), ((), ())), preferred_element_type=F32)


def _bwd_block(j, n_ctx_blocks, n_blocks):
    return jnp.where(j < n_ctx_blocks, n_ctx_blocks - 1 - j, n_blocks - 1 - (j - n_ctx_blocks))


def _bdot_nt(a, b):
    return lax.dot_general(a, b, (((2,), (2,)), ((0,), (0,))), preferred_element_type=F32)


def _bdot_nn(a, b):
    return lax.dot_general(a, b, (((2,), (1,)), ((0,), (0,))), preferred_element_type=F32)


def _hg_block(q, z, v, lb, ht, reverse):
    rows = q.shape[0]
    c = HG_CHUNK
    hh = HG_HALF
    n = rows // c
    f = lb + (1.0 - lb) * jax.nn.sigmoid(z)
    logf = jnp.log(f)
    kk = 1.0 - f
    ri = lax.broadcasted_iota(jnp.int32, (n, c, c), 1)
    ci = lax.broadcasted_iota(jnp.int32, (n, c, c), 2)
    tri = jnp.where((ci >= ri) if reverse else (ci <= ri), 1.0, 0.0).astype(BF16)
    l1 = logf.astype(BF16)
    r1 = logf - l1.astype(F32)
    l2 = r1.astype(BF16)
    l3 = (r1 - l2.astype(F32)).astype(BF16)
    b3 = _bdot_nn(tri, jnp.concatenate([l1, l2, l3], axis=1).reshape(n, c, 3 * PAIR))
    b4 = b3[..., 0:PAIR] + b3[..., PAIR:2 * PAIR] + b3[..., 2 * PAIR:3 * PAIR]
    q4, k4 = q.reshape(n, c, PAIR), kk.reshape(n, c, PAIR)
    vb = v.astype(BF16).reshape(n, c, PAIR)
    if reverse:
        first, second = slice(hh, c), slice(0, hh)
        r = b4[:, hh:hh + 1]
        bend = b4[:, 0:1]
    else:
        first, second = slice(0, hh), slice(hh, c)
        r = b4[:, hh - 1:hh]
        bend = b4[:, c - 1:c]
    b1, b2 = b4[:, first], b4[:, second]
    qd1 = q4[:, first] * jnp.exp(b1)
    qd2 = q4[:, second] * jnp.exp(b2 - r)
    k1d = k4[:, first] * jnp.exp(-b1)
    k2 = k4 * jnp.exp(r - b4)
    s1 = _bdot_nt(_stack_heads(qd1).astype(BF16), k1d.astype(BF16))
    s2 = _bdot_nt(_stack_heads(qd2).astype(BF16), k2.astype(BF16))
    t1 = lax.broadcasted_iota(jnp.int32, (n, 2 * hh, hh), 1) % hh
    c1 = lax.broadcasted_iota(jnp.int32, (n, 2 * hh, hh), 2)
    t2 = lax.broadcasted_iota(jnp.int32, (n, 2 * hh, c), 1) % hh
    c2 = lax.broadcasted_iota(jnp.int32, (n, 2 * hh, c), 2)
    if reverse:
        s1 = jnp.where(c1 >= t1, s1, 0.0)
        s2 = jnp.where(c2 >= t2, s2, 0.0)
    else:
        s1 = jnp.where(c1 <= t1, s1, 0.0)
        s2 = jnp.where(c2 <= t2 + hh, s2, 0.0)
    o1 = _unstack_heads(_bdot_nn(s1.astype(BF16), vb[:, first]))
    o2 = _unstack_heads(_bdot_nn(s2.astype(BF16), vb))
    o = jnp.concatenate([o2, o1] if reverse else [o1, o2], axis=1)
    ke = (k4 * jnp.exp(bend - b4)).astype(BF16)
    dec = jnp.exp(bend)
    bmask = _pair_block_mask()
    states = [None] * n
    for ch in (range(n - 1, -1, -1) if reverse else range(n)):
        states[ch] = ht.astype(BF16)
        ht = ht * dec[ch] + jnp.where(bmask, _dot_tn(vb[ch], ke[ch]), 0.0)
    o = o + _bdot_nt((q4 * jnp.exp(b4)).astype(BF16), jnp.stack(states, axis=0))
    return o.reshape(rows, PAIR), ht


def _hg_kernel(qf_ref, zf_ref, vf_ref, qb_ref, zb_ref, vb_ref, lb_ref, of_ref, ob_ref, hf_scr, hb_scr, *, n_pairs):
    @pl.when(pl.program_id(1) == 0)
    def _():
        hf_scr[...] = jnp.zeros_like(hf_scr)
        hb_scr[...] = jnp.zeros_like(hb_scr)

    for p in range(n_pairs):
        sl = slice(p * PAIR, (p + 1) * PAIR)
        o, hn = _hg_block(qf_ref[0, :, sl], zf_ref[0, :, sl], vf_ref[0, :, sl], lb_ref[0:1, sl], hf_scr[p], False)
        of_ref[0, :, sl] = o
        hf_scr[p] = hn
        o, hn = _hg_block(qb_ref[0, :, sl], zb_ref[0, :, sl], vb_ref[0, :, sl], lb_ref[1:2, sl], hb_scr[p], True)
        ob_ref[0, :, sl] = o
        hb_scr[p] = hn


def _hg_mix(hg, lb, n_ctx):
    bsz, t_len, total = hg.shape
    w = total // 5
    n_pairs = w // PAIR
    c = HG_BLOCK
    nc, ncc = t_len // c, n_ctx // c
    bw = functools.partial(_bwd_block, n_ctx_blocks=ncc, n_blocks=nc)

    def fspec(col):
        return pl.BlockSpec((1, c, w), lambda b, j: (b, j, col))

    def bspec(col):
        return pl.BlockSpec((1, c, w), lambda b, j: (b, bw(j), col))

    kern = functools.partial(_hg_kernel, n_pairs=n_pairs)
    return pl.pallas_call(
        kern,
        grid=(bsz, nc),
        in_specs=[fspec(0), fspec(1), fspec(3), bspec(0), bspec(2), bspec(3),
                  pl.BlockSpec((2, w), lambda b, j: (0, 0))],
        out_specs=[pl.BlockSpec((1, c, w), lambda b, j: (b, j, 0)),
                   pl.BlockSpec((1, c, w), lambda b, j: (b, bw(j), 0))],
        out_shape=[jax.ShapeDtypeStruct((bsz, t_len, w), F32)] * 2,
        scratch_shapes=[pltpu.VMEM((n_pairs, PAIR, PAIR), F32)] * 2,
        compiler_params=_cparams("parallel", "arbitrary"),
    )(hg, hg, hg, hg, hg, hg, lb)


def _rope_rows(x, cos, sin):
    hd2 = HEAD_DIM // 2
    lane = lax.broadcasted_iota(jnp.int32, x.shape, 1) % HEAD_DIM
    swapped = jnp.where(lane < hd2, pltpu.roll(x, PAIR - hd2, 1), pltpu.roll(x, hd2, 1))
    return x * cos + swapped * sin


def _ret_kernel(qf_ref, kf_ref, vf_ref, cf_ref, sf_ref, qb_ref, kb_ref, vb_ref, cb_ref, sb_ref,
                d_ref, gqf_ref, gkf_ref, gqb_ref, gkb_ref, gh_ref, of_ref, ob_ref, hf_scr, hb_scr, *, n_pairs):
    @pl.when(pl.program_id(1) == 0)
    def _():
        hf_scr[...] = jnp.zeros_like(hf_scr)
        hb_scr[...] = jnp.zeros_like(hb_scr)

    kscale = HEAD_DIM ** -0.5
    bmask = _pair_block_mask()
    for p in range(n_pairs):
        sl = slice(p * PAIR, (p + 1) * PAIR)
        q = _rope_rows(qf_ref[0, :, sl], cf_ref[...], sf_ref[...])
        k = _rope_rows(kf_ref[0, :, sl] * kscale, cf_ref[...], sf_ref[...])
        vb16 = vf_ref[0, :, sl].astype(BF16)
        s = _dot_nt(_stack_heads(q).astype(BF16), k.astype(BF16)) * d_ref[p]
        o = _unstack_heads(jnp.dot(s.astype(BF16), vb16, preferred_element_type=F32))
        ht = hf_scr[p]
        o = o + _dot_nt((q * gqf_ref[:, sl]).astype(BF16), ht.astype(BF16))
        upd = _dot_tn(vb16, (k * gkf_ref[:, sl]).astype(BF16))
        hf_scr[p] = ht * gh_ref[p] + jnp.where(bmask, upd, 0.0)
        of_ref[0, :, sl] = o

        q = _rope_rows(qb_ref[0, :, sl], cb_ref[...], sb_ref[...])
        k = _rope_rows(kb_ref[0, :, sl] * kscale, cb_ref[...], sb_ref[...])
        vb16 = vb_ref[0, :, sl].astype(BF16)
        ht = hb_scr[p]
        ob_ref[0, :, sl] = _dot_nt((q * gqb_ref[:, sl]).astype(BF16), ht.astype(BF16))
        upd = _dot_tn(vb16, (k * gkb_ref[:, sl]).astype(BF16))
        hb_scr[p] = ht * gh_ref[p] + jnp.where(bmask, upd, 0.0)


def _ret_constants(n_heads, t_len, n_ctx):
    c = RET_BLOCK
    log_gamma = jnp.log(1.0 - 2.0 ** (-5.0 - jnp.arange(n_heads, dtype=F32)))
    t = jnp.arange(c, dtype=F32)
    dist = jnp.abs(t[:, None] - t[None, :])
    dmat = jnp.exp(log_gamma[:, None, None] * dist[None])
    dmat = dmat.reshape(n_heads // 2, 2 * c, c)
    lg_lane = jnp.repeat(log_gamma, HEAD_DIM)[None, :]
    gqf = jnp.exp(lg_lane * (t[:, None] + 1.0))
    gkf = jnp.exp(lg_lane * (c - 1.0 - t[:, None]))
    gqb = jnp.exp(lg_lane * (c - t[:, None]))
    gkb = jnp.exp(lg_lane * t[:, None])
    gh = jnp.exp(lg_lane * float(c)).reshape(n_heads // 2, 1, PAIR)
    gh = jnp.broadcast_to(gh, (n_heads // 2, PAIR, PAIR))
    n_lat = t_len - n_ctx
    rows = n_lat // GRID_W
    row = jnp.repeat(jnp.arange(rows, dtype=F32), GRID_W)
    col = jnp.broadcast_to(jnp.arange(GRID_W, dtype=F32)[None, :], (rows, GRID_W)).reshape(-1)
    n_freq = HEAD_DIM // 4
    inv_freq = ROPE_BASE ** (-jnp.arange(n_freq, dtype=F32) / n_freq)
    ang = jnp.concatenate([row[:, None] * inv_freq, col[:, None] * inv_freq], axis=-1)
    cos_l, sin_l = jnp.cos(ang), jnp.sin(ang)
    cos_h = jnp.concatenate([cos_l, cos_l], axis=-1)
    sin_h = jnp.concatenate([-sin_l, sin_l], axis=-1)
    cos_t = jnp.concatenate([jnp.ones((n_ctx, HEAD_DIM), F32), cos_h], axis=0)
    sin_t = jnp.concatenate([jnp.zeros((n_ctx, HEAD_DIM), F32), sin_h], axis=0)
    cos_t = jnp.concatenate([cos_t, cos_t], axis=-1)
    sin_t = jnp.concatenate([sin_t, sin_t], axis=-1)
    return dmat, gqf, gkf, gqb, gkb, gh, cos_t, sin_t


def _ret_mix(ret, consts, n_ctx):
    dmat, gqf, gkf, gqb, gkb, gh, cos_t, sin_t = consts
    bsz, t_len, total = ret.shape
    w = total // 4
    n_pairs = w // PAIR
    c = RET_BLOCK
    nb, ncb = t_len // c, n_ctx // c
    bw = functools.partial(_bwd_block, n_ctx_blocks=ncb, n_blocks=nb)

    def fspec(col):
        return pl.BlockSpec((1, c, w), lambda b, j: (b, j, col))

    def bspec(col):
        return pl.BlockSpec((1, c, w), lambda b, j: (b, bw(j), col))

    def const(a):
        nd = a.ndim
        return pl.BlockSpec(a.shape, lambda b, j: (0,) * nd)

    rope_f = pl.BlockSpec((c, PAIR), lambda b, j: (j, 0))
    rope_b = pl.BlockSpec((c, PAIR), lambda b, j: (bw(j), 0))
    kern = functools.partial(_ret_kernel, n_pairs=n_pairs)
    return pl.pallas_call(
        kern,
        grid=(bsz, nb),
        in_specs=[fspec(0), fspec(1), fspec(2), rope_f, rope_f, bspec(0), bspec(1), bspec(2), rope_b, rope_b,
                  const(dmat), const(gqf), const(gkf), const(gqb), const(gkb), const(gh)],
        out_specs=[pl.BlockSpec((1, c, w), lambda b, j: (b, j, 0)),
                   pl.BlockSpec((1, c, w), lambda b, j: (b, bw(j), 0))],
        out_shape=[jax.ShapeDtypeStruct((bsz, t_len, w), F32)] * 2,
        scratch_shapes=[pltpu.VMEM((n_pairs, PAIR, PAIR), F32)] * 2,
        compiler_params=_cparams("parallel", "arbitrary"),
    )(ret, ret, ret, cos_t, sin_t, ret, ret, ret, cos_t, sin_t, dmat, gqf, gkf, gqb, gkb, gh)


def _head_norm_gate(o, g, ones_bd):
    sq = o * o
    hi = sq.astype(BF16)
    lo = (sq - hi.astype(F32)).astype(BF16)
    ss = jnp.dot(hi, ones_bd, preferred_element_type=F32) + jnp.dot(lo, ones_bd, preferred_element_type=F32)
    return o * lax.rsqrt(ss * (1.0 / HEAD_DIM) + EPS) * jax.nn.silu(g)


def _route(lg):
    lane = lax.broadcasted_iota(jnp.int32, lg.shape, 1).astype(F32)
    big = float(1 << 20)
    neg = -jnp.inf
    gmask = lane < N_GROUPS
    lgm = jnp.where(gmask, lg, neg)
    gmax = jnp.max(lgm, axis=1, keepdims=True)
    gsum = jnp.sum(jnp.where(gmask, jnp.exp(lgm - gmax), 0.0), axis=1, keepdims=True)
    gp = 1.0 / gsum
    gi = jnp.min(jnp.where(gmask & (lgm == gmax), lane, big), axis=1, keepdims=True)
    lo_lane = N_GROUPS + EXPERTS_PER_GROUP * gi
    emask = (lane >= lo_lane) & (lane < lo_lane + EXPERTS_PER_GROUP)
    l1 = jnp.max(jnp.where(emask, lg, neg), axis=1, keepdims=True)
    i1 = jnp.min(jnp.where(emask & (lg == l1), lane, big), axis=1, keepdims=True)
    mask2 = emask & (lane != i1)
    l2 = jnp.max(jnp.where(mask2, lg, neg), axis=1, keepdims=True)
    i2 = jnp.min(jnp.where(mask2 & (lg == l2), lane, big), axis=1, keepdims=True)
    e21 = jnp.exp(l2 - l1)
    w1 = gp / (1.0 + e21)
    w2 = gp * e21 / (1.0 + e21)
    e1 = i1 - N_GROUPS
    e2 = i2 - N_GROUPS
    return jnp.where(lane == 0, w1, jnp.where(lane == 1, w2, jnp.where(lane == 2, e1, jnp.where(lane == 3, e2, 0.0))))


def _merge_kernel(x_ref, ys_ref, hof_ref, hob_ref, hgg_ref, rof_ref, rob_ref, rgg_ref, gz_ref,
                  ml_ref, mc_ref, g2_ref, wglu_ref, wbs_ref, wbh_ref, wbr_ref, wout_ref, ones_ref,
                  wr_ref, br_ref, xo_ref, h2_ref, rt_ref, *, tm, tiles_per_b, n_ctx):
    d = x_ref.shape[1]
    is_ctx = _is_ctx_rows(tm, tiles_per_b, n_ctx)
    y = jax.nn.gelu(ys_ref[...], approximate=True)
    y = y * jax.nn.sigmoid(jnp.dot(y.astype(BF16), wglu_ref[...], preferred_element_type=F32))
    t_s5 = jnp.dot(y.astype(BF16), wbs_ref[...], preferred_element_type=F32)
    yh = _head_norm_gate(hof_ref[...] + hob_ref[...], hgg_ref[...], ones_ref[...])
    t_hg = jnp.dot(yh.astype(BF16), wbh_ref[...], preferred_element_type=F32)
    yr = _head_norm_gate(rof_ref[...] + rob_ref[...], rgg_ref[...], ones_ref[...])
    t_ret = jnp.dot(yr.astype(BF16), wbr_ref[...], preferred_element_type=F32)
    merged = (jax.nn.sigmoid(gz_ref[:, 0:d]) * t_s5 + jax.nn.sigmoid(gz_ref[:, d:2 * d]) * t_hg
              + jax.nn.sigmoid(gz_ref[:, 2 * d:3 * d]) * t_ret)
    mix = jnp.dot(merged.astype(BF16), wout_ref[...], preferred_element_type=F32)
    xn = x_ref[...] + _mod_rows(ml_ref, mc_ref, 2, is_ctx) * mix
    xo_ref[...] = xn
    h2 = _rmsnorm_rows(xn, g2_ref[...])
    h2 = h2 * (1.0 + _mod_rows(ml_ref, mc_ref, 4, is_ctx)) + _mod_rows(ml_ref, mc_ref, 3, is_ctx)
    _store_token_tiles(h2_ref, h2)
    h_hi = h2.astype(BF16)
    h_lo = (h2 - h_hi.astype(F32)).astype(BF16)
    p_hi = jnp.dot(h_hi, wr_ref[...], preferred_element_type=F32)
    p_lo = jnp.dot(h_lo, wr_ref[:, 0:LANES], preferred_element_type=F32)
    rt_ref[...] = _route(p_hi[:, 0:LANES] + p_hi[:, LANES:2 * LANES] + p_lo + br_ref[...])


def _merge(x2, ys5, hg_of, hg_ob, hg2, ret_of, ret_ob, ret2, gz, mod_l, mod_c, g2, wts, t_len, n_ctx):
    m, d = x2.shape
    w_s5 = ys5.shape[1]
    w_h = hg_of.shape[1]
    tm = _row_tile(t_len)
    tpb = t_len // tm
    wglu, wbs, wbh, wbr, wout, ones_bd, wr, br = wts

    def rows(width, col=0):
        return pl.BlockSpec((tm, width), lambda i: (i, col))

    def const(a):
        nd = a.ndim
        return pl.BlockSpec(a.shape, lambda i: (0,) * nd)

    kern = functools.partial(_merge_kernel, tm=tm, tiles_per_b=tpb, n_ctx=n_ctx)
    return pl.pallas_call(
        kern,
        grid=(m // tm,),
        in_specs=[rows(d), rows(w_s5), rows(w_h), rows(w_h), rows(w_h, 4), rows(w_h), rows(w_h), rows(w_h, 3),
                  rows(3 * d),
                  pl.BlockSpec((1, N_MOD, d), lambda i: (i // tpb, 0, 0)),
                  pl.BlockSpec((1, N_MOD, d), lambda i: (0, 0, 0)),
                  const(g2), const(wglu), const(wbs), const(wbh), const(wbr), const(wout), const(ones_bd),
                  const(wr), const(br)],
        out_specs=[rows(d), pl.BlockSpec((tm * SUBLANES, LANES), lambda i: (i, 0)), rows(LANES)],
        out_shape=[jax.ShapeDtypeStruct((m, d), F32), jax.ShapeDtypeStruct((m * SUBLANES, LANES), F32),
                   jax.ShapeDtypeStruct((m, LANES), F32)],
        compiler_params=_cparams("parallel"),
    )(x2, ys5, hg_of, hg_ob, hg2, ret_of, ret_ob, ret2, gz, mod_l, mod_c, g2, wglu, wbs, wbh, wbr, wout,
      ones_bd, wr, br)


def _ffn_kernel(lay_ref, be_ref, nu_ref, bs_ref, cnt_ref, src_ref, dst_ref, h2_hbm, wg_ref, wu_ref, wd_ref, y2_hbm,
                wg_scr, wu_scr, wd_scr, xbuf, obuf, gsem, ssem, *, n_spare_rows):
    i = pl.program_id(0)
    nu = nu_ref[0]
    slot = i % 2
    tile = SUBLANES

    def gather_copy(rows_src, rows_dst, sl):
        return pltpu.make_async_copy(h2_hbm.at[rows_src, :], xbuf.at[sl, rows_dst, :], gsem.at[sl])

    def scatter_copy(rows_src, rows_dst, sl):
        return pltpu.make_async_copy(obuf.at[sl, rows_src, :], y2_hbm.at[rows_dst, :], ssem.at[sl])

    def n_groups(blk):
        return lax.shift_right_logical(cnt_ref[blk] + (ROW_GROUP - 1), ROW_GROUP.bit_length() - 1)

    def start_gather(blk, sl):
        base = bs_ref[blk]

        def body(g, carry):
            for j in range(ROW_GROUP):
                r = g * ROW_GROUP + j
                gather_copy(pl.ds(pl.multiple_of(src_ref[base + r], tile), tile),
                            pl.ds(pl.multiple_of(r * tile, tile), tile), sl).start()
            return carry

        lax.fori_loop(0, n_groups(blk), body, 0)

    def wait_gather(blk, sl):
        n = pl.multiple_of(n_groups(blk) * (ROW_GROUP * tile), tile)
        gather_copy(pl.ds(0, n), pl.ds(0, n), sl).wait()

    def start_scatter(blk, sl):
        base = bs_ref[blk]

        def body(g, carry):
            for j in range(ROW_GROUP):
                r = g * ROW_GROUP + j
                scatter_copy(pl.ds(pl.multiple_of(r * tile, tile), tile),
                             pl.ds(pl.multiple_of(dst_ref[base + r], tile), tile), sl).start()
            return carry

        lax.fori_loop(0, n_groups(blk), body, 0)

    def wait_scatter(blk, sl):
        n = pl.multiple_of(n_groups(blk) * (ROW_GROUP * tile), tile)
        scatter_copy(pl.ds(0, n), pl.ds(0, n), sl).wait()

    @pl.when(i == 0)
    def _():
        xbuf[...] = jnp.zeros_like(xbuf)
        first_spare = y2_hbm.shape[0] - n_spare_rows
        clear = scatter_copy(pl.ds(0, n_spare_rows), pl.ds(first_spare, n_spare_rows), 0)
        obuf[0, pl.ds(0, n_spare_rows), :] = jnp.zeros((n_spare_rows, LANES), F32)
        clear.start()
        clear.wait()

    @pl.when((i == 0) & (nu > 0))
    def _():
        start_gather(0, 0)

    @pl.when(i + 1 < nu)
    def _():
        start_gather(i + 1, 1 - slot)

    prev = be_ref[jnp.maximum(i - 1, 0)]

    @pl.when((i < nu) & ((i == 0) | (be_ref[i] != prev)))
    def _():
        wg_scr[...] = wg_ref[0, 0].astype(BF16)
        wu_scr[...] = wu_ref[0, 0].astype(BF16)
        wd_scr[...] = wd_ref[0, 0].astype(BF16)

    @pl.when(i < nu)
    def _():
        wait_gather(i, slot)
        xb = _load_token_tiles(xbuf.at[slot]).astype(BF16)
        hid = (jax.nn.silu(jnp.dot(xb, wg_scr[...], preferred_element_type=F32))
               * jnp.dot(xb, wu_scr[...], preferred_element_type=F32))
        y = jnp.dot(hid.astype(BF16), wd_scr[...], preferred_element_type=F32)
        _store_token_tiles(obuf.at[slot], y)
        start_scatter(i, slot)

    @pl.when((i >= 1) & (i < nu + 1))
    def _():
        wait_scatter(i - 1, 1 - slot)

    @pl.when((i == pl.num_programs(0) - 1) & (i < nu))
    def _():
        wait_scatter(i, slot)


def _expert_ffn(layer, blk_e, n_used, blk_start, blk_cnt, src_rows, dst_rows, h2t, wg, wu, wd, n_tok):
    d, f = wg.shape[2], wg.shape[3]
    bm = MOE_BLOCK
    n_blocks = blk_e.shape[0] - 1
    n_spare = N_EXPERTS * (ROW_GROUP - 1)

    def wspec(shape):
        return pl.BlockSpec((1, 1) + shape, lambda i, lay, be, *_: (lay[0], be[i], 0, 0))

    grid_spec = pltpu.PrefetchScalarGridSpec(
        num_scalar_prefetch=7,
        grid=(n_blocks,),
        in_specs=[pl.BlockSpec(memory_space=pl.ANY), wspec((d, f)), wspec((d, f)), wspec((f, d))],
        out_specs=pl.BlockSpec(memory_space=pl.ANY),
        scratch_shapes=[pltpu.VMEM((d, f), BF16), pltpu.VMEM((d, f), BF16), pltpu.VMEM((f, d), BF16),
                        pltpu.VMEM((2, bm * SUBLANES, LANES), F32), pltpu.VMEM((2, bm * SUBLANES, LANES), F32),
                        pltpu.SemaphoreType.DMA((2,)), pltpu.SemaphoreType.DMA((2,))],
    )
    return pl.pallas_call(
        functools.partial(_ffn_kernel, n_spare_rows=n_spare * SUBLANES),
        grid_spec=grid_spec,
        out_shape=jax.ShapeDtypeStruct(((n_tok * TOP_K + n_spare) * SUBLANES, LANES), F32),
        compiler_params=_cparams("arbitrary"),
    )(layer, blk_e, n_used, blk_start, blk_cnt, src_rows, dst_rows, h2t, wg, wu, wd)


def _combine_kernel(x_ref, y2_ref, rt_ref, ml_ref, mc_ref, o_ref, *, tm, tiles_per_b, n_ctx):
    is_ctx = _is_ctx_rows(tm, tiles_per_b, n_ctx)
    rt = rt_ref[...]
    step = TOP_K * SUBLANES
    y0 = jnp.concatenate([y2_ref[pl.ds(k, tm, stride=step), :] for k in range(SUBLANES)], axis=1)
    y1 = jnp.concatenate([y2_ref[pl.ds(SUBLANES + k, tm, stride=step), :] for k in range(SUBLANES)], axis=1)
    y = rt[:, 0:1] * y0 + rt[:, 1:2] * y1
    o_ref[...] = x_ref[...] + _mod_rows(ml_ref, mc_ref, 5, is_ctx) * y


def _combine(x2, y2, route, mod_l, mod_c, t_len, n_ctx):
    m, d = x2.shape
    tm = _row_tile(t_len)
    tpb = t_len // tm
    kern = functools.partial(_combine_kernel, tm=tm, tiles_per_b=tpb, n_ctx=n_ctx)
    return pl.pallas_call(
        kern,
        grid=(m // tm,),
        in_specs=[pl.BlockSpec((tm, d), lambda i: (i, 0)),
                  pl.BlockSpec((tm * TOP_K * SUBLANES, LANES), lambda i: (i, 0)),
                  pl.BlockSpec((tm, LANES), lambda i: (i, 0)),
                  pl.BlockSpec((1, N_MOD, d), lambda i: (i // tpb, 0, 0)),
                  pl.BlockSpec((1, N_MOD, d), lambda i: (0, 0, 0))],
        out_specs=pl.BlockSpec((tm, d), lambda i: (i, 0)),
        out_shape=jax.ShapeDtypeStruct((m, d), F32),
        compiler_params=_cparams("parallel"),
    )(x2, y2, route, mod_l, mod_c)


def _final_norm_kernel(x_ref, g_ref, o_ref):
    o_ref[0] = _rmsnorm_rows(x_ref[0], g_ref[...])


def _final_norm(x3, g, n_ctx):
    bsz, t_len, d = x3.shape
    n_lat = t_len - n_ctx
    tm = n_ctx
    off = n_ctx // tm
    return pl.pallas_call(
        _final_norm_kernel,
        grid=(bsz, n_lat // tm),
        in_specs=[pl.BlockSpec((1, tm, d), lambda b, j: (b, j + off, 0)),
                  pl.BlockSpec((1, d), lambda b, j: (0, 0))],
        out_specs=pl.BlockSpec((1, tm, d), lambda b, j: (b, j, 0)),
        out_shape=jax.ShapeDtypeStruct((bsz, n_lat, d), F32),
        compiler_params=_cparams("parallel", "parallel"),
    )(x3, g.reshape(1, d))


def _dispatch_tables(eid, n_tok):
    a = n_tok * TOP_K
    bm = MOE_BLOCK
    pad = ROW_GROUP - 1
    flat_e = eid.reshape(a)
    experts = jnp.arange(N_EXPERTS, dtype=jnp.int32)
    counts = jnp.sum(flat_e[:, None] == experts[None, :], axis=0, dtype=jnp.int32)
    nblk_e = (counts + bm - 1) // bm
    blk_end = jnp.cumsum(nblk_e)
    blk_first = blk_end - nblk_e
    order = jnp.argsort(jnp.concatenate([flat_e, jnp.repeat(experts, pad)])).astype(jnp.int32)
    seg = counts + pad
    seg_start = jnp.cumsum(seg) - seg
    src_rows = jnp.where(order < a, lax.shift_right_logical(order, 1), 0) * SUBLANES
    dst_rows = order * SUBLANES
    n_blocks = -(-a // bm) + N_EXPERTS
    blk = jnp.arange(n_blocks + 1, dtype=jnp.int32)
    blk_e = jnp.minimum(jnp.sum(blk_end[None, :] <= blk[:, None], axis=1, dtype=jnp.int32), N_EXPERTS - 1)
    within = (blk - blk_first[blk_e]) * bm
    n_used = blk_end[-1].reshape(1)
    blk_start = jnp.clip(seg_start[blk_e] + within, 0, order.shape[0] - 1)
    blk_cnt = jnp.where(blk < n_used[0], jnp.clip(counts[blk_e] - within, 0, bm), 0)
    return src_rows, dst_rows, blk_e, blk_start, blk_cnt, n_used


def _moe(layer, h2t, route, wg, wu, wd):
    n_tok = route.shape[0]
    eid = route[:, 2:4].astype(jnp.int32)
    src_rows, dst_rows, blk_e, blk_start, blk_cnt, n_used = _dispatch_tables(eid, n_tok)
    return _expert_ffn(layer, blk_e, n_used, blk_start, blk_cnt, src_rows, dst_rows, h2t, wg, wu, wd, n_tok)


def kernel(x, c, ctx, c_ctx, w_ada, b_ada, norm1_g, norm2_g, w_in, s5_lam_re, s5_lam_im, s5_log_dt, s5_b_re, s5_b_im, s5_c_re, s5_c_im, s5_d, s5_w_glu, hgrn_lb_raw, w_branch_s5, w_branch_hgrn, w_branch_ret, w_out, moe_w_group, moe_b_group, moe_w_expert, moe_b_expert, moe_w_gate, moe_w_up, moe_w_down, final_norm_g):
    bsz, n_lat, d = x.shape
    n_ctx = ctx.shape[1]
    t_len = n_ctx + n_lat
    depth = w_ada.shape[0]
    m = bsz * t_len
    w_s5 = s5_d.shape[1]
    w_hg = w_branch_hgrn.shape[1]
    w_ret = w_branch_ret.shape[1]
    n_heads = w_ret // HEAD_DIM

    pad_rows = (-(bsz + 1)) % SUBLANES
    cc = jnp.concatenate([c, c_ctx[None, :], jnp.zeros((pad_rows, d), F32)], axis=0)
    mods = _ada_tables(cc, w_ada, b_ada).reshape(depth, bsz + 1 + pad_rows, N_MOD, d)

    p_lb = jax.nn.softmax(hgrn_lb_raw.astype(F32), axis=0)
    lower_bounds = jnp.cumsum(p_lb, axis=0) - p_lb[0:1]
    ret_consts = _ret_constants(n_heads, t_len, n_ctx)
    ones_bd = jnp.kron(jnp.eye(w_hg // HEAD_DIM, dtype=F32), jnp.ones((HEAD_DIM, HEAD_DIM), F32)).astype(BF16)

    xs = jnp.concatenate([ctx, x], axis=1).reshape(m, d)
    off_hg = w_s5
    off_ret = off_hg + 5 * w_hg
    off_gz = off_ret + 4 * w_ret
    w_in_bf = w_in.astype(BF16)
    s5w = jax.vmap(_s5_weights)(s5_lam_re, s5_lam_im, s5_log_dt, s5_b_re, s5_b_im, s5_c_re, s5_c_im, s5_d)
    for l in range(depth):
        mod_l = mods[l, :bsz]
        mod_c = mods[l, bsz:bsz + 1]
        u, hg, ret, gz = _project(xs, norm1_g[l], mod_l, mod_c, w_in_bf, l,
                                  (0, off_hg, off_ret, off_gz, w_in.shape[2]), t_len, n_ctx)

        ys5 = _s5_mix(u.reshape(bsz, t_len, w_s5), *(w[l] for w in s5w), n_ctx).reshape(m, w_s5)
        hg_of, hg_ob = _hg_mix(hg.reshape(bsz, t_len, 5 * w_hg), lower_bounds[l], n_ctx)
        ret_of, ret_ob = _ret_mix(ret.reshape(bsz, t_len, 4 * w_ret), ret_consts, n_ctx)

        wr = jnp.concatenate([moe_w_group[l], moe_w_expert[l],
                              jnp.zeros((d, LANES - N_GROUPS - N_EXPERTS), F32)], axis=1)
        br = jnp.concatenate([moe_b_group[l], moe_b_expert[l],
                              jnp.zeros((LANES - N_GROUPS - N_EXPERTS,), F32)])[None, :]
        wr_hi = wr.astype(BF16)
        wr = jnp.concatenate([wr_hi, (wr - wr_hi.astype(F32)).astype(BF16)], axis=1)
        wts = (s5_w_glu[l].astype(BF16), w_branch_s5[l].astype(BF16), w_branch_hgrn[l].astype(BF16),
               w_branch_ret[l].astype(BF16), w_out[l].astype(BF16), ones_bd, wr, br)
        xs, h2, route = _merge(xs, ys5, hg_of.reshape(m, w_hg), hg_ob.reshape(m, w_hg), hg,
                               ret_of.reshape(m, w_ret), ret_ob.reshape(m, w_ret), ret, gz,
                               mod_l, mod_c, norm2_g[l].reshape(1, d), wts, t_len, n_ctx)
        y2 = _moe(jnp.full((1,), l, jnp.int32), h2, route, moe_w_gate, moe_w_up, moe_w_down)
        xs = _combine(xs, y2, route, mod_l, mod_c, t_len, n_ctx)
    return _final_norm(xs.reshape(bsz, t_len, d), final_norm_g, n_ctx)
```

```python
import functools

import jax
import jax.numpy as jnp
import numpy as np
from jax import lax
from jax.experimental import pallas as pl
from jax.experimental.pallas import tpu as pltpu

F32 = jnp.float32
BF16 = jnp.bfloat16
HIGHEST = lax.Precision.HIGHEST

LANES = 128
SUBLANES = 8
VMEM_LIMIT = 56 * 1024 * 1024

EPS = 1e-6
N_MOD = 6
GRID_W = 64
HEAD_DIM = 64
PAIR = 2 * HEAD_DIM
S5_GROUP = 16
S5_STATE = 64
S5_CHUNK = 16
HG_CHUNK = 64
HG_HALF = HG_CHUNK // 2
HG_BLOCK = 256
RET_BLOCK = 256
ROPE_BASE = 10000.0
N_GROUPS = 4
EXPERTS_PER_GROUP = 8
N_EXPERTS = N_GROUPS * EXPERTS_PER_GROUP
TOP_K = 2
MOE_BLOCK = 512
ROW_GROUP = 8


def _cparams(*sem):
    return pltpu.CompilerParams(dimension_semantics=sem, vmem_limit_bytes=VMEM_LIMIT)


def _row_tile(t_len):
    for tm in (544, 512, 384, 272, 256, 128, 64, 32, 16, 8):
        if t_len % tm == 0:
            return tm
    raise ValueError(f"unsupported stream length {t_len}")


def _ada_kernel(c_ref, w_ref, b_ref, o_ref):
    a = jax.nn.silu(c_ref[...]).astype(BF16)
    o_ref[0] = jnp.dot(a, w_ref[0].astype(BF16), preferred_element_type=F32) + b_ref[0]


def _ada_tables(cc, w_ada, b_ada):
    depth, d, n = w_ada.shape
    rows = cc.shape[0]
    tn = 1536 if n % 1536 == 0 else n
    return pl.pallas_call(
        _ada_kernel,
        grid=(depth, n // tn),
        in_specs=[
            pl.BlockSpec((rows, d), lambda l, j: (0, 0)),
            pl.BlockSpec((1, d, tn), lambda l, j: (l, 0, j)),
            pl.BlockSpec((1, 1, tn), lambda l, j: (l, 0, j)),
        ],
        out_specs=pl.BlockSpec((1, rows, tn), lambda l, j: (l, 0, j)),
        out_shape=jax.ShapeDtypeStruct((depth, rows, n), F32),
        compiler_params=_cparams("parallel", "parallel"),
    )(cc, w_ada, b_ada.reshape(depth, 1, n))


def _mod_rows(ml_ref, mc_ref, idx, is_ctx):
    return jnp.where(is_ctx, mc_ref[0, idx:idx + 1, :], ml_ref[0, idx:idx + 1, :])


def _is_ctx_rows(tm, tiles_per_b, n_ctx):
    jt = pl.program_id(0) % tiles_per_b
    row = jt * tm + lax.broadcasted_iota(jnp.int32, (tm, 1), 0)
    return row < n_ctx


def _store_token_tiles(ref, val):
    n_rows = val.shape[0]
    for k in range(val.shape[1] // LANES):
        ref[pl.ds(k, n_rows, stride=SUBLANES), :] = val[:, LANES * k:LANES * (k + 1)]


def _load_token_tiles(ref):
    n_rows = ref.shape[0] // SUBLANES
    return jnp.concatenate([ref[pl.ds(k, n_rows, stride=SUBLANES), :] for k in range(SUBLANES)], axis=1)


def _rmsnorm_rows(x, g):
    return x * lax.rsqrt(jnp.mean(x * x, axis=-1, keepdims=True) + EPS) * g


def _proj_kernel(x_ref, g_ref, ml_ref, mc_ref, w_ref, *out_refs, tm, tiles_per_b, n_ctx, offsets):
    is_ctx = _is_ctx_rows(tm, tiles_per_b, n_ctx)
    y = _rmsnorm_rows(x_ref[...], g_ref[...])
    y = y * (1.0 + _mod_rows(ml_ref, mc_ref, 1, is_ctx)) + _mod_rows(ml_ref, mc_ref, 0, is_ctx)
    h = y.astype(BF16)
    for k, o_ref in enumerate(out_refs):
        w = w_ref[0, :, offsets[k]:offsets[k + 1]]
        o_ref[...] = jnp.dot(h, w, preferred_element_type=F32).astype(o_ref.dtype)


def _project(x2, g, mod_l, mod_c, w_all, layer, offsets, t_len, n_ctx):
    m, d = x2.shape
    tm = _row_tile(t_len)
    tpb = t_len // tm
    kern = functools.partial(_proj_kernel, tm=tm, tiles_per_b=tpb, n_ctx=n_ctx, offsets=offsets)
    widths = [offsets[k + 1] - offsets[k] for k in range(len(offsets) - 1)]
    out_dtypes = (F32, F32, BF16, BF16)
    return pl.pallas_call(
        kern,
        grid=(m // tm,),
        in_specs=[
            pl.BlockSpec((tm, d), lambda i: (i, 0)),
            pl.BlockSpec((1, d), lambda i: (0, 0)),
            pl.BlockSpec((1, N_MOD, d), lambda i: (i // tpb, 0, 0)),
            pl.BlockSpec((1, N_MOD, d), lambda i: (0, 0, 0)),
            pl.BlockSpec((1,) + w_all.shape[1:], lambda i: (layer, 0, 0), pipeline_mode=pl.Buffered(1)),
        ],
        out_specs=[pl.BlockSpec((tm, w), lambda i: (i, 0)) for w in widths],
        out_shape=[jax.ShapeDtypeStruct((m, w), dt) for w, dt in zip(widths, out_dtypes)],
        compiler_params=_cparams("parallel"),
    )(x2, g.reshape(1, d), mod_l, mod_c, w_all)


def _s5_weights(lam_re, lam_im, log_dt, b_re, b_im, c_re, c_im, d_skip):
    n_dir, g_n, p_n = lam_re.shape
    h_n = b_re.shape[-1]
    cn = S5_CHUNK
    lr, li = lam_re.astype(F32), lam_im.astype(F32)
    dt = jnp.exp(log_dt.astype(F32))[..., None]
    mag = jnp.exp(lr * dt)
    ar = mag * jnp.cos(li * dt)
    ai = mag * jnp.sin(li * dt)
    den = lr * lr + li * li
    zr = ((ar - 1.0) * lr + ai * li) / den
    zi = (ai * lr - (ar - 1.0) * li) / den
    bbr = zr[..., None] * b_re - zi[..., None] * b_im
    bbi = zr[..., None] * b_im + zi[..., None] * b_re
    j = jnp.arange(cn + 1, dtype=F32)[:, None, None, None]
    pmag = jnp.exp(lr[None] * dt[None] * j)
    pang = li[None] * dt[None] * j
    pr = pmag * jnp.cos(pang)
    pi = pmag * jnp.sin(pang)
    wr = pr[..., None] * bbr[None] - pi[..., None] * bbi[None]
    wi = pr[..., None] * bbi[None] + pi[..., None] * bbr[None]
    kj = (jnp.einsum('dghp,jdgpk->jdghk', c_re, wr, precision=HIGHEST)
          - jnp.einsum('dghp,jdgpk->jdghk', c_im, wi, precision=HIGHEST))
    s_idx = jnp.arange(cn)[:, None]
    t_idx = jnp.arange(cn)[None, :]
    lag = t_idx - s_idx
    kf = kj[:cn, 0][jnp.clip(lag, 0, cn - 1)]
    kb = kj[:cn, 1][jnp.clip(-lag, 0, cn - 1)]
    kf = jnp.where((lag >= 0)[:, :, None, None, None], kf, 0.0)
    kb = jnp.where((lag <= 0)[:, :, None, None, None], kb, 0.0)
    dsk = d_skip.astype(F32).reshape(g_n, h_n)
    eye_h = jnp.eye(h_n, dtype=F32)
    diag = (lag == 0)[:, :, None, None, None] * (dsk[None, None, :, :, None] * eye_h[None, None, None])
    m_full = kf + kb + diag
    m_mat = m_full.transpose(2, 0, 4, 1, 3).reshape(g_n, cn * h_n, cn * h_n)
    wf_r = wr[:cn, 0][::-1]
    wf_i = wi[:cn, 0][::-1]
    wb_r = wr[:cn, 1]
    wb_i = wi[:cn, 1]
    w_in = jnp.stack([wf_r, wf_i, wb_r, wb_i], axis=0)
    w_in = w_in.transpose(2, 1, 4, 0, 3).reshape(g_n, cn * h_n, 4 * p_n)
    def out_map(d, powers):
        prd, pid = pr[powers, d], pi[powers, d]
        w_re = c_re[d][None] * prd[:, :, None, :] - c_im[d][None] * pid[:, :, None, :]
        w_im = c_re[d][None] * pid[:, :, None, :] + c_im[d][None] * prd[:, :, None, :]
        return w_re, -w_im
    t_arr = jnp.arange(cn)
    of_r, of_i = out_map(0, t_arr + 1)
    ob_r, ob_i = out_map(1, cn - t_arr)
    w_out = jnp.stack([of_r, of_i, ob_r, ob_i], axis=0)
    w_out = w_out.transpose(2, 0, 4, 1, 3).reshape(g_n, 4 * p_n, cn * h_n)
    slot = jax.nn.one_hot(jnp.arange(g_n) % 2, 2, dtype=F32)
    w_in = w_in.reshape(g_n, cn * h_n, 4, 1, p_n) * slot[:, None, None, :, None]
    w_in = w_in.reshape(g_n, cn * h_n, 8 * p_n)
    w_out = w_out.reshape(g_n, 4, 1, p_n, cn * h_n) * slot[:, None, :, None, None]
    w_out = w_out.reshape(g_n, 8 * p_n, cn * h_n)
    a_mat = jnp.stack([pr[cn, 0].reshape(-1), pi[cn, 0].reshape(-1),
                       pr[cn, 1].reshape(-1), pi[cn, 1].reshape(-1)], axis=0)
    w1 = jnp.concatenate([m_mat, w_in], axis=-1).astype(BF16)
    return w1, w_out.astype(BF16), a_mat


def _s5_fold_perm():
    j, g8, h = np.meshgrid(np.arange(8), np.arange(8), np.arange(S5_GROUP), indexing='ij')
    src = (j * 8 + g8) * S5_GROUP + h
    dst = (g8 * 8 + j) * S5_GROUP + h
    perm = np.zeros((8 * LANES, 8 * LANES), np.float32)
    perm[src.reshape(-1), dst.reshape(-1)] = 1.0
    return jnp.asarray(perm, dtype=BF16)


def _s5_kernel(u_ref, perm_ref, w1_ref, wo_ref, a_ref, y_ref, ug_scr, yg_scr, xfr, xfi, xbr, xbi,
               *, n_chunks, n_ctx_chunks, n_groups):
    cn = S5_CHUNK
    blk = cn * S5_GROUP
    tok_per_slab = LANES // S5_GROUP
    for c in range(cn // tok_per_slab):
        slabs = [u_ref[0, pl.ds(tok_per_slab * c + j, n_chunks, stride=cn), :] for j in range(tok_per_slab)]
        s = jnp.concatenate(slabs, axis=1).astype(BF16)
        up = jnp.dot(s, perm_ref[...], preferred_element_type=F32).astype(BF16)
        for g in range(n_groups):
            ug_scr[g, :, LANES * c:LANES * (c + 1)] = up[:, LANES * g:LANES * (g + 1)]
    xs = (xfr, xfi, xbr, xbi)
    for g in range(n_groups):
        z = jnp.dot(ug_scr[g], w1_ref[g], preferred_element_type=F32)
        yg_scr[g] = z[:, :blk]
        lanes = slice(LANES * (g // 2), LANES * (g // 2 + 1))
        for k, xr in enumerate(xs):
            piece = z[:, blk + LANES * k:blk + LANES * (k + 1)]
            if g % 2 == 0:
                xr[:, lanes] = piece
            else:
                xr[:, lanes] += piece
    arf, aif, arb, aib = a_ref[0:1, :], a_ref[1:2, :], a_ref[2:3, :], a_ref[3:4, :]

    def body(i, carry):
        fr, fi, br, bi = carry
        nb = jnp.where(i < n_ctx_chunks, n_ctx_chunks - 1 - i, n_chunks - 1 - (i - n_ctx_chunks))
        x_fr, x_fi = xfr[pl.ds(i, 1), :], xfi[pl.ds(i, 1), :]
        x_br, x_bi = xbr[pl.ds(nb, 1), :], xbi[pl.ds(nb, 1), :]
        xfr[pl.ds(i, 1), :] = fr
        xfi[pl.ds(i, 1), :] = fi
        xbr[pl.ds(nb, 1), :] = br
        xbi[pl.ds(nb, 1), :] = bi
        return (arf * fr - aif * fi + x_fr, arf * fi + aif * fr + x_fi,
                arb * br - aib * bi + x_br, arb * bi + aib * br + x_bi)

    zero = jnp.zeros((1, xfr.shape[1]), F32)
    lax.fori_loop(0, n_chunks, body, (zero, zero, zero, zero))
    for g in range(n_groups):
        lanes = slice(LANES * (g // 2), LANES * (g // 2 + 1))
        hp = jnp.concatenate([xr[:, lanes] for xr in xs], axis=1).astype(BF16)
        yg_scr[g] += jnp.dot(hp, wo_ref[g], preferred_element_type=F32)
    for c in range(cn // tok_per_slab):
        ycat = jnp.concatenate([yg_scr[g, :, LANES * c:LANES * (c + 1)] for g in range(n_groups)], axis=1)
        hi = ycat.astype(BF16)
        lo = (ycat - hi.astype(F32)).astype(BF16)
        r = _dot_nt(hi, perm_ref[...]) + _dot_nt(lo, perm_ref[...])
        for j in range(tok_per_slab):
            y_ref[0, pl.ds(tok_per_slab * c + j, n_chunks, stride=cn), :] = r[:, LANES * j:LANES * (j + 1)]


def _s5_mix(u, w1, w_out, a_mat, n_ctx):
    bsz, t_len, width = u.shape
    g_n = width // S5_GROUP
    cn = S5_CHUNK
    n_chunks = t_len // cn
    blk = cn * S5_GROUP
    perm = _s5_fold_perm()
    gs = LANES // S5_GROUP
    state_w = gs * S5_STATE
    kern = functools.partial(_s5_kernel, n_chunks=n_chunks, n_ctx_chunks=n_ctx // cn, n_groups=gs)
    return pl.pallas_call(
        kern,
        grid=(bsz, g_n // gs),
        in_specs=[pl.BlockSpec((1, t_len, LANES), lambda b, s: (b, 0, s)),
                  pl.BlockSpec(perm.shape, lambda b, s: (0, 0), pipeline_mode=pl.Buffered(1)),
                  pl.BlockSpec((gs,) + w1.shape[1:], lambda b, s: (s, 0, 0)),
                  pl.BlockSpec((gs,) + w_out.shape[1:], lambda b, s: (s, 0, 0)),
                  pl.BlockSpec((4, state_w), lambda b, s: (0, s))],
        out_specs=pl.BlockSpec((1, t_len, LANES), lambda b, s: (b, 0, s)),
        out_shape=jax.ShapeDtypeStruct((bsz, t_len, width), F32),
        scratch_shapes=[pltpu.VMEM((gs, n_chunks, blk), BF16), pltpu.VMEM((gs, n_chunks, blk), F32)]
        + [pltpu.VMEM((n_chunks, state_w), F32)] * 4,
        compiler_params=_cparams("parallel", "parallel"),
    )(u, perm, w1, w_out, a_mat)


def _head0_lanes(shape):
    return lax.broadcasted_iota(jnp.int32, shape, len(shape) - 1) < HEAD_DIM


def _stack_heads(q):
    m0 = _head0_lanes(q.shape)
    return jnp.concatenate([jnp.where(m0, q, 0.0), jnp.where(m0, 0.0, q)], axis=-2)


def _unstack_heads(o2):
    c = o2.shape[-2] // 2
    top, bottom = o2[..., :c, :], o2[..., c:, :]
    return jnp.where(_head0_lanes(top.shape), top, bottom)


def _pair_block_mask():
    r = lax.broadcasted_iota(jnp.int32, (PAIR, PAIR), 0) < HEAD_DIM
    c = lax.broadcasted_iota(jnp.int32, (PAIR, PAIR), 1) < HEAD_DIM
    return r == c


def _dot_nt(a, b):
    return lax.dot_general(a, b, (((1,), (1,)), ((), ())), preferred_element_type=F32)


def _dot_tn(a, b):
    return lax.dot_general(a, b, (((0,), (0,)), ((), ())), preferred_element_type=F32)


def _bwd_block(j, n_ctx_blocks, n_blocks):
    return jnp.where(j < n_ctx_blocks, n_ctx_blocks - 1 - j, n_blocks - 1 - (j - n_ctx_blocks))


def _bdot_nt(a, b):
    return lax.dot_general(a, b, (((2,), (2,)), ((0,), (0,))), preferred_element_type=F32)


def _bdot_nn(a, b):
    return lax.dot_general(a, b, (((2,), (1,)), ((0,), (0,))), preferred_element_type=F32)


def _hg_block(q, z, v, lb, ht, reverse):
    rows = q.shape[0]
    c = HG_CHUNK
    hh = HG_HALF
    n = rows // c
    f = lb + (1.0 - lb) * jax.nn.sigmoid(z)
    logf = jnp.log(f)
    kk = 1.0 - f
    ri = lax.broadcasted_iota(jnp.int32, (n, c, c), 1)
    ci = lax.broadcasted_iota(jnp.int32, (n, c, c), 2)
    tri = jnp.where((ci >= ri) if reverse else (ci <= ri), 1.0, 0.0).astype(BF16)
    l1 = logf.astype(BF16)
    r1 = logf - l1.astype(F32)
    l2 = r1.astype(BF16)
    l3 = (r1 - l2.astype(F32)).astype(BF16)
    b3 = _bdot_nn(tri, jnp.concatenate([l1, l2, l3], axis=1).reshape(n, c, 3 * PAIR))
    b4 = b3[..., 0:PAIR] + b3[..., PAIR:2 * PAIR] + b3[..., 2 * PAIR:3 * PAIR]
    q4, k4 = q.reshape(n, c, PAIR), kk.reshape(n, c, PAIR)
    vb = v.astype(BF16).reshape(n, c, PAIR)
    if reverse:
        first, second = slice(hh, c), slice(0, hh)
        r = b4[:, hh:hh + 1]
        bend = b4[:, 0:1]
    else:
        first, second = slice(0, hh), slice(hh, c)
        r = b4[:, hh - 1:hh]
        bend = b4[:, c - 1:c]
    b1, b2 = b4[:, first], b4[:, second]
    qd1 = q4[:, first] * jnp.exp(b1)
    qd2 = q4[:, second] * jnp.exp(b2 - r)
    k1d = k4[:, first] * jnp.exp(-b1)
    k2 = k4 * jnp.exp(r - b4)
    s1 = _bdot_nt(_stack_heads(qd1).astype(BF16), k1d.astype(BF16))
    s2 = _bdot_nt(_stack_heads(qd2).astype(BF16), k2.astype(BF16))
    t1 = lax.broadcasted_iota(jnp.int32, (n, 2 * hh, hh), 1) % hh
    c1 = lax.broadcasted_iota(jnp.int32, (n, 2 * hh, hh), 2)
    t2 = lax.broadcasted_iota(jnp.int32, (n, 2 * hh, c), 1) % hh
    c2 = lax.broadcasted_iota(jnp.int32, (n, 2 * hh, c), 2)
    if reverse:
        s1 = jnp.where(c1 >= t1, s1, 0.0)
        s2 = jnp.where(c2 >= t2, s2, 0.0)
    else:
        s1 = jnp.where(c1 <= t1, s1, 0.0)
        s2 = jnp.where(c2 <= t2 + hh, s2, 0.0)
    o1 = _unstack_heads(_bdot_nn(s1.astype(BF16), vb[:, first]))
    o2 = _unstack_heads(_bdot_nn(s2.astype(BF16), vb))
    o = jnp.concatenate([o2, o1] if reverse else [o1, o2], axis=1)
    ke = (k4 * jnp.exp(bend - b4)).astype(BF16)
    dec = jnp.exp(bend)
    bmask = _pair_block_mask()
    states = [None] * n
    for ch in (range(n - 1, -1, -1) if reverse else range(n)):
        states[ch] = ht.astype(BF16)
        ht = ht * dec[ch] + jnp.where(bmask, _dot_tn(vb[ch], ke[ch]), 0.0)
    o = o + _bdot_nt((q4 * jnp.exp(b4)).astype(BF16), jnp.stack(states, axis=0))
    return o.reshape(rows, PAIR), ht


def _hg_kernel(qf_ref, zf_ref, vf_ref, qb_ref, zb_ref, vb_ref, lb_ref, of_ref, ob_ref, hf_scr, hb_scr, *, n_pairs):
    @pl.when(pl.program_id(1) == 0)
    def _():
        hf_scr[...] = jnp.zeros_like(hf_scr)
        hb_scr[...] = jnp.zeros_like(hb_scr)

    for p in range(n_pairs):
        sl = slice(p * PAIR, (p + 1) * PAIR)
        o, hn = _hg_block(qf_ref[0, :, sl], zf_ref[0, :, sl], vf_ref[0, :, sl], lb_ref[0:1, sl], hf_scr[p], False)
        of_ref[0, :, sl] = o
        hf_scr[p] = hn
        o, hn = _hg_block(qb_ref[0, :, sl], zb_ref[0, :, sl], vb_ref[0, :, sl], lb_ref[1:2, sl], hb_scr[p], True)
        ob_ref[0, :, sl] = o
        hb_scr[p] = hn


def _hg_mix(hg, lb, n_ctx):
    bsz, t_len, total = hg.shape
    w = total // 5
    n_pairs = w // PAIR
    c = HG_BLOCK
    nc, ncc = t_len // c, n_ctx // c
    bw = functools.partial(_bwd_block, n_ctx_blocks=ncc, n_blocks=nc)

    def fspec(col):
        return pl.BlockSpec((1, c, w), lambda b, j: (b, j, col))

    def bspec(col):
        return pl.BlockSpec((1, c, w), lambda b, j: (b, bw(j), col))

    kern = functools.partial(_hg_kernel, n_pairs=n_pairs)
    return pl.pallas_call(
        kern,
        grid=(bsz, nc),
        in_specs=[fspec(0), fspec(1), fspec(3), bspec(0), bspec(2), bspec(3),
                  pl.BlockSpec((2, w), lambda b, j: (0, 0))],
        out_specs=[pl.BlockSpec((1, c, w), lambda b, j: (b, j, 0)),
                   pl.BlockSpec((1, c, w), lambda b, j: (b, bw(j), 0))],
        out_shape=[jax.ShapeDtypeStruct((bsz, t_len, w), F32)] * 2,
        scratch_shapes=[pltpu.VMEM((n_pairs, PAIR, PAIR), F32)] * 2,
        compiler_params=_cparams("parallel", "arbitrary"),
    )(hg, hg, hg, hg, hg, hg, lb)


def _rope_rows(x, cos, sin):
    hd2 = HEAD_DIM // 2
    lane = lax.broadcasted_iota(jnp.int32, x.shape, 1) % HEAD_DIM
    swapped = jnp.where(lane < hd2, pltpu.roll(x, PAIR - hd2, 1), pltpu.roll(x, hd2, 1))
    return x * cos + swapped * sin


def _ret_kernel(qf_ref, kf_ref, vf_ref, cf_ref, sf_ref, qb_ref, kb_ref, vb_ref, cb_ref, sb_ref,
                d_ref, gqf_ref, gkf_ref, gqb_ref, gkb_ref, gh_ref, of_ref, ob_ref, hf_scr, hb_scr, *, n_pairs):
    @pl.when(pl.program_id(1) == 0)
    def _():
        hf_scr[...] = jnp.zeros_like(hf_scr)
        hb_scr[...] = jnp.zeros_like(hb_scr)

    kscale = HEAD_DIM ** -0.5
    bmask = _pair_block_mask()
    for p in range(n_pairs):
        sl = slice(p * PAIR, (p + 1) * PAIR)
        q = _rope_rows(qf_ref[0, :, sl].astype(F32), cf_ref[...], sf_ref[...])
        k = _rope_rows(kf_ref[0, :, sl].astype(F32) * kscale, cf_ref[...], sf_ref[...])
        vb16 = vf_ref[0, :, sl].astype(BF16)
        s = _dot_nt(_stack_heads(q).astype(BF16), k.astype(BF16)) * d_ref[p]
        o = _unstack_heads(jnp.dot(s.astype(BF16), vb16, preferred_element_type=F32))
        ht = hf_scr[p]
        o = o + _dot_nt((q * gqf_ref[:, sl]).astype(BF16), ht.astype(BF16))
        upd = _dot_tn(vb16, (k * gkf_ref[:, sl]).astype(BF16))
        hf_scr[p] = ht * gh_ref[p] + jnp.where(bmask, upd, 0.0)
        of_ref[0, :, sl] = o

        q = _rope_rows(qb_ref[0, :, sl].astype(F32), cb_ref[...], sb_ref[...])
        k = _rope_rows(kb_ref[0, :, sl].astype(F32) * kscale, cb_ref[...], sb_ref[...])
        vb16 = vb_ref[0, :, sl].astype(BF16)
        ht = hb_scr[p]
        ob_ref[0, :, sl] = _dot_nt((q * gqb_ref[:, sl]).astype(BF16), ht.astype(BF16))
        upd = _dot_tn(vb16, (k * gkb_ref[:, sl]).astype(BF16))
        hb_scr[p] = ht * gh_ref[p] + jnp.where(bmask, upd, 0.0)


def _ret_constants(n_heads, t_len, n_ctx):
    c = RET_BLOCK
    log_gamma = jnp.log(1.0 - 2.0 ** (-5.0 - jnp.arange(n_heads, dtype=F32)))
    t = jnp.arange(c, dtype=F32)
    dist = jnp.abs(t[:, None] - t[None, :])
    dmat = jnp.exp(log_gamma[:, None, None] * dist[None])
    dmat = dmat.reshape(n_heads // 2, 2 * c, c)
    lg_lane = jnp.repeat(log_gamma, HEAD_DIM)[None, :]
    gqf = jnp.exp(lg_lane * (t[:, None] + 1.0))
    gkf = jnp.exp(lg_lane * (c - 1.0 - t[:, None]))
    gqb = jnp.exp(lg_lane * (c - t[:, None]))
    gkb = jnp.exp(lg_lane * t[:, None])
    gh = jnp.exp(lg_lane * float(c)).reshape(n_heads // 2, 1, PAIR)
    gh = jnp.broadcast_to(gh, (n_heads // 2, PAIR, PAIR))
    n_lat = t_len - n_ctx
    rows = n_lat // GRID_W
    row = jnp.repeat(jnp.arange(rows, dtype=F32), GRID_W)
    col = jnp.broadcast_to(jnp.arange(GRID_W, dtype=F32)[None, :], (rows, GRID_W)).reshape(-1)
    n_freq = HEAD_DIM // 4
    inv_freq = ROPE_BASE ** (-jnp.arange(n_freq, dtype=F32) / n_freq)
    ang = jnp.concatenate([row[:, None] * inv_freq, col[:, None] * inv_freq], axis=-1)
    cos_l, sin_l = jnp.cos(ang), jnp.sin(ang)
    cos_h = jnp.concatenate([cos_l, cos_l], axis=-1)
    sin_h = jnp.concatenate([-sin_l, sin_l], axis=-1)
    cos_t = jnp.concatenate([jnp.ones((n_ctx, HEAD_DIM), F32), cos_h], axis=0)
    sin_t = jnp.concatenate([jnp.zeros((n_ctx, HEAD_DIM), F32), sin_h], axis=0)
    cos_t = jnp.concatenate([cos_t, cos_t], axis=-1)
    sin_t = jnp.concatenate([sin_t, sin_t], axis=-1)
    return dmat, gqf, gkf, gqb, gkb, gh, cos_t, sin_t


def _ret_mix(ret, consts, n_ctx):
    dmat, gqf, gkf, gqb, gkb, gh, cos_t, sin_t = consts
    bsz, t_len, total = ret.shape
    w = total // 4
    n_pairs = w // PAIR
    c = RET_BLOCK
    nb, ncb = t_len // c, n_ctx // c
    bw = functools.partial(_bwd_block, n_ctx_blocks=ncb, n_blocks=nb)

    def fspec(col):
        return pl.BlockSpec((1, c, w), lambda b, j: (b, j, col))

    def bspec(col):
        return pl.BlockSpec((1, c, w), lambda b, j: (b, bw(j), col))

    def const(a):
        nd = a.ndim
        return pl.BlockSpec(a.shape, lambda b, j: (0,) * nd)

    rope_f = pl.BlockSpec((c, PAIR), lambda b, j: (j, 0))
    rope_b = pl.BlockSpec((c, PAIR), lambda b, j: (bw(j), 0))
    kern = functools.partial(_ret_kernel, n_pairs=n_pairs)
    return pl.pallas_call(
        kern,
        grid=(bsz, nb),
        in_specs=[fspec(0), fspec(1), fspec(2), rope_f, rope_f, bspec(0), bspec(1), bspec(2), rope_b, rope_b,
                  const(dmat), const(gqf), const(gkf), const(gqb), const(gkb), const(gh)],
        out_specs=[pl.BlockSpec((1, c, w), lambda b, j: (b, j, 0)),
                   pl.BlockSpec((1, c, w), lambda b, j: (b, bw(j), 0))],
        out_shape=[jax.ShapeDtypeStruct((bsz, t_len, w), F32)] * 2,
        scratch_shapes=[pltpu.VMEM((n_pairs, PAIR, PAIR), F32)] * 2,
        compiler_params=_cparams("parallel", "arbitrary"),
    )(ret, ret, ret, cos_t, sin_t, ret, ret, ret, cos_t, sin_t, dmat, gqf, gkf, gqb, gkb, gh)


def _head_norm_gate(o, g, ones_bd):
    sq = o * o
    hi = sq.astype(BF16)
    lo = (sq - hi.astype(F32)).astype(BF16)
    ss = jnp.dot(hi, ones_bd, preferred_element_type=F32) + jnp.dot(lo, ones_bd, preferred_element_type=F32)
    return o * lax.rsqrt(ss * (1.0 / HEAD_DIM) + EPS) * jax.nn.silu(g)


def _route(lg):
    lane = lax.broadcasted_iota(jnp.int32, lg.shape, 1).astype(F32)
    big = float(1 << 20)
    neg = -jnp.inf
    gmask = lane < N_GROUPS
    lgm = jnp.where(gmask, lg, neg)
    gmax = jnp.max(lgm, axis=1, keepdims=True)
    gsum = jnp.sum(jnp.where(gmask, jnp.exp(lgm - gmax), 0.0), axis=1, keepdims=True)
    gp = 1.0 / gsum
    gi = jnp.min(jnp.where(gmask & (lgm == gmax), lane, big), axis=1, keepdims=True)
    lo_lane = N_GROUPS + EXPERTS_PER_GROUP * gi
    emask = (lane >= lo_lane) & (lane < lo_lane + EXPERTS_PER_GROUP)
    l1 = jnp.max(jnp.where(emask, lg, neg), axis=1, keepdims=True)
    i1 = jnp.min(jnp.where(emask & (lg == l1), lane, big), axis=1, keepdims=True)
    mask2 = emask & (lane != i1)
    l2 = jnp.max(jnp.where(mask2, lg, neg), axis=1, keepdims=True)
    i2 = jnp.min(jnp.where(mask2 & (lg == l2), lane, big), axis=1, keepdims=True)
    e21 = jnp.exp(l2 - l1)
    w1 = gp / (1.0 + e21)
    w2 = gp * e21 / (1.0 + e21)
    e1 = i1 - N_GROUPS
    e2 = i2 - N_GROUPS
    return jnp.where(lane == 0, w1, jnp.where(lane == 1, w2, jnp.where(lane == 2, e1, jnp.where(lane == 3, e2, 0.0))))


def _merge_kernel(x_ref, ys_ref, hof_ref, hob_ref, hgg_ref, rof_ref, rob_ref, rgg_ref, gz_ref,
                  ml_ref, mc_ref, g2_ref, wglu_ref, wbs_ref, wbh_ref, wbr_ref, wout_ref, ones_ref,
                  wr_ref, br_ref, xo_ref, h2_ref, rt_ref, *, tm, tiles_per_b, n_ctx):
    d = x_ref.shape[1]
    is_ctx = _is_ctx_rows(tm, tiles_per_b, n_ctx)
    y = jax.nn.gelu(ys_ref[...], approximate=True)
    y = y * jax.nn.sigmoid(jnp.dot(y.astype(BF16), wglu_ref[...], preferred_element_type=F32))
    t_s5 = jnp.dot(y.astype(BF16), wbs_ref[...], preferred_element_type=F32)
    yh = _head_norm_gate(hof_ref[...] + hob_ref[...], hgg_ref[...], ones_ref[...])
    t_hg = jnp.dot(yh.astype(BF16), wbh_ref[...], preferred_element_type=F32)
    yr = _head_norm_gate(rof_ref[...] + rob_ref[...], rgg_ref[...], ones_ref[...])
    t_ret = jnp.dot(yr.astype(BF16), wbr_ref[...], preferred_element_type=F32)
    merged = (jax.nn.sigmoid(gz_ref[:, 0:d]) * t_s5 + jax.nn.sigmoid(gz_ref[:, d:2 * d]) * t_hg
              + jax.nn.sigmoid(gz_ref[:, 2 * d:3 * d]) * t_ret)
    mix = jnp.dot(merged.astype(BF16), wout_ref[...], preferred_element_type=F32)
    xn = x_ref[...] + _mod_rows(ml_ref, mc_ref, 2, is_ctx) * mix
    xo_ref[...] = xn
    h2 = _rmsnorm_rows(xn, g2_ref[...])
    h2 = h2 * (1.0 + _mod_rows(ml_ref, mc_ref, 4, is_ctx)) + _mod_rows(ml_ref, mc_ref, 3, is_ctx)
    _store_token_tiles(h2_ref, h2)
    h_hi = h2.astype(BF16)
    h_lo = (h2 - h_hi.astype(F32)).astype(BF16)
    p_hi = jnp.dot(h_hi, wr_ref[...], preferred_element_type=F32)
    p_lo = jnp.dot(h_lo, wr_ref[:, 0:LANES], preferred_element_type=F32)
    rt_ref[...] = _route(p_hi[:, 0:LANES] + p_hi[:, LANES:2 * LANES] + p_lo + br_ref[...])


def _merge(x2, ys5, hg_of, hg_ob, hg2, ret_of, ret_ob, ret2, gz, mod_l, mod_c, g2, wts, t_len, n_ctx):
    m, d = x2.shape
    w_s5 = ys5.shape[1]
    w_h = hg_of.shape[1]
    tm = _row_tile(t_len)
    tpb = t_len // tm
    wglu, wbs, wbh, wbr, wout, ones_bd, wr, br = wts

    def rows(width, col=0):
        return pl.BlockSpec((tm, width), lambda i: (i, col))

    def const(a):
        nd = a.ndim
        return pl.BlockSpec(a.shape, lambda i: (0,) * nd)

    kern = functools.partial(_merge_kernel, tm=tm, tiles_per_b=tpb, n_ctx=n_ctx)
    return pl.pallas_call(
        kern,
        grid=(m // tm,),
        in_specs=[rows(d), rows(w_s5), rows(w_h), rows(w_h), rows(w_h, 4), rows(w_h), rows(w_h), rows(w_h, 3),
                  rows(3 * d),
                  pl.BlockSpec((1, N_MOD, d), lambda i: (i // tpb, 0, 0)),
                  pl.BlockSpec((1, N_MOD, d), lambda i: (0, 0, 0)),
                  const(g2), const(wglu), const(wbs), const(wbh), const(wbr), const(wout), const(ones_bd),
                  const(wr), const(br)],
        out_specs=[rows(d), pl.BlockSpec((tm * SUBLANES, LANES), lambda i: (i, 0)), rows(LANES)],
        out_shape=[jax.ShapeDtypeStruct((m, d), F32), jax.ShapeDtypeStruct((m * SUBLANES, LANES), F32),
                   jax.ShapeDtypeStruct((m, LANES), F32)],
        compiler_params=_cparams("parallel"),
    )(x2, ys5, hg_of, hg_ob, hg2, ret_of, ret_ob, ret2, gz, mod_l, mod_c, g2, wglu, wbs, wbh, wbr, wout,
      ones_bd, wr, br)


def _ffn_kernel(lay_ref, be_ref, nu_ref, bs_ref, cnt_ref, src_ref, dst_ref, h2_hbm, wg_ref, wu_ref, wd_ref, y2_hbm,
                wg_scr, wu_scr, wd_scr, xbuf, obuf, gsem, ssem, *, n_spare_rows):
    i = pl.program_id(0)
    nu = nu_ref[0]
    slot = i % 2
    tile = SUBLANES

    def gather_copy(rows_src, rows_dst, sl):
        return pltpu.make_async_copy(h2_hbm.at[rows_src, :], xbuf.at[sl, rows_dst, :], gsem.at[sl])

    def scatter_copy(rows_src, rows_dst, sl):
        return pltpu.make_async_copy(obuf.at[sl, rows_src, :], y2_hbm.at[rows_dst, :], ssem.at[sl])

    def n_groups(blk):
        return lax.shift_right_logical(cnt_ref[blk] + (ROW_GROUP - 1), ROW_GROUP.bit_length() - 1)

    def start_gather(blk, sl):
        base = bs_ref[blk]

        def body(g, carry):
            for j in range(ROW_GROUP):
                r = g * ROW_GROUP + j
                gather_copy(pl.ds(pl.multiple_of(src_ref[base + r], tile), tile),
                            pl.ds(pl.multiple_of(r * tile, tile), tile), sl).start()
            return carry

        lax.fori_loop(0, n_groups(blk), body, 0)

    def wait_gather(blk, sl):
        n = pl.multiple_of(n_groups(blk) * (ROW_GROUP * tile), tile)
        gather_copy(pl.ds(0, n), pl.ds(0, n), sl).wait()

    def start_scatter(blk, sl):
        base = bs_ref[blk]

        def body(g, carry):
            for j in range(ROW_GROUP):
                r = g * ROW_GROUP + j
                scatter_copy(pl.ds(pl.multiple_of(r * tile, tile), tile),
                             pl.ds(pl.multiple_of(dst_ref[base + r], tile), tile), sl).start()
            return carry

        lax.fori_loop(0, n_groups(blk), body, 0)

    def wait_scatter(blk, sl):
        n = pl.multiple_of(n_groups(blk) * (ROW_GROUP * tile), tile)
        scatter_copy(pl.ds(0, n), pl.ds(0, n), sl).wait()

    @pl.when(i == 0)
    def _():
        xbuf[...] = jnp.zeros_like(xbuf)
        first_spare = y2_hbm.shape[0] - n_spare_rows
        clear = scatter_copy(pl.ds(0, n_spare_rows), pl.ds(first_spare, n_spare_rows), 0)
        obuf[0, pl.ds(0, n_spare_rows), :] = jnp.zeros((n_spare_rows, LANES), F32)
        clear.start()
        clear.wait()

    @pl.when((i == 0) & (nu > 0))
    def _():
        start_gather(0, 0)

    @pl.when(i + 1 < nu)
    def _():
        start_gather(i + 1, 1 - slot)

    prev = be_ref[jnp.maximum(i - 1, 0)]

    @pl.when((i < nu) & ((i == 0) | (be_ref[i] != prev)))
    def _():
        wg_scr[...] = wg_ref[0, 0].astype(BF16)
        wu_scr[...] = wu_ref[0, 0].astype(BF16)
        wd_scr[...] = wd_ref[0, 0].astype(BF16)

    @pl.when(i < nu)
    def _():
        wait_gather(i, slot)
        xb = _load_token_tiles(xbuf.at[slot]).astype(BF16)
        hid = (jax.nn.silu(jnp.dot(xb, wg_scr[...], preferred_element_type=F32))
               * jnp.dot(xb, wu_scr[...], preferred_element_type=F32))
        y = jnp.dot(hid.astype(BF16), wd_scr[...], preferred_element_type=F32)
        _store_token_tiles(obuf.at[slot], y)
        start_scatter(i, slot)

    @pl.when((i >= 1) & (i < nu + 1))
    def _():
        wait_scatter(i - 1, 1 - slot)

    @pl.when((i == pl.num_programs(0) - 1) & (i < nu))
    def _():
        wait_scatter(i, slot)


def _expert_ffn(layer, blk_e, n_used, blk_start, blk_cnt, src_rows, dst_rows, h2t, wg, wu, wd, n_tok):
    d, f = wg.shape[2], wg.shape[3]
    bm = MOE_BLOCK
    n_blocks = blk_e.shape[0] - 1
    n_spare = N_EXPERTS * (ROW_GROUP - 1)

    def wspec(shape):
        return pl.BlockSpec((1, 1) + shape, lambda i, lay, be, *_: (lay[0], be[i], 0, 0))

    grid_spec = pltpu.PrefetchScalarGridSpec(
        num_scalar_prefetch=7,
        grid=(n_blocks,),
        in_specs=[pl.BlockSpec(memory_space=pl.ANY), wspec((d, f)), wspec((d, f)), wspec((f, d))],
        out_specs=pl.BlockSpec(memory_space=pl.ANY),
        scratch_shapes=[pltpu.VMEM((d, f), BF16), pltpu.VMEM((d, f), BF16), pltpu.VMEM((f, d), BF16),
                        pltpu.VMEM((2, bm * SUBLANES, LANES), F32), pltpu.VMEM((2, bm * SUBLANES, LANES), F32),
                        pltpu.SemaphoreType.DMA((2,)), pltpu.SemaphoreType.DMA((2,))],
    )
    return pl.pallas_call(
        functools.partial(_ffn_kernel, n_spare_rows=n_spare * SUBLANES),
        grid_spec=grid_spec,
        out_shape=jax.ShapeDtypeStruct(((n_tok * TOP_K + n_spare) * SUBLANES, LANES), F32),
        compiler_params=_cparams("arbitrary"),
    )(layer, blk_e, n_used, blk_start, blk_cnt, src_rows, dst_rows, h2t, wg, wu, wd)


def _combine_kernel(x_ref, y2_ref, rt_ref, ml_ref, mc_ref, o_ref, *, tm, tiles_per_b, n_ctx):
    is_ctx = _is_ctx_rows(tm, tiles_per_b, n_ctx)
    rt = rt_ref[...]
    step = TOP_K * SUBLANES
    y0 = jnp.concatenate([y2_ref[pl.ds(k, tm, stride=step), :] for k in range(SUBLANES)], axis=1)
    y1 = jnp.concatenate([y2_ref[pl.ds(SUBLANES + k, tm, stride=step), :] for k in range(SUBLANES)], axis=1)
    y = rt[:, 0:1] * y0 + rt[:, 1:2] * y1
    o_ref[...] = x_ref[...] + _mod_rows(ml_ref, mc_ref, 5, is_ctx) * y


def _combine(x2, y2, route, mod_l, mod_c, t_len, n_ctx):
    m, d = x2.shape
    tm = _row_tile(t_len)
    tpb = t_len // tm
    kern = functools.partial(_combine_kernel, tm=tm, tiles_per_b=tpb, n_ctx=n_ctx)
    return pl.pallas_call(
        kern,
        grid=(m // tm,),
        in_specs=[pl.BlockSpec((tm, d), lambda i: (i, 0)),
                  pl.BlockSpec((tm * TOP_K * SUBLANES, LANES), lambda i: (i, 0)),
                  pl.BlockSpec((tm, LANES), lambda i: (i, 0)),
                  pl.BlockSpec((1, N_MOD, d), lambda i: (i // tpb, 0, 0)),
                  pl.BlockSpec((1, N_MOD, d), lambda i: (0, 0, 0))],
        out_specs=pl.BlockSpec((tm, d), lambda i: (i, 0)),
        out_shape=jax.ShapeDtypeStruct((m, d), F32),
        compiler_params=_cparams("parallel"),
    )(x2, y2, route, mod_l, mod_c)


def _final_norm_kernel(x_ref, g_ref, o_ref):
    o_ref[0] = _rmsnorm_rows(x_ref[0], g_ref[...])


def _final_norm(x3, g, n_ctx):
    bsz, t_len, d = x3.shape
    n_lat = t_len - n_ctx
    tm = n_ctx
    off = n_ctx // tm
    return pl.pallas_call(
        _final_norm_kernel,
        grid=(bsz, n_lat // tm),
        in_specs=[pl.BlockSpec((1, tm, d), lambda b, j: (b, j + off, 0)),
                  pl.BlockSpec((1, d), lambda b, j: (0, 0))],
        out_specs=pl.BlockSpec((1, tm, d), lambda b, j: (b, j, 0)),
        out_shape=jax.ShapeDtypeStruct((bsz, n_lat, d), F32),
        compiler_params=_cparams("parallel", "parallel"),
    )(x3, g.reshape(1, d))


def _dispatch_tables(eid, n_tok):
    a = n_tok * TOP_K
    bm = MOE_BLOCK
    pad = ROW_GROUP - 1
    flat_e = eid.reshape(a)
    experts = jnp.arange(N_EXPERTS, dtype=jnp.int32)
    counts = jnp.sum(flat_e[:, None] == experts[None, :], axis=0, dtype=jnp.int32)
    nblk_e = (counts + bm - 1) // bm
    blk_end = jnp.cumsum(nblk_e)
    blk_first = blk_end - nblk_e
    order = jnp.argsort(jnp.concatenate([flat_e, jnp.repeat(experts, pad)])).astype(jnp.int32)
    seg = counts + pad
    seg_start = jnp.cumsum(seg) - seg
    src_rows = jnp.where(order < a, lax.shift_right_logical(order, 1), 0) * SUBLANES
    dst_rows = order * SUBLANES
    n_blocks = -(-a // bm) + N_EXPERTS
    blk = jnp.arange(n_blocks + 1, dtype=jnp.int32)
    blk_e = jnp.minimum(jnp.sum(blk_end[None, :] <= blk[:, None], axis=1, dtype=jnp.int32), N_EXPERTS - 1)
    within = (blk - blk_first[blk_e]) * bm
    n_used = blk_end[-1].reshape(1)
    blk_start = jnp.clip(seg_start[blk_e] + within, 0, order.shape[0] - 1)
    blk_cnt = jnp.where(blk < n_used[0], jnp.clip(counts[blk_e] - within, 0, bm), 0)
    return src_rows, dst_rows, blk_e, blk_start, blk_cnt, n_used


def _moe(layer, h2t, route, wg, wu, wd):
    n_tok = route.shape[0]
    eid = route[:, 2:4].astype(jnp.int32)
    src_rows, dst_rows, blk_e, blk_start, blk_cnt, n_used = _dispatch_tables(eid, n_tok)
    return _expert_ffn(layer, blk_e, n_used, blk_start, blk_cnt, src_rows, dst_rows, h2t, wg, wu, wd, n_tok)


def kernel(x, c, ctx, c_ctx, w_ada, b_ada, norm1_g, norm2_g, w_in, s5_lam_re, s5_lam_im, s5_log_dt, s5_b_re, s5_b_im, s5_c_re, s5_c_im, s5_d, s5_w_glu, hgrn_lb_raw, w_branch_s5, w_branch_hgrn, w_branch_ret, w_out, moe_w_group, moe_b_group, moe_w_expert, moe_b_expert, moe_w_gate, moe_w_up, moe_w_down, final_norm_g):
    bsz, n_lat, d = x.shape
    n_ctx = ctx.shape[1]
    t_len = n_ctx + n_lat
    depth = w_ada.shape[0]
    m = bsz * t_len
    w_s5 = s5_d.shape[1]
    w_hg = w_branch_hgrn.shape[1]
    w_ret = w_branch_ret.shape[1]
    n_heads = w_ret // HEAD_DIM

    pad_rows = (-(bsz + 1)) % SUBLANES
    cc = jnp.concatenate([c, c_ctx[None, :], jnp.zeros((pad_rows, d), F32)], axis=0)
    mods = _ada_tables(cc, w_ada, b_ada).reshape(depth, bsz + 1 + pad_rows, N_MOD, d)

    p_lb = jax.nn.softmax(hgrn_lb_raw.astype(F32), axis=0)
    lower_bounds = jnp.cumsum(p_lb, axis=0) - p_lb[0:1]
    ret_consts = _ret_constants(n_heads, t_len, n_ctx)
    ones_bd = jnp.kron(jnp.eye(w_hg // HEAD_DIM, dtype=F32), jnp.ones((HEAD_DIM, HEAD_DIM), F32)).astype(BF16)

    xs = jnp.concatenate([ctx, x], axis=1).reshape(m, d)
    off_hg = w_s5
    off_ret = off_hg + 5 * w_hg
    off_gz = off_ret + 4 * w_ret
    w_in_bf = w_in.astype(BF16)
    s5w = jax.vmap(_s5_weights)(s5_lam_re, s5_lam_im, s5_log_dt, s5_b_re, s5_b_im, s5_c_re, s5_c_im, s5_d)
    for l in range(depth):
        mod_l = mods[l, :bsz]
        mod_c = mods[l, bsz:bsz + 1]
        u, hg, ret, gz = _project(xs, norm1_g[l], mod_l, mod_c, w_in_bf, l,
                                  (0, off_hg, off_ret, off_gz, w_in.shape[2]), t_len, n_ctx)

        ys5 = _s5_mix(u.reshape(bsz, t_len, w_s5), *(w[l] for w in s5w), n_ctx).reshape(m, w_s5)
        hg_of, hg_ob = _hg_mix(hg.reshape(bsz, t_len, 5 * w_hg), lower_bounds[l], n_ctx)
        ret_of, ret_ob = _ret_mix(ret.reshape(bsz, t_len, 4 * w_ret), ret_consts, n_ctx)

        wr = jnp.concatenate([moe_w_group[l], moe_w_expert[l],
                              jnp.zeros((d, LANES - N_GROUPS - N_EXPERTS), F32)], axis=1)
        br = jnp.concatenate([moe_b_group[l], moe_b_expert[l],
                              jnp.zeros((LANES - N_GROUPS - N_EXPERTS,), F32)])[None, :]
        wr_hi = wr.astype(BF16)
        wr = jnp.concatenate([wr_hi, (wr - wr_hi.astype(F32)).astype(BF16)], axis=1)
        wts = (s5_w_glu[l].astype(BF16), w_branch_s5[l].astype(BF16), w_branch_hgrn[l].astype(BF16),
               w_branch_ret[l].astype(BF16), w_out[l].astype(BF16), ones_bd, wr, br)
        xs, h2, route = _merge(xs, ys5, hg_of.reshape(m, w_hg), hg_ob.reshape(m, w_hg), hg,
                               ret_of.reshape(m, w_ret), ret_ob.reshape(m, w_ret), ret, gz,
                               mod_l, mod_c, norm2_g[l].reshape(1, d), wts, t_len, n_ctx)
        y2 = _moe(jnp.full((1,), l, jnp.int32), h2, route, moe_w_gate, moe_w_up, moe_w_down)
        xs = _combine(xs, y2, route, mod_l, mod_c, t_len, n_ctx)
    return _final_norm(xs.reshape(bsz, t_len, d), final_norm_g, n_ctx)
```

```python
import functools

import jax
import jax.numpy as jnp
import numpy as np
from jax import lax
from jax.experimental import pallas as pl
from jax.experimental.pallas import tpu as pltpu

F32 = jnp.float32
BF16 = jnp.bfloat16
HIGHEST = lax.Precision.HIGHEST

LANES = 128
SUBLANES = 8
VMEM_LIMIT = 56 * 1024 * 1024

EPS = 1e-6
N_MOD = 6
GRID_W = 64
HEAD_DIM = 64
PAIR = 2 * HEAD_DIM
S5_GROUP = 16
S5_STATE = 64
S5_CHUNK = 16
HG_CHUNK = 64
HG_HALF = HG_CHUNK // 2
HG_BLOCK = 256
RET_BLOCK = 256
ROPE_BASE = 10000.0
N_GROUPS = 4
EXPERTS_PER_GROUP = 8
N_EXPERTS = N_GROUPS * EXPERTS_PER_GROUP
TOP_K = 2
MOE_BLOCK = 512
ROW_GROUP = 8


def _cparams(*sem):
    return pltpu.CompilerParams(dimension_semantics=sem, vmem_limit_bytes=VMEM_LIMIT)


def _row_tile(t_len):
    for tm in (544, 512, 384, 272, 256, 128, 64, 32, 16, 8):
        if t_len % tm == 0:
            return tm
    raise ValueError(f"unsupported stream length {t_len}")


def _ada_kernel(c_ref, w_ref, b_ref, o_ref):
    a = jax.nn.silu(c_ref[...]).astype(BF16)
    o_ref[0] = jnp.dot(a, w_ref[0].astype(BF16), preferred_element_type=F32) + b_ref[0]


def _ada_tables(cc, w_ada, b_ada):
    depth, d, n = w_ada.shape
    rows = cc.shape[0]
    tn = 1536 if n % 1536 == 0 else n
    return pl.pallas_call(
        _ada_kernel,
        grid=(depth, n // tn),
        in_specs=[
            pl.BlockSpec((rows, d), lambda l, j: (0, 0)),
            pl.BlockSpec((1, d, tn), lambda l, j: (l, 0, j)),
            pl.BlockSpec((1, 1, tn), lambda l, j: (l, 0, j)),
        ],
        out_specs=pl.BlockSpec((1, rows, tn), lambda l, j: (l, 0, j)),
        out_shape=jax.ShapeDtypeStruct((depth, rows, n), F32),
        compiler_params=_cparams("parallel", "parallel"),
    )(cc, w_ada, b_ada.reshape(depth, 1, n))


def _mod_rows(ml_ref, mc_ref, idx, is_ctx):
    return jnp.where(is_ctx, mc_ref[0, idx:idx + 1, :], ml_ref[0, idx:idx + 1, :])


def _is_ctx_rows(tm, tiles_per_b, n_ctx):
    jt = pl.program_id(0) % tiles_per_b
    row = jt * tm + lax.broadcasted_iota(jnp.int32, (tm, 1), 0)
    return row < n_ctx


def _store_token_tiles(ref, val):
    n_rows = val.shape[0]
    for k in range(val.shape[1] // LANES):
        ref[pl.ds(k, n_rows, stride=SUBLANES), :] = val[:, LANES * k:LANES * (k + 1)]


def _load_token_tiles(ref):
    n_rows = ref.shape[0] // SUBLANES
    return jnp.concatenate([ref[pl.ds(k, n_rows, stride=SUBLANES), :] for k in range(SUBLANES)], axis=1)


def _rmsnorm_rows(x, g):
    return x * lax.rsqrt(jnp.mean(x * x, axis=-1, keepdims=True) + EPS) * g


def _proj_kernel(x_ref, g_ref, ml_ref, mc_ref, w_ref, *out_refs, tm, tiles_per_b, n_ctx, offsets):
    is_ctx = _is_ctx_rows(tm, tiles_per_b, n_ctx)
    y = _rmsnorm_rows(x_ref[...], g_ref[...])
    y = y * (1.0 + _mod_rows(ml_ref, mc_ref, 1, is_ctx)) + _mod_rows(ml_ref, mc_ref, 0, is_ctx)
    h = y.astype(BF16)
    for k, o_ref in enumerate(out_refs):
        w = w_ref[0, :, offsets[k]:offsets[k + 1]]
        o_ref[...] = jnp.dot(h, w, preferred_element_type=F32).astype(o_ref.dtype)


def _project(x2, g, mod_l, mod_c, w_all, layer, offsets, t_len, n_ctx):
    m, d = x2.shape
    tm = _row_tile(t_len)
    tpb = t_len // tm
    kern = functools.partial(_proj_kernel, tm=tm, tiles_per_b=tpb, n_ctx=n_ctx, offsets=offsets)
    widths = [offsets[k + 1] - offsets[k] for k in range(len(offsets) - 1)]
    out_dtypes = (F32, F32, BF16, BF16)
    return pl.pallas_call(
        kern,
        grid=(m // tm,),
        in_specs=[
            pl.BlockSpec((tm, d), lambda i: (i, 0)),
            pl.BlockSpec((1, d), lambda i: (0, 0)),
            pl.BlockSpec((1, N_MOD, d), lambda i: (i // tpb, 0, 0)),
            pl.BlockSpec((1, N_MOD, d), lambda i: (0, 0, 0)),
            pl.BlockSpec((1,) + w_all.shape[1:], lambda i: (layer, 0, 0), pipeline_mode=pl.Buffered(1)),
        ],
        out_specs=[pl.BlockSpec((tm, w), lambda i: (i, 0)) for w in widths],
        out_shape=[jax.ShapeDtypeStruct((m, w), dt) for w, dt in zip(widths, out_dtypes)],
        compiler_params=_cparams("parallel"),
    )(x2, g.reshape(1, d), mod_l, mod_c, w_all)


def _s5_weights(lam_re, lam_im, log_dt, b_re, b_im, c_re, c_im, d_skip):
    n_dir, g_n, p_n = lam_re.shape
    h_n = b_re.shape[-1]
    cn = S5_CHUNK
    lr, li = lam_re.astype(F32), lam_im.astype(F32)
    dt = jnp.exp(log_dt.astype(F32))[..., None]
    mag = jnp.exp(lr * dt)
    ar = mag * jnp.cos(li * dt)
    ai = mag * jnp.sin(li * dt)
    den = lr * lr + li * li
    zr = ((ar - 1.0) * lr + ai * li) / den
    zi = (ai * lr - (ar - 1.0) * li) / den
    bbr = zr[..., None] * b_re - zi[..., None] * b_im
    bbi = zr[..., None] * b_im + zi[..., None] * b_re
    j = jnp.arange(cn + 1, dtype=F32)[:, None, None, None]
    pmag = jnp.exp(lr[None] * dt[None] * j)
    pang = li[None] * dt[None] * j
    pr = pmag * jnp.cos(pang)
    pi = pmag * jnp.sin(pang)
    wr = pr[..., None] * bbr[None] - pi[..., None] * bbi[None]
    wi = pr[..., None] * bbi[None] + pi[..., None] * bbr[None]
    kj = (jnp.einsum('dghp,jdgpk->jdghk', c_re, wr, precision=HIGHEST)
          - jnp.einsum('dghp,jdgpk->jdghk', c_im, wi, precision=HIGHEST))
    s_idx = jnp.arange(cn)[:, None]
    t_idx = jnp.arange(cn)[None, :]
    lag = t_idx - s_idx
    kf = kj[:cn, 0][jnp.clip(lag, 0, cn - 1)]
    kb = kj[:cn, 1][jnp.clip(-lag, 0, cn - 1)]
    kf = jnp.where((lag >= 0)[:, :, None, None, None], kf, 0.0)
    kb = jnp.where((lag <= 0)[:, :, None, None, None], kb, 0.0)
    dsk = d_skip.astype(F32).reshape(g_n, h_n)
    eye_h = jnp.eye(h_n, dtype=F32)
    diag = (lag == 0)[:, :, None, None, None] * (dsk[None, None, :, :, None] * eye_h[None, None, None])
    m_full = kf + kb + diag
    m_mat = m_full.transpose(2, 0, 4, 1, 3).reshape(g_n, cn * h_n, cn * h_n)
    wf_r = wr[:cn, 0][::-1]
    wf_i = wi[:cn, 0][::-1]
    wb_r = wr[:cn, 1]
    wb_i = wi[:cn, 1]
    w_in = jnp.stack([wf_r, wf_i, wb_r, wb_i], axis=0)
    w_in = w_in.transpose(2, 1, 4, 0, 3).reshape(g_n, cn * h_n, 4 * p_n)
    def out_map(d, powers):
        prd, pid = pr[powers, d], pi[powers, d]
        w_re = c_re[d][None] * prd[:, :, None, :] - c_im[d][None] * pid[:, :, None, :]
        w_im = c_re[d][None] * pid[:, :, None, :] + c_im[d][None] * prd[:, :, None, :]
        return w_re, -w_im
    t_arr = jnp.arange(cn)
    of_r, of_i = out_map(0, t_arr + 1)
    ob_r, ob_i = out_map(1, cn - t_arr)
    w_out = jnp.stack([of_r, of_i, ob_r, ob_i], axis=0)
    w_out = w_out.transpose(2, 0, 4, 1, 3).reshape(g_n, 4 * p_n, cn * h_n)
    slot = jax.nn.one_hot(jnp.arange(g_n) % 2, 2, dtype=F32)
    w_in = w_in.reshape(g_n, cn * h_n, 4, 1, p_n) * slot[:, None, None, :, None]
    w_in = w_in.reshape(g_n, cn * h_n, 8 * p_n)
    w_out = w_out.reshape(g_n, 4, 1, p_n, cn * h_n) * slot[:, None, :, None, None]
    w_out = w_out.reshape(g_n, 8 * p_n, cn * h_n)
    a_mat = jnp.stack([pr[cn, 0].reshape(-1), pi[cn, 0].reshape(-1),
                       pr[cn, 1].reshape(-1), pi[cn, 1].reshape(-1)], axis=0)
    w1 = jnp.concatenate([m_mat, w_in], axis=-1).astype(BF16)
    return w1, w_out.astype(BF16), a_mat


def _s5_fold_perm():
    j, g8, h = np.meshgrid(np.arange(8), np.arange(8), np.arange(S5_GROUP), indexing='ij')
    src = (j * 8 + g8) * S5_GROUP + h
    dst = (g8 * 8 + j) * S5_GROUP + h
    perm = np.zeros((8 * LANES, 8 * LANES), np.float32)
    perm[src.reshape(-1), dst.reshape(-1)] = 1.0
    return jnp.asarray(perm, dtype=BF16)


def _s5_kernel(u_ref, perm_ref, w1_ref, wo_ref, a_ref, y_ref, ug_scr, yg_scr, xfr, xfi, xbr, xbi,
               *, n_chunks, n_ctx_chunks, n_groups):
    cn = S5_CHUNK
    blk = cn * S5_GROUP
    tok_per_slab = LANES // S5_GROUP
    for c in range(cn // tok_per_slab):
        slabs = [u_ref[0, pl.ds(tok_per_slab * c + j, n_chunks, stride=cn), :] for j in range(tok_per_slab)]
        s = jnp.concatenate(slabs, axis=1).astype(BF16)
        up = jnp.dot(s, perm_ref[...], preferred_element_type=F32).astype(BF16)
        for g in range(n_groups):
            ug_scr[g, :, LANES * c:LANES * (c + 1)] = up[:, LANES * g:LANES * (g + 1)]
    xs = (xfr, xfi, xbr, xbi)
    for g in range(n_groups):
        z = jnp.dot(ug_scr[g], w1_ref[g], preferred_element_type=F32)
        yg_scr[g] = z[:, :blk]
        lanes = slice(LANES * (g // 2), LANES * (g // 2 + 1))
        for k, xr in enumerate(xs):
            piece = z[:, blk + LANES * k:blk + LANES * (k + 1)]
            if g % 2 == 0:
                xr[:, lanes] = piece
            else:
                xr[:, lanes] += piece
    arf, aif, arb, aib = a_ref[0:1, :], a_ref[1:2, :], a_ref[2:3, :], a_ref[3:4, :]

    def body(i, carry):
        fr, fi, br, bi = carry
        nb = jnp.where(i < n_ctx_chunks, n_ctx_chunks - 1 - i, n_chunks - 1 - (i - n_ctx_chunks))
        x_fr, x_fi = xfr[pl.ds(i, 1), :], xfi[pl.ds(i, 1), :]
        x_br, x_bi = xbr[pl.ds(nb, 1), :], xbi[pl.ds(nb, 1), :]
        xfr[pl.ds(i, 1), :] = fr
        xfi[pl.ds(i, 1), :] = fi
        xbr[pl.ds(nb, 1), :] = br
        xbi[pl.ds(nb, 1), :] = bi
        return (arf * fr - aif * fi + x_fr, arf * fi + aif * fr + x_fi,
                arb * br - aib * bi + x_br, arb * bi + aib * br + x_bi)

    zero = jnp.zeros((1, xfr.shape[1]), F32)
    lax.fori_loop(0, n_chunks, body, (zero, zero, zero, zero))
    for g in range(n_groups):
        lanes = slice(LANES * (g // 2), LANES * (g // 2 + 1))
        hp = jnp.concatenate([xr[:, lanes] for xr in xs], axis=1).astype(BF16)
        yg_scr[g] += jnp.dot(hp, wo_ref[g], preferred_element_type=F32)
    for c in range(cn // tok_per_slab):
        ycat = jnp.concatenate([yg_scr[g, :, LANES * c:LANES * (c + 1)] for g in range(n_groups)], axis=1)
        hi = ycat.astype(BF16)
        lo = (ycat - hi.astype(F32)).astype(BF16)
        r = _dot_nt(hi, perm_ref[...]) + _dot_nt(lo, perm_ref[...])
        for j in range(tok_per_slab):
            y_ref[0, pl.ds(tok_per_slab * c + j, n_chunks, stride=cn), :] = r[:, LANES * j:LANES * (j + 1)]


def _s5_mix(u, w1, w_out, a_mat, n_ctx):
    bsz, t_len, width = u.shape
    g_n = width // S5_GROUP
    cn = S5_CHUNK
    n_chunks = t_len // cn
    blk = cn * S5_GROUP
    perm = _s5_fold_perm()
    gs = LANES // S5_GROUP
    state_w = gs * S5_STATE
    kern = functools.partial(_s5_kernel, n_chunks=n_chunks, n_ctx_chunks=n_ctx // cn, n_groups=gs)
    return pl.pallas_call(
        kern,
        grid=(bsz, g_n // gs),
        in_specs=[pl.BlockSpec((1, t_len, LANES), lambda b, s: (b, 0, s)),
                  pl.BlockSpec(perm.shape, lambda b, s: (0, 0), pipeline_mode=pl.Buffered(1)),
                  pl.BlockSpec((gs,) + w1.shape[1:], lambda b, s: (s, 0, 0)),
                  pl.BlockSpec((gs,) + w_out.shape[1:], lambda b, s: (s, 0, 0)),
                  pl.BlockSpec((4, state_w), lambda b, s: (0, s))],
        out_specs=pl.BlockSpec((1, t_len, LANES), lambda b, s: (b, 0, s)),
        out_shape=jax.ShapeDtypeStruct((bsz, t_len, width), F32),
        scratch_shapes=[pltpu.VMEM((gs, n_chunks, blk), BF16), pltpu.VMEM((gs, n_chunks, blk), F32)]
        + [pltpu.VMEM((n_chunks, state_w), F32)] * 4,
        compiler_params=_cparams("parallel", "parallel"),
    )(u, perm, w1, w_out, a_mat)


def _head0_lanes(shape):
    return lax.broadcasted_iota(jnp.int32, shape, len(shape) - 1) < HEAD_DIM


def _stack_heads(q):
    m0 = _head0_lanes(q.shape)
    return jnp.concatenate([jnp.where(m0, q, 0.0), jnp.where(m0, 0.0, q)], axis=-2)


def _unstack_heads(o2):
    c = o2.shape[-2] // 2
    top, bottom = o2[..., :c, :], o2[..., c:, :]
    return jnp.where(_head0_lanes(top.shape), top, bottom)


def _pair_block_mask():
    r = lax.broadcasted_iota(jnp.int32, (PAIR, PAIR), 0) < HEAD_DIM
    c = lax.broadcasted_iota(jnp.int32, (PAIR, PAIR), 1) < HEAD_DIM
    return r == c


def _dot_nt(a, b):
    return lax.dot_general(a, b, (((1,), (1,)), ((), ())), preferred_element_type=F32)


def _dot_tn(a, b):
    return lax.dot_general(a, b, (((0,), (0,)), ((), ())), preferred_element_type=F32)


def _bwd_block(j, n_ctx_blocks, n_blocks):
    return jnp.where(j < n_ctx_blocks, n_ctx_blocks - 1 - j, n_blocks - 1 - (j - n_ctx_blocks))


def _bdot_nt(a, b):
    return lax.dot_general(a, b, (((2,), (2,)), ((0,), (0,))), preferred_element_type=F32)


def _bdot_nn(a, b):
    return lax.dot_general(a, b, (((2,), (1,)), ((0,), (0,))), preferred_element_type=F32)


def _hg_block(q, z, v, lb, ht, reverse):
    rows = q.shape[0]
    c = HG_CHUNK
    hh = HG_HALF
    n = rows // c
    f = lb + (1.0 - lb) * jax.nn.sigmoid(z)
    logf = jnp.log(f)
    kk = 1.0 - f
    ri = lax.broadcasted_iota(jnp.int32, (n, c, c), 1)
    ci = lax.broadcasted_iota(jnp.int32, (n, c, c), 2)
    tri = jnp.where((ci >= ri) if reverse else (ci <= ri), 1.0, 0.0).astype(BF16)
    l1 = logf.astype(BF16)
    r1 = logf - l1.astype(F32)
    l2 = r1.astype(BF16)
    l3 = (r1 - l2.astype(F32)).astype(BF16)
    b3 = _bdot_nn(tri, jnp.concatenate([l1, l2, l3], axis=1).reshape(n, c, 3 * PAIR))
    b4 = b3[..., 0:PAIR] + b3[..., PAIR:2 * PAIR] + b3[..., 2 * PAIR:3 * PAIR]
    q4, k4 = q.reshape(n, c, PAIR), kk.reshape(n, c, PAIR)
    vb = v.astype(BF16).reshape(n, c, PAIR)
    if reverse:
        first, second = slice(hh, c), slice(0, hh)
        r = b4[:, hh:hh + 1]
        bend = b4[:, 0:1]
    else:
        first, second = slice(0, hh), slice(hh, c)
        r = b4[:, hh - 1:hh]
        bend = b4[:, c - 1:c]
    b1, b2 = b4[:, first], b4[:, second]
    qd1 = q4[:, first] * jnp.exp(b1)
    qd2 = q4[:, second] * jnp.exp(b2 - r)
    k1d = k4[:, first] * jnp.exp(-b1)
    k2 = k4 * jnp.exp(r - b4)
    s1 = _bdot_nt(_stack_heads(qd1).astype(BF16), k1d.astype(BF16))
    s2 = _bdot_nt(_stack_heads(qd2).astype(BF16), k2.astype(BF16))
    t1 = lax.broadcasted_iota(jnp.int32, (n, 2 * hh, hh), 1) % hh
    c1 = lax.broadcasted_iota(jnp.int32, (n, 2 * hh, hh), 2)
    t2 = lax.broadcasted_iota(jnp.int32, (n, 2 * hh, c), 1) % hh
    c2 = lax.broadcasted_iota(jnp.int32, (n, 2 * hh, c), 2)
    if reverse:
        s1 = jnp.where(c1 >= t1, s1, 0.0)
        s2 = jnp.where(c2 >= t2, s2, 0.0)
    else:
        s1 = jnp.where(c1 <= t1, s1, 0.0)
        s2 = jnp.where(c2 <= t2 + hh, s2, 0.0)
    o1 = _unstack_heads(_bdot_nn(s1.astype(BF16), vb[:, first]))
    o2 = _unstack_heads(_bdot_nn(s2.astype(BF16), vb))
    o = jnp.concatenate([o2, o1] if reverse else [o1, o2], axis=1)
    ke = (k4 * jnp.exp(bend - b4)).astype(BF16)
    dec = jnp.exp(bend)
    bmask = _pair_block_mask()
    states = [None] * n
    for ch in (range(n - 1, -1, -1) if reverse else range(n)):
        states[ch] = ht.astype(BF16)
        ht = ht * dec[ch] + jnp.where(bmask, _dot_tn(vb[ch], ke[ch]), 0.0)
    o = o + _bdot_nt((q4 * jnp.exp(b4)).astype(BF16), jnp.stack(states, axis=0))
    return o.reshape(rows, PAIR), ht


def _hg_kernel(qf_ref, zf_ref, vf_ref, qb_ref, zb_ref, vb_ref, lb_ref, of_ref, ob_ref, hf_scr, hb_scr, *, n_pairs):
    @pl.when(pl.program_id(1) == 0)
    def _():
        hf_scr[...] = jnp.zeros_like(hf_scr)
        hb_scr[...] = jnp.zeros_like(hb_scr)

    for p in range(n_pairs):
        sl = slice(p * PAIR, (p + 1) * PAIR)
        o, hn = _hg_block(qf_ref[0, :, sl], zf_ref[0, :, sl], vf_ref[0, :, sl], lb_ref[0:1, sl], hf_scr[p], False)
        of_ref[0, :, sl] = o
        hf_scr[p] = hn
        o, hn = _hg_block(qb_ref[0, :, sl], zb_ref[0, :, sl], vb_ref[0, :, sl], lb_ref[1:2, sl], hb_scr[p], True)
        ob_ref[0, :, sl] = o
        hb_scr[p] = hn


def _hg_mix(hg, lb, n_ctx):
    bsz, t_len, total = hg.shape
    w = total // 5
    n_pairs = w // PAIR
    c = HG_BLOCK
    nc, ncc = t_len // c, n_ctx // c
    bw = functools.partial(_bwd_block, n_ctx_blocks=ncc, n_blocks=nc)

    def fspec(col):
        return pl.BlockSpec((1, c, w), lambda b, j: (b, j, col))

    def bspec(col):
        return pl.BlockSpec((1, c, w), lambda b, j: (b, bw(j), col))

    kern = functools.partial(_hg_kernel, n_pairs=n_pairs)
    return pl.pallas_call(
        kern,
        grid=(bsz, nc),
        in_specs=[fspec(0), fspec(1), fspec(3), bspec(0), bspec(2), bspec(3),
                  pl.BlockSpec((2, w), lambda b, j: (0, 0))],
        out_specs=[pl.BlockSpec((1, c, w), lambda b, j: (b, j, 0)),
                   pl.BlockSpec((1, c, w), lambda b, j: (b, bw(j), 0))],
        out_shape=[jax.ShapeDtypeStruct((bsz, t_len, w), F32)] * 2,
        scratch_shapes=[pltpu.VMEM((n_pairs, PAIR, PAIR), F32)] * 2,
        compiler_params=_cparams("parallel", "arbitrary"),
    )(hg, hg, hg, hg, hg, hg, lb)


def _rope_rows(x, cos, sin):
    hd2 = HEAD_DIM // 2
    lane = lax.broadcasted_iota(jnp.int32, x.shape, 1) % HEAD_DIM
    swapped = jnp.where(lane < hd2, pltpu.roll(x, PAIR - hd2, 1), pltpu.roll(x, hd2, 1))
    return x * cos + swapped * sin


def _ret_kernel(qf_ref, kf_ref, vf_ref, cf_ref, sf_ref, qb_ref, kb_ref, vb_ref, cb_ref, sb_ref,
                d_ref, gqf_ref, gkf_ref, gqb_ref, gkb_ref, gh_ref, of_ref, ob_ref, hf_scr, hb_scr, *, n_pairs):
    @pl.when(pl.program_id(1) == 0)
    def _():
        hf_scr[...] = jnp.zeros_like(hf_scr)
        hb_scr[...] = jnp.zeros_like(hb_scr)

    kscale = HEAD_DIM ** -0.5
    bmask = _pair_block_mask()
    for p in range(n_pairs):
        sl = slice(p * PAIR, (p + 1) * PAIR)
        q = _rope_rows(qf_ref[0, :, sl].astype(F32), cf_ref[...], sf_ref[...])
        k = _rope_rows(kf_ref[0, :, sl].astype(F32) * kscale, cf_ref[...], sf_ref[...])
        vb16 = vf_ref[0, :, sl].astype(BF16)
        s = _dot_nt(_stack_heads(q).astype(BF16), k.astype(BF16)) * d_ref[p]
        o = _unstack_heads(jnp.dot(s.astype(BF16), vb16, preferred_element_type=F32))
        ht = hf_scr[p]
        o = o + _dot_nt((q * gqf_ref[:, sl]).astype(BF16), ht.astype(BF16))
        upd = _dot_tn(vb16, (k * gkf_ref[:, sl]).astype(BF16))
        hf_scr[p] = ht * gh_ref[p] + jnp.where(bmask, upd, 0.0)
        of_ref[0, :, sl] = o

        q = _rope_rows(qb_ref[0, :, sl].astype(F32), cb_ref[...], sb_ref[...])
        k = _rope_rows(kb_ref[0, :, sl].astype(F32) * kscale, cb_ref[...], sb_ref[...])
        vb16 = vb_ref[0, :, sl].astype(BF16)
        ht = hb_scr[p]
        ob_ref[0, :, sl] = _dot_nt((q * gqb_ref[:, sl]).astype(BF16), ht.astype(BF16))
        upd = _dot_tn(vb16, (k * gkb_ref[:, sl]).astype(BF16))
        hb_scr[p] = ht * gh_ref[p] + jnp.where(bmask, upd, 0.0)


def _ret_constants(n_heads, t_len, n_ctx):
    c = RET_BLOCK
    log_gamma = jnp.log(1.0 - 2.0 ** (-5.0 - jnp.arange(n_heads, dtype=F32)))
    t = jnp.arange(c, dtype=F32)
    dist = jnp.abs(t[:, None] - t[None, :])
    dmat = jnp.exp(log_gamma[:, None, None] * dist[None])
    dmat = dmat.reshape(n_heads // 2, 2 * c, c)
    lg_lane = jnp.repeat(log_gamma, HEAD_DIM)[None, :]
    gqf = jnp.exp(lg_lane * (t[:, None] + 1.0))
    gkf = jnp.exp(lg_lane * (c - 1.0 - t[:, None]))
    gqb = jnp.exp(lg_lane * (c - t[:, None]))
    gkb = jnp.exp(lg_lane * t[:, None])
    gh = jnp.exp(lg_lane * float(c)).reshape(n_heads // 2, 1, PAIR)
    gh = jnp.broadcast_to(gh, (n_heads // 2, PAIR, PAIR))
    n_lat = t_len - n_ctx
    rows = n_lat // GRID_W
    row = jnp.repeat(jnp.arange(rows, dtype=F32), GRID_W)
    col = jnp.broadcast_to(jnp.arange(GRID_W, dtype=F32)[None, :], (rows, GRID_W)).reshape(-1)
    n_freq = HEAD_DIM // 4
    inv_freq = ROPE_BASE ** (-jnp.arange(n_freq, dtype=F32) / n_freq)
    ang = jnp.concatenate([row[:, None] * inv_freq, col[:, None] * inv_freq], axis=-1)
    cos_l, sin_l = jnp.cos(ang), jnp.sin(ang)
    cos_h = jnp.concatenate([cos_l, cos_l], axis=-1)
    sin_h = jnp.concatenate([-sin_l, sin_l], axis=-1)
    cos_t = jnp.concatenate([jnp.ones((n_ctx, HEAD_DIM), F32), cos_h], axis=0)
    sin_t = jnp.concatenate([jnp.zeros((n_ctx, HEAD_DIM), F32), sin_h], axis=0)
    cos_t = jnp.concatenate([cos_t, cos_t], axis=-1)
    sin_t = jnp.concatenate([sin_t, sin_t], axis=-1)
    return dmat, gqf, gkf, gqb, gkb, gh, cos_t, sin_t


def _ret_mix(ret, consts, n_ctx):
    dmat, gqf, gkf, gqb, gkb, gh, cos_t, sin_t = consts
    bsz, t_len, total = ret.shape
    w = total // 4
    n_pairs = w // PAIR
    c = RET_BLOCK
    nb, ncb = t_len // c, n_ctx // c
    bw = functools.partial(_bwd_block, n_ctx_blocks=ncb, n_blocks=nb)

    def fspec(col):
        return pl.BlockSpec((1, c, w), lambda b, j: (b, j, col))

    def bspec(col):
        return pl.BlockSpec((1, c, w), lambda b, j: (b, bw(j), col))

    def const(a):
        nd = a.ndim
        return pl.BlockSpec(a.shape, lambda b, j: (0,) * nd)

    rope_f = pl.BlockSpec((c, PAIR), lambda b, j: (j, 0))
    rope_b = pl.BlockSpec((c, PAIR), lambda b, j: (bw(j), 0))
    kern = functools.partial(_ret_kernel, n_pairs=n_pairs)
    return pl.pallas_call(
        kern,
        grid=(bsz, nb),
        in_specs=[fspec(0), fspec(1), fspec(2), rope_f, rope_f, bspec(0), bspec(1), bspec(2), rope_b, rope_b,
                  const(dmat), const(gqf), const(gkf), const(gqb), const(gkb), const(gh)],
        out_specs=[pl.BlockSpec((1, c, w), lambda b, j: (b, j, 0)),
                   pl.BlockSpec((1, c, w), lambda b, j: (b, bw(j), 0))],
        out_shape=[jax.ShapeDtypeStruct((bsz, t_len, w), F32)] * 2,
        scratch_shapes=[pltpu.VMEM((n_pairs, PAIR, PAIR), F32)] * 2,
        compiler_params=_cparams("parallel", "arbitrary"),
    )(ret, ret, ret, cos_t, sin_t, ret, ret, ret, cos_t, sin_t, dmat, gqf, gkf, gqb, gkb, gh)


def _head_norm_gate(o, g, ones_bd):
    sq = o * o
    hi = sq.astype(BF16)
    lo = (sq - hi.astype(F32)).astype(BF16)
    ss = jnp.dot(hi, ones_bd, preferred_element_type=F32) + jnp.dot(lo, ones_bd, preferred_element_type=F32)
    return o * lax.rsqrt(ss * (1.0 / HEAD_DIM) + EPS) * jax.nn.silu(g)


def _route(lg):
    lane = lax.broadcasted_iota(jnp.int32, lg.shape, 1).astype(F32)
    big = float(1 << 20)
    neg = -jnp.inf
    gmask = lane < N_GROUPS
    lgm = jnp.where(gmask, lg, neg)
    gmax = jnp.max(lgm, axis=1, keepdims=True)
    gsum = jnp.sum(jnp.where(gmask, jnp.exp(lgm - gmax), 0.0), axis=1, keepdims=True)
    gp = 1.0 / gsum
    gi = jnp.min(jnp.where(gmask & (lgm == gmax), lane, big), axis=1, keepdims=True)
    lo_lane = N_GROUPS + EXPERTS_PER_GROUP * gi
    emask = (lane >= lo_lane) & (lane < lo_lane + EXPERTS_PER_GROUP)
    l1 = jnp.max(jnp.where(emask, lg, neg), axis=1, keepdims=True)
    i1 = jnp.min(jnp.where(emask & (lg == l1), lane, big), axis=1, keepdims=True)
    mask2 = emask & (lane != i1)
    l2 = jnp.max(jnp.where(mask2, lg, neg), axis=1, keepdims=True)
    i2 = jnp.min(jnp.where(mask2 & (lg == l2), lane, big), axis=1, keepdims=True)
    e21 = jnp.exp(l2 - l1)
    w1 = gp / (1.0 + e21)
    w2 = gp * e21 / (1.0 + e21)
    e1 = i1 - N_GROUPS
    e2 = i2 - N_GROUPS
    return jnp.where(lane == 0, w1, jnp.where(lane == 1, w2, jnp.where(lane == 2, e1, jnp.where(lane == 3, e2, 0.0))))


def _merge_kernel(x_ref, ys_ref, hof_ref, hob_ref, hgg_ref, rof_ref, rob_ref, rgg_ref, gz_ref,
                  ml_ref, mc_ref, g2_ref, wglu_ref, wbs_ref, wbh_ref, wbr_ref, wout_ref, ones_ref,
                  wr_ref, br_ref, xo_ref, h2_ref, rt_ref, *, tm, tiles_per_b, n_ctx):
    d = x_ref.shape[1]
    is_ctx = _is_ctx_rows(tm, tiles_per_b, n_ctx)
    y = jax.nn.gelu(ys_ref[...], approximate=True)
    y = y * jax.nn.sigmoid(jnp.dot(y.astype(BF16), wglu_ref[...], preferred_element_type=F32))
    t_s5 = jnp.dot(y.astype(BF16), wbs_ref[...], preferred_element_type=F32)
    yh = _head_norm_gate(hof_ref[...] + hob_ref[...], hgg_ref[...], ones_ref[...])
    t_hg = jnp.dot(yh.astype(BF16), wbh_ref[...], preferred_element_type=F32)
    yr = _head_norm_gate(rof_ref[...] + rob_ref[...], rgg_ref[...], ones_ref[...])
    t_ret = jnp.dot(yr.astype(BF16), wbr_ref[...], preferred_element_type=F32)
    merged = (jax.nn.sigmoid(gz_ref[:, 0:d]) * t_s5 + jax.nn.sigmoid(gz_ref[:, d:2 * d]) * t_hg
              + jax.nn.sigmoid(gz_ref[:, 2 * d:3 * d]) * t_ret)
    mix = jnp.dot(merged.astype(BF16), wout_ref[...], preferred_element_type=F32)
    xn = x_ref[...] + _mod_rows(ml_ref, mc_ref, 2, is_ctx) * mix
    xo_ref[...] = xn
    h2 = _rmsnorm_rows(xn, g2_ref[...])
    h2 = h2 * (1.0 + _mod_rows(ml_ref, mc_ref, 4, is_ctx)) + _mod_rows(ml_ref, mc_ref, 3, is_ctx)
    _store_token_tiles(h2_ref, h2)
    h_hi = h2.astype(BF16)
    h_lo = (h2 - h_hi.astype(F32)).astype(BF16)
    p_hi = jnp.dot(h_hi, wr_ref[...], preferred_element_type=F32)
    p_lo = jnp.dot(h_lo, wr_ref[:, 0:LANES], preferred_element_type=F32)
    rt_ref[...] = _route(p_hi[:, 0:LANES] + p_hi[:, LANES:2 * LANES] + p_lo + br_ref[...])


def _merge(x2, ys5, hg_of, hg_ob, hg2, ret_of, ret_ob, ret2, gz, mod_l, mod_c, g2, wts, t_len, n_ctx):
    m, d = x2.shape
    w_s5 = ys5.shape[1]
    w_h = hg_of.shape[1]
    tm = _row_tile(t_len)
    tpb = t_len // tm
    wglu, wbs, wbh, wbr, wout, ones_bd, wr, br = wts

    def rows(width, col=0):
        return pl.BlockSpec((tm, width), lambda i: (i, col))

    def const(a):
        nd = a.ndim
        return pl.BlockSpec(a.shape, lambda i: (0,) * nd)

    kern = functools.partial(_merge_kernel, tm=tm, tiles_per_b=tpb, n_ctx=n_ctx)
    return pl.pallas_call(
        kern,
        grid=(m // tm,),
        in_specs=[rows(d), rows(w_s5), rows(w_h), rows(w_h), rows(w_h, 4), rows(w_h), rows(w_h), rows(w_h, 3),
                  rows(3 * d),
                  pl.BlockSpec((1, N_MOD, d), lambda i: (i // tpb, 0, 0)),
                  pl.BlockSpec((1, N_MOD, d), lambda i: (0, 0, 0)),
                  const(g2), const(wglu), const(wbs), const(wbh), const(wbr), const(wout), const(ones_bd),
                  const(wr), const(br)],
        out_specs=[rows(d), pl.BlockSpec((tm * SUBLANES, LANES), lambda i: (i, 0)), rows(LANES)],
        out_shape=[jax.ShapeDtypeStruct((m, d), F32), jax.ShapeDtypeStruct((m * SUBLANES, LANES), F32),
                   jax.ShapeDtypeStruct((m, LANES), F32)],
        compiler_params=_cparams("parallel"),
    )(x2, ys5, hg_of, hg_ob, hg2, ret_of, ret_ob, ret2, gz, mod_l, mod_c, g2, wglu, wbs, wbh, wbr, wout,
      ones_bd, wr, br)


def _ffn_kernel(lay_ref, be_ref, nu_ref, bs_ref, cnt_ref, src_ref, dst_ref, h2_hbm, wg_ref, wu_ref, wd_ref, y2_hbm,
                wg_scr, wu_scr, wd_scr, xbuf, obuf, gsem, ssem, *, n_spare_rows):
    i = pl.program_id(0)
    nu = nu_ref[0]
    slot = i % 2
    tile = SUBLANES

    def gather_copy(rows_src, rows_dst, sl):
        return pltpu.make_async_copy(h2_hbm.at[rows_src, :], xbuf.at[sl, rows_dst, :], gsem.at[sl])

    def scatter_copy(rows_src, rows_dst, sl):
        return pltpu.make_async_copy(obuf.at[sl, rows_src, :], y2_hbm.at[rows_dst, :], ssem.at[sl])

    def n_groups(blk):
        return lax.shift_right_logical(cnt_ref[blk] + (ROW_GROUP - 1), ROW_GROUP.bit_length() - 1)

    def start_gather(blk, sl):
        base = bs_ref[blk]

        def body(g, carry):
            for j in range(ROW_GROUP):
                r = g * ROW_GROUP + j
                gather_copy(pl.ds(pl.multiple_of(src_ref[base + r], tile), tile),
                            pl.ds(pl.multiple_of(r * tile, tile), tile), sl).start()
            return carry

        lax.fori_loop(0, n_groups(blk), body, 0)

    def wait_gather(blk, sl):
        n = pl.multiple_of(n_groups(blk) * (ROW_GROUP * tile), tile)
        gather_copy(pl.ds(0, n), pl.ds(0, n), sl).wait()

    def start_scatter(blk, sl):
        base = bs_ref[blk]

        def body(g, carry):
            for j in range(ROW_GROUP):
                r = g * ROW_GROUP + j
                scatter_copy(pl.ds(pl.multiple_of(r * tile, tile), tile),
                             pl.ds(pl.multiple_of(dst_ref[base + r], tile), tile), sl).start()
            return carry

        lax.fori_loop(0, n_groups(blk), body, 0)

    def wait_scatter(blk, sl):
        n = pl.multiple_of(n_groups(blk) * (ROW_GROUP * tile), tile)
        scatter_copy(pl.ds(0, n), pl.ds(0, n), sl).wait()

    @pl.when(i == 0)
    def _():
        xbuf[...] = jnp.zeros_like(xbuf)
        first_spare = y2_hbm.shape[0] - n_spare_rows
        clear = scatter_copy(pl.ds(0, n_spare_rows), pl.ds(first_spare, n_spare_rows), 0)
        obuf[0, pl.ds(0, n_spare_rows), :] = jnp.zeros((n_spare_rows, LANES), F32)
        clear.start()
        clear.wait()

    @pl.when((i == 0) & (nu > 0))
    def _():
        start_gather(0, 0)

    @pl.when(i + 1 < nu)
    def _():
        start_gather(i + 1, 1 - slot)

    prev = be_ref[jnp.maximum(i - 1, 0)]

    @pl.when((i < nu) & ((i == 0) | (be_ref[i] != prev)))
    def _():
        wg_scr[...] = wg_ref[0, 0].astype(BF16)
        wu_scr[...] = wu_ref[0, 0].astype(BF16)
        wd_scr[...] = wd_ref[0, 0].astype(BF16)

    @pl.when(i < nu)
    def _():
        wait_gather(i, slot)
        xb = _load_token_tiles(xbuf.at[slot]).astype(BF16)
        hid = (jax.nn.silu(jnp.dot(xb, wg_scr[...], preferred_element_type=F32))
               * jnp.dot(xb, wu_scr[...], preferred_element_type=F32))
        y = jnp.dot(hid.astype(BF16), wd_scr[...], preferred_element_type=F32)
        _store_token_tiles(obuf.at[slot], y)
        start_scatter(i, slot)

    @pl.when((i >= 1) & (i < nu + 1))
    def _():
        wait_scatter(i - 1, 1 - slot)

    @pl.when((i == pl.num_programs(0) - 1) & (i < nu))
    def _():
        wait_scatter(i, slot)


def _expert_ffn(layer, blk_e, n_used, blk_start, blk_cnt, src_rows, dst_rows, h2t, wg, wu, wd, n_tok):
    d, f = wg.shape[2], wg.shape[3]
    bm = MOE_BLOCK
    n_blocks = blk_e.shape[0] - 1
    n_spare = N_EXPERTS * (ROW_GROUP - 1)

    def wspec(shape):
        return pl.BlockSpec((1, 1) + shape, lambda i, lay, be, *_: (lay[0], be[i], 0, 0))

    grid_spec = pltpu.PrefetchScalarGridSpec(
        num_scalar_prefetch=7,
        grid=(n_blocks,),
        in_specs=[pl.BlockSpec(memory_space=pl.ANY), wspec((d, f)), wspec((d, f)), wspec((f, d))],
        out_specs=pl.BlockSpec(memory_space=pl.ANY),
        scratch_shapes=[pltpu.VMEM((d, f), BF16), pltpu.VMEM((d, f), BF16), pltpu.VMEM((f, d), BF16),
                        pltpu.VMEM((2, bm * SUBLANES, LANES), F32), pltpu.VMEM((2, bm * SUBLANES, LANES), F32),
                        pltpu.SemaphoreType.DMA((2,)), pltpu.SemaphoreType.DMA((2,))],
    )
    return pl.pallas_call(
        functools.partial(_ffn_kernel, n_spare_rows=n_spare * SUBLANES),
        grid_spec=grid_spec,
        out_shape=jax.ShapeDtypeStruct(((n_tok * TOP_K + n_spare) * SUBLANES, LANES), F32),
        compiler_params=_cparams("arbitrary"),
    )(layer, blk_e, n_used, blk_start, blk_cnt, src_rows, dst_rows, h2t, wg, wu, wd)


def _combine_kernel(x_ref, y2_ref, rt_ref, ml_ref, mc_ref, o_ref, *, tm, tiles_per_b, n_ctx):
    is_ctx = _is_ctx_rows(tm, tiles_per_b, n_ctx)
    rt = rt_ref[...]
    step = TOP_K * SUBLANES
    y0 = jnp.concatenate([y2_ref[pl.ds(k, tm, stride=step), :] for k in range(SUBLANES)], axis=1)
    y1 = jnp.concatenate([y2_ref[pl.ds(SUBLANES + k, tm, stride=step), :] for k in range(SUBLANES)], axis=1)
    y = rt[:, 0:1] * y0 + rt[:, 1:2] * y1
    o_ref[...] = x_ref[...] + _mod_rows(ml_ref, mc_ref, 5, is_ctx) * y


def _combine(x2, y2, route, mod_l, mod_c, t_len, n_ctx):
    m, d = x2.shape
    tm = _row_tile(t_len)
    tpb = t_len // tm
    kern = functools.partial(_combine_kernel, tm=tm, tiles_per_b=tpb, n_ctx=n_ctx)
    return pl.pallas_call(
        kern,
        grid=(m // tm,),
        in_specs=[pl.BlockSpec((tm, d), lambda i: (i, 0)),
                  pl.BlockSpec((tm * TOP_K * SUBLANES, LANES), lambda i: (i, 0)),
                  pl.BlockSpec((tm, LANES), lambda i: (i, 0)),
                  pl.BlockSpec((1, N_MOD, d), lambda i: (i // tpb, 0, 0)),
                  pl.BlockSpec((1, N_MOD, d), lambda i: (0, 0, 0))],
        out_specs=pl.BlockSpec((tm, d), lambda i: (i, 0)),
        out_shape=jax.ShapeDtypeStruct((m, d), F32),
        compiler_params=_cparams("parallel"),
    )(x2, y2, route, mod_l, mod_c)


def _final_norm_kernel(x_ref, g_ref, o_ref):
    o_ref[0] = _rmsnorm_rows(x_ref[0], g_ref[...])


def _final_norm(x3, g, n_ctx):
    bsz, t_len, d = x3.shape
    n_lat = t_len - n_ctx
    tm = n_ctx
    off = n_ctx // tm
    return pl.pallas_call(
        _final_norm_kernel,
        grid=(bsz, n_lat // tm),
        in_specs=[pl.BlockSpec((1, tm, d), lambda b, j: (b, j + off, 0)),
                  pl.BlockSpec((1, d), lambda b, j: (0, 0))],
        out_specs=pl.BlockSpec((1, tm, d), lambda b, j: (b, j, 0)),
        out_shape=jax.ShapeDtypeStruct((bsz, n_lat, d), F32),
        compiler_params=_cparams("parallel", "parallel"),
    )(x3, g.reshape(1, d))


def _dispatch_tables(eid, n_tok):
    a = n_tok * TOP_K
    bm = MOE_BLOCK
    pad = ROW_GROUP - 1
    flat_e = eid.reshape(a)
    experts = jnp.arange(N_EXPERTS, dtype=jnp.int32)
    counts = jnp.sum(flat_e[:, None] == experts[None, :], axis=0, dtype=jnp.int32)
    nblk_e = (counts + bm - 1) // bm
    blk_end = jnp.cumsum(nblk_e)
    blk_first = blk_end - nblk_e
    keys = jnp.concatenate([flat_e, jnp.repeat(experts, pad)])
    id_bits = (keys.shape[0] - 1).bit_length()
    packed = jnp.sort(keys * (1 << id_bits) + jnp.arange(keys.shape[0], dtype=jnp.int32))
    order = packed & ((1 << id_bits) - 1)
    seg = counts + pad
    seg_start = jnp.cumsum(seg) - seg
    src_rows = jnp.where(order < a, lax.shift_right_logical(order, 1), 0) * SUBLANES
    dst_rows = order * SUBLANES
    n_blocks = -(-a // bm) + N_EXPERTS
    blk = jnp.arange(n_blocks + 1, dtype=jnp.int32)
    blk_e = jnp.minimum(jnp.sum(blk_end[None, :] <= blk[:, None], axis=1, dtype=jnp.int32), N_EXPERTS - 1)
    within = (blk - blk_first[blk_e]) * bm
    n_used = blk_end[-1].reshape(1)
    blk_start = jnp.clip(seg_start[blk_e] + within, 0, order.shape[0] - 1)
    blk_cnt = jnp.where(blk < n_used[0], jnp.clip(counts[blk_e] - within, 0, bm), 0)
    return src_rows, dst_rows, blk_e, blk_start, blk_cnt, n_used


def _moe(layer, h2t, route, wg, wu, wd):
    n_tok = route.shape[0]
    eid = route[:, 2:4].astype(jnp.int32)
    src_rows, dst_rows, blk_e, blk_start, blk_cnt, n_used = _dispatch_tables(eid, n_tok)
    return _expert_ffn(layer, blk_e, n_used, blk_start, blk_cnt, src_rows, dst_rows, h2t, wg, wu, wd, n_tok)


def kernel(x, c, ctx, c_ctx, w_ada, b_ada, norm1_g, norm2_g, w_in, s5_lam_re, s5_lam_im, s5_log_dt, s5_b_re, s5_b_im, s5_c_re, s5_c_im, s5_d, s5_w_glu, hgrn_lb_raw, w_branch_s5, w_branch_hgrn, w_branch_ret, w_out, moe_w_group, moe_b_group, moe_w_expert, moe_b_expert, moe_w_gate, moe_w_up, moe_w_down, final_norm_g):
    bsz, n_lat, d = x.shape
    n_ctx = ctx.shape[1]
    t_len = n_ctx + n_lat
    depth = w_ada.shape[0]
    m = bsz * t_len
    w_s5 = s5_d.shape[1]
    w_hg = w_branch_hgrn.shape[1]
    w_ret = w_branch_ret.shape[1]
    n_heads = w_ret // HEAD_DIM

    pad_rows = (-(bsz + 1)) % SUBLANES
    cc = jnp.concatenate([c, c_ctx[None, :], jnp.zeros((pad_rows, d), F32)], axis=0)
    mods = _ada_tables(cc, w_ada, b_ada).reshape(depth, bsz + 1 + pad_rows, N_MOD, d)

    p_lb = jax.nn.softmax(hgrn_lb_raw.astype(F32), axis=0)
    lower_bounds = jnp.cumsum(p_lb, axis=0) - p_lb[0:1]
    ret_consts = _ret_constants(n_heads, t_len, n_ctx)
    ones_bd = jnp.kron(jnp.eye(w_hg // HEAD_DIM, dtype=F32), jnp.ones((HEAD_DIM, HEAD_DIM), F32)).astype(BF16)

    xs = jnp.concatenate([ctx, x], axis=1).reshape(m, d)
    off_hg = w_s5
    off_ret = off_hg + 5 * w_hg
    off_gz = off_ret + 4 * w_ret
    w_in_bf = w_in.astype(BF16)
    s5w = jax.vmap(_s5_weights)(s5_lam_re, s5_lam_im, s5_log_dt, s5_b_re, s5_b_im, s5_c_re, s5_c_im, s5_d)
    for l in range(depth):
        mod_l = mods[l, :bsz]
        mod_c = mods[l, bsz:bsz + 1]
        u, hg, ret, gz = _project(xs, norm1_g[l], mod_l, mod_c, w_in_bf, l,
                                  (0, off_hg, off_ret, off_gz, w_in.shape[2]), t_len, n_ctx)

        ys5 = _s5_mix(u.reshape(bsz, t_len, w_s5), *(w[l] for w in s5w), n_ctx).reshape(m, w_s5)
        hg_of, hg_ob = _hg_mix(hg.reshape(bsz, t_len, 5 * w_hg), lower_bounds[l], n_ctx)
        ret_of, ret_ob = _ret_mix(ret.reshape(bsz, t_len, 4 * w_ret), ret_consts, n_ctx)

        wr = jnp.concatenate([moe_w_group[l], moe_w_expert[l],
                              jnp.zeros((d, LANES - N_GROUPS - N_EXPERTS), F32)], axis=1)
        br = jnp.concatenate([moe_b_group[l], moe_b_expert[l],
                              jnp.zeros((LANES - N_GROUPS - N_EXPERTS,), F32)])[None, :]
        wr_hi = wr.astype(BF16)
        wr = jnp.concatenate([wr_hi, (wr - wr_hi.astype(F32)).astype(BF16)], axis=1)
        wts = (s5_w_glu[l].astype(BF16), w_branch_s5[l].astype(BF16), w_branch_hgrn[l].astype(BF16),
               w_branch_ret[l].astype(BF16), w_out[l].astype(BF16), ones_bd, wr, br)
        xs, h2, route = _merge(xs, ys5, hg_of.reshape(m, w_hg), hg_ob.reshape(m, w_hg), hg,
                               ret_of.reshape(m, w_ret), ret_ob.reshape(m, w_ret), ret, gz,
                               mod_l, mod_c, norm2_g[l].reshape(1, d), wts, t_len, n_ctx)
        y2 = _moe(jnp.full((1,), l, jnp.int32), h2, route, moe_w_gate, moe_w_up, moe_w_down)
        xs = _combine(xs, y2, route, mod_l, mod_c, t_len, n_ctx)
    return _final_norm(xs.reshape(bsz, t_len, d), final_norm_g, n_ctx)
```

```python
import functools

import jax
import jax.numpy as jnp
import numpy as np
from jax import lax
from jax.experimental import pallas as pl
from jax.experimental.pallas import tpu as pltpu

F32 = jnp.float32
BF16 = jnp.bfloat16
HIGHEST = lax.Precision.HIGHEST

LANES = 128
SUBLANES = 8
VMEM_LIMIT = 56 * 1024 * 1024

EPS = 1e-6
N_MOD = 6
GRID_W = 64
HEAD_DIM = 64
PAIR = 2 * HEAD_DIM
S5_GROUP = 16
S5_STATE = 64
S5_CHUNK = 16
HG_CHUNK = 64
HG_HALF = HG_CHUNK // 2
HG_BLOCK = 256
RET_BLOCK = 256
ROPE_BASE = 10000.0
N_GROUPS = 4
EXPERTS_PER_GROUP = 8
N_EXPERTS = N_GROUPS * EXPERTS_PER_GROUP
TOP_K = 2
MOE_BLOCK = 512
ROW_GROUP = 8


def _cparams(*sem):
    return pltpu.CompilerParams(dimension_semantics=sem, vmem_limit_bytes=VMEM_LIMIT)


def _row_tile(t_len):
    for tm in (544, 512, 384, 272, 256, 128, 64, 32, 16, 8):
        if t_len % tm == 0:
            return tm
    raise ValueError(f"unsupported stream length {t_len}")


def _ada_kernel(c_ref, w_ref, b_ref, o_ref):
    a = jax.nn.silu(c_ref[...]).astype(BF16)
    o_ref[0] = jnp.dot(a, w_ref[0].astype(BF16), preferred_element_type=F32) + b_ref[0]


def _ada_tables(cc, w_ada, b_ada):
    depth, d, n = w_ada.shape
    rows = cc.shape[0]
    tn = 1536 if n % 1536 == 0 else n
    return pl.pallas_call(
        _ada_kernel,
        grid=(depth, n // tn),
        in_specs=[
            pl.BlockSpec((rows, d), lambda l, j: (0, 0)),
            pl.BlockSpec((1, d, tn), lambda l, j: (l, 0, j)),
            pl.BlockSpec((1, 1, tn), lambda l, j: (l, 0, j)),
        ],
        out_specs=pl.BlockSpec((1, rows, tn), lambda l, j: (l, 0, j)),
        out_shape=jax.ShapeDtypeStruct((depth, rows, n), F32),
        compiler_params=_cparams("parallel", "parallel"),
    )(cc, w_ada, b_ada.reshape(depth, 1, n))


def _mod_rows(ml_ref, mc_ref, idx, is_ctx):
    return jnp.where(is_ctx, mc_ref[0, idx:idx + 1, :], ml_ref[0, idx:idx + 1, :])


def _is_ctx_rows(tm, tiles_per_b, n_ctx):
    jt = pl.program_id(0) % tiles_per_b
    row = jt * tm + lax.broadcasted_iota(jnp.int32, (tm, 1), 0)
    return row < n_ctx


def _store_token_tiles(ref, val):
    n_rows = val.shape[0]
    for k in range(val.shape[1] // LANES):
        ref[pl.ds(k, n_rows, stride=SUBLANES), :] = val[:, LANES * k:LANES * (k + 1)]


def _load_token_tiles(ref):
    n_rows = ref.shape[0] // SUBLANES
    return jnp.concatenate([ref[pl.ds(k, n_rows, stride=SUBLANES), :] for k in range(SUBLANES)], axis=1)


def _rmsnorm_rows(x, g):
    return x * lax.rsqrt(jnp.mean(x * x, axis=-1, keepdims=True) + EPS) * g


def _proj_kernel(x_ref, g_ref, ml_ref, mc_ref, w_ref, *out_refs, tm, tiles_per_b, n_ctx, offsets):
    is_ctx = _is_ctx_rows(tm, tiles_per_b, n_ctx)
    y = _rmsnorm_rows(x_ref[...], g_ref[...])
    y = y * (1.0 + _mod_rows(ml_ref, mc_ref, 1, is_ctx)) + _mod_rows(ml_ref, mc_ref, 0, is_ctx)
    h = y.astype(BF16)
    for k, o_ref in enumerate(out_refs):
        w = w_ref[0, :, offsets[k]:offsets[k + 1]]
        o_ref[...] = jnp.dot(h, w, preferred_element_type=F32).astype(o_ref.dtype)


def _project(x2, g, mod_l, mod_c, w_all, layer, offsets, t_len, n_ctx):
    m, d = x2.shape
    tm = _row_tile(t_len)
    tpb = t_len // tm
    kern = functools.partial(_proj_kernel, tm=tm, tiles_per_b=tpb, n_ctx=n_ctx, offsets=offsets)
    widths = [offsets[k + 1] - offsets[k] for k in range(len(offsets) - 1)]
    out_dtypes = (F32, F32, BF16, BF16)
    return pl.pallas_call(
        kern,
        grid=(m // tm,),
        in_specs=[
            pl.BlockSpec((tm, d), lambda i: (i, 0)),
            pl.BlockSpec((1, d), lambda i: (0, 0)),
            pl.BlockSpec((1, N_MOD, d), lambda i: (i // tpb, 0, 0)),
            pl.BlockSpec((1, N_MOD, d), lambda i: (0, 0, 0)),
            pl.BlockSpec((1,) + w_all.shape[1:], lambda i: (layer, 0, 0), pipeline_mode=pl.Buffered(1)),
        ],
        out_specs=[pl.BlockSpec((tm, w), lambda i: (i, 0)) for w in widths],
        out_shape=[jax.ShapeDtypeStruct((m, w), dt) for w, dt in zip(widths, out_dtypes)],
        compiler_params=_cparams("parallel"),
    )(x2, g.reshape(1, d), mod_l, mod_c, w_all)


def _s5_weights(lam_re, lam_im, log_dt, b_re, b_im, c_re, c_im, d_skip):
    n_dir, g_n, p_n = lam_re.shape
    h_n = b_re.shape[-1]
    cn = S5_CHUNK
    lr, li = lam_re.astype(F32), lam_im.astype(F32)
    dt = jnp.exp(log_dt.astype(F32))[..., None]
    mag = jnp.exp(lr * dt)
    ar = mag * jnp.cos(li * dt)
    ai = mag * jnp.sin(li * dt)
    den = lr * lr + li * li
    zr = ((ar - 1.0) * lr + ai * li) / den
    zi = (ai * lr - (ar - 1.0) * li) / den
    bbr = zr[..., None] * b_re - zi[..., None] * b_im
    bbi = zr[..., None] * b_im + zi[..., None] * b_re
    j = jnp.arange(cn + 1, dtype=F32)[:, None, None, None]
    pmag = jnp.exp(lr[None] * dt[None] * j)
    pang = li[None] * dt[None] * j
    pr = pmag * jnp.cos(pang)
    pi = pmag * jnp.sin(pang)
    wr = pr[..., None] * bbr[None] - pi[..., None] * bbi[None]
    wi = pr[..., None] * bbi[None] + pi[..., None] * bbr[None]
    kj = (jnp.einsum('dghp,jdgpk->jdghk', c_re, wr, precision=HIGHEST)
          - jnp.einsum('dghp,jdgpk->jdghk', c_im, wi, precision=HIGHEST))
    s_idx = jnp.arange(cn)[:, None]
    t_idx = jnp.arange(cn)[None, :]
    lag = t_idx - s_idx
    kf = kj[:cn, 0][jnp.clip(lag, 0, cn - 1)]
    kb = kj[:cn, 1][jnp.clip(-lag, 0, cn - 1)]
    kf = jnp.where((lag >= 0)[:, :, None, None, None], kf, 0.0)
    kb = jnp.where((lag <= 0)[:, :, None, None, None], kb, 0.0)
    dsk = d_skip.astype(F32).reshape(g_n, h_n)
    eye_h = jnp.eye(h_n, dtype=F32)
    diag = (lag == 0)[:, :, None, None, None] * (dsk[None, None, :, :, None] * eye_h[None, None, None])
    m_full = kf + kb + diag
    m_mat = m_full.transpose(2, 0, 4, 1, 3).reshape(g_n, cn * h_n, cn * h_n)
    wf_r = wr[:cn, 0][::-1]
    wf_i = wi[:cn, 0][::-1]
    wb_r = wr[:cn, 1]
    wb_i = wi[:cn, 1]
    w_in = jnp.stack([wf_r, wf_i, wb_r, wb_i], axis=0)
    w_in = w_in.transpose(2, 1, 4, 0, 3).reshape(g_n, cn * h_n, 4 * p_n)
    def out_map(d, powers):
        prd, pid = pr[powers, d], pi[powers, d]
        w_re = c_re[d][None] * prd[:, :, None, :] - c_im[d][None] * pid[:, :, None, :]
        w_im = c_re[d][None] * pid[:, :, None, :] + c_im[d][None] * prd[:, :, None, :]
        return w_re, -w_im
    t_arr = jnp.arange(cn)
    of_r, of_i = out_map(0, t_arr + 1)
    ob_r, ob_i = out_map(1, cn - t_arr)
    w_out = jnp.stack([of_r, of_i, ob_r, ob_i], axis=0)
    w_out = w_out.transpose(2, 0, 4, 1, 3).reshape(g_n, 4 * p_n, cn * h_n)
    slot = jax.nn.one_hot(jnp.arange(g_n) % 2, 2, dtype=F32)
    w_in = w_in.reshape(g_n, cn * h_n, 4, 1, p_n) * slot[:, None, None, :, None]
    w_in = w_in.reshape(g_n, cn * h_n, 8 * p_n)
    w_out = w_out.reshape(g_n, 4, 1, p_n, cn * h_n) * slot[:, None, :, None, None]
    w_out = w_out.reshape(g_n, 8 * p_n, cn * h_n)
    a_mat = jnp.stack([pr[cn, 0].reshape(-1), pi[cn, 0].reshape(-1),
                       pr[cn, 1].reshape(-1), pi[cn, 1].reshape(-1)], axis=0)
    w1 = jnp.concatenate([m_mat, w_in], axis=-1).astype(BF16)
    return w1, w_out.astype(BF16), a_mat


def _s5_fold_perm():
    j, g8, h = np.meshgrid(np.arange(8), np.arange(8), np.arange(S5_GROUP), indexing='ij')
    src = (j * 8 + g8) * S5_GROUP + h
    dst = (g8 * 8 + j) * S5_GROUP + h
    perm = np.zeros((8 * LANES, 8 * LANES), np.float32)
    perm[src.reshape(-1), dst.reshape(-1)] = 1.0
    return jnp.asarray(perm, dtype=BF16)


def _s5_kernel(u_ref, perm_ref, w1_ref, wo_ref, a_ref, y_ref, ug_scr, yg_scr, xfr, xfi, xbr, xbi,
               *, n_chunks, n_ctx_chunks, n_groups):
    cn = S5_CHUNK
    blk = cn * S5_GROUP
    tok_per_slab = LANES // S5_GROUP
    for c in range(cn // tok_per_slab):
        slabs = [u_ref[0, pl.ds(tok_per_slab * c + j, n_chunks, stride=cn), :] for j in range(tok_per_slab)]
        s = jnp.concatenate(slabs, axis=1).astype(BF16)
        up = jnp.dot(s, perm_ref[...], preferred_element_type=F32).astype(BF16)
        for g in range(n_groups):
            ug_scr[g, :, LANES * c:LANES * (c + 1)] = up[:, LANES * g:LANES * (g + 1)]
    xs = (xfr, xfi, xbr, xbi)
    for g in range(n_groups):
        z = jnp.dot(ug_scr[g], w1_ref[g], preferred_element_type=F32)
        yg_scr[g] = z[:, :blk]
        lanes = slice(LANES * (g // 2), LANES * (g // 2 + 1))
        for k, xr in enumerate(xs):
            piece = z[:, blk + LANES * k:blk + LANES * (k + 1)]
            if g % 2 == 0:
                xr[:, lanes] = piece
            else:
                xr[:, lanes] += piece
    arf, aif, arb, aib = a_ref[0:1, :], a_ref[1:2, :], a_ref[2:3, :], a_ref[3:4, :]

    def body(i, carry):
        fr, fi, br, bi = carry
        nb = jnp.where(i < n_ctx_chunks, n_ctx_chunks - 1 - i, n_chunks - 1 - (i - n_ctx_chunks))
        x_fr, x_fi = xfr[pl.ds(i, 1), :], xfi[pl.ds(i, 1), :]
        x_br, x_bi = xbr[pl.ds(nb, 1), :], xbi[pl.ds(nb, 1), :]
        xfr[pl.ds(i, 1), :] = fr
        xfi[pl.ds(i, 1), :] = fi
        xbr[pl.ds(nb, 1), :] = br
        xbi[pl.ds(nb, 1), :] = bi
        return (arf * fr - aif * fi + x_fr, arf * fi + aif * fr + x_fi,
                arb * br - aib * bi + x_br, arb * bi + aib * br + x_bi)

    zero = jnp.zeros((1, xfr.shape[1]), F32)
    lax.fori_loop(0, n_chunks, body, (zero, zero, zero, zero))
    for g in range(n_groups):
        lanes = slice(LANES * (g // 2), LANES * (g // 2 + 1))
        hp = jnp.concatenate([xr[:, lanes] for xr in xs], axis=1).astype(BF16)
        yg_scr[g] += jnp.dot(hp, wo_ref[g], preferred_element_type=F32)
    for c in range(cn // tok_per_slab):
        ycat = jnp.concatenate([yg_scr[g, :, LANES * c:LANES * (c + 1)] for g in range(n_groups)], axis=1)
        hi = ycat.astype(BF16)
        lo = (ycat - hi.astype(F32)).astype(BF16)
        r = _dot_nt(hi, perm_ref[...]) + _dot_nt(lo, perm_ref[...])
        for j in range(tok_per_slab):
            y_ref[0, pl.ds(tok_per_slab * c + j, n_chunks, stride=cn), :] = r[:, LANES * j:LANES * (j + 1)]


def _s5_mix(u, w1, w_out, a_mat, n_ctx):
    bsz, t_len, width = u.shape
    g_n = width // S5_GROUP
    cn = S5_CHUNK
    n_chunks = t_len // cn
    blk = cn * S5_GROUP
    perm = _s5_fold_perm()
    gs = LANES // S5_GROUP
    state_w = gs * S5_STATE
    kern = functools.partial(_s5_kernel, n_chunks=n_chunks, n_ctx_chunks=n_ctx // cn, n_groups=gs)
    return pl.pallas_call(
        kern,
        grid=(bsz, g_n // gs),
        in_specs=[pl.BlockSpec((1, t_len, LANES), lambda b, s: (b, 0, s)),
                  pl.BlockSpec(perm.shape, lambda b, s: (0, 0), pipeline_mode=pl.Buffered(1)),
                  pl.BlockSpec((gs,) + w1.shape[1:], lambda b, s: (s, 0, 0)),
                  pl.BlockSpec((gs,) + w_out.shape[1:], lambda b, s: (s, 0, 0)),
                  pl.BlockSpec((4, state_w), lambda b, s: (0, s))],
        out_specs=pl.BlockSpec((1, t_len, LANES), lambda b, s: (b, 0, s)),
        out_shape=jax.ShapeDtypeStruct((bsz, t_len, width), F32),
        scratch_shapes=[pltpu.VMEM((gs, n_chunks, blk), BF16), pltpu.VMEM((gs, n_chunks, blk), F32)]
        + [pltpu.VMEM((n_chunks, state_w), F32)] * 4,
        compiler_params=_cparams("parallel", "parallel"),
    )(u, perm, w1, w_out, a_mat)


def _head0_lanes(shape):
    return lax.broadcasted_iota(jnp.int32, shape, len(shape) - 1) < HEAD_DIM


def _stack_heads(q):
    m0 = _head0_lanes(q.shape)
    return jnp.concatenate([jnp.where(m0, q, 0.0), jnp.where(m0, 0.0, q)], axis=-2)


def _unstack_heads(o2):
    c = o2.shape[-2] // 2
    top, bottom = o2[..., :c, :], o2[..., c:, :]
    return jnp.where(_head0_lanes(top.shape), top, bottom)


def _pair_block_mask():
    r = lax.broadcasted_iota(jnp.int32, (PAIR, PAIR), 0) < HEAD_DIM
    c = lax.broadcasted_iota(jnp.int32, (PAIR, PAIR), 1) < HEAD_DIM
    return r == c


def _dot_nt(a, b):
    return lax.dot_general(a, b, (((1,), (1,)), ((), ())), preferred_element_type=F32)


def _dot_tn(a, b):
    return lax.dot_general(a, b, (((0,), (0,)), ((), ())), preferred_element_type=F32)


def _bwd_block(j, n_ctx_blocks, n_blocks):
    return jnp.where(j < n_ctx_blocks, n_ctx_blocks - 1 - j, n_blocks - 1 - (j - n_ctx_blocks))


def _bdot_nt(a, b):
    return lax.dot_general(a, b, (((2,), (2,)), ((0,), (0,))), preferred_element_type=F32)


def _bdot_nn(a, b):
    return lax.dot_general(a, b, (((2,), (1,)), ((0,), (0,))), preferred_element_type=F32)


def _hg_block(q, z, v, lb, ht, reverse):
    rows = q.shape[0]
    c = HG_CHUNK
    hh = HG_HALF
    n = rows // c
    f = lb + (1.0 - lb) * jax.nn.sigmoid(z)
    logf = jnp.log(f)
    kk = 1.0 - f
    ri = lax.broadcasted_iota(jnp.int32, (n, c, c), 1)
    ci = lax.broadcasted_iota(jnp.int32, (n, c, c), 2)
    tri = jnp.where((ci >= ri) if reverse else (ci <= ri), 1.0, 0.0).astype(BF16)
    l1 = logf.astype(BF16)
    r1 = logf - l1.astype(F32)
    l2 = r1.astype(BF16)
    l3 = (r1 - l2.astype(F32)).astype(BF16)
    b3 = _bdot_nn(tri, jnp.concatenate([l1, l2, l3], axis=1).reshape(n, c, 3 * PAIR))
    b4 = b3[..., 0:PAIR] + b3[..., PAIR:2 * PAIR] + b3[..., 2 * PAIR:3 * PAIR]
    q4, k4 = q.reshape(n, c, PAIR), kk.reshape(n, c, PAIR)
    vb = v.astype(BF16).reshape(n, c, PAIR)
    if reverse:
        first, second = slice(hh, c), slice(0, hh)
        r = b4[:, hh:hh + 1]
        bend = b4[:, 0:1]
    else:
        first, second = slice(0, hh), slice(hh, c)
        r = b4[:, hh - 1:hh]
        bend = b4[:, c - 1:c]
    b1, b2 = b4[:, first], b4[:, second]
    qd1 = q4[:, first] * jnp.exp(b1)
    qd2 = q4[:, second] * jnp.exp(b2 - r)
    k1d = k4[:, first] * jnp.exp(-b1)
    k2 = k4 * jnp.exp(r - b4)
    s1 = _bdot_nt(_stack_heads(qd1).astype(BF16), k1d.astype(BF16))
    s2 = _bdot_nt(_stack_heads(qd2).astype(BF16), k2.astype(BF16))
    t1 = lax.broadcasted_iota(jnp.int32, (n, 2 * hh, hh), 1) % hh
    c1 = lax.broadcasted_iota(jnp.int32, (n, 2 * hh, hh), 2)
    t2 = lax.broadcasted_iota(jnp.int32, (n, 2 * hh, c), 1) % hh
    c2 = lax.broadcasted_iota(jnp.int32, (n, 2 * hh, c), 2)
    if reverse:
        s1 = jnp.where(c1 >= t1, s1, 0.0)
        s2 = jnp.where(c2 >= t2, s2, 0.0)
    else:
        s1 = jnp.where(c1 <= t1, s1, 0.0)
        s2 = jnp.where(c2 <= t2 + hh, s2, 0.0)
    o1 = _unstack_heads(_bdot_nn(s1.astype(BF16), vb[:, first]))
    o2 = _unstack_heads(_bdot_nn(s2.astype(BF16), vb))
    o = jnp.concatenate([o2, o1] if reverse else [o1, o2], axis=1)
    ke = (k4 * jnp.exp(bend - b4)).astype(BF16)
    dec = jnp.exp(bend)
    bmask = _pair_block_mask()
    states = [None] * n
    for ch in (range(n - 1, -1, -1) if reverse else range(n)):
        states[ch] = ht.astype(BF16)
        ht = ht * dec[ch] + jnp.where(bmask, _dot_tn(vb[ch], ke[ch]), 0.0)
    o = o + _bdot_nt((q4 * jnp.exp(b4)).astype(BF16), jnp.stack(states, axis=0))
    return o.reshape(rows, PAIR), ht


def _hg_kernel(qf_ref, zf_ref, vf_ref, qb_ref, zb_ref, vb_ref, lb_ref, of_ref, ob_ref, hf_scr, hb_scr, *, n_pairs):
    @pl.when(pl.program_id(1) == 0)
    def _():
        hf_scr[...] = jnp.zeros_like(hf_scr)
        hb_scr[...] = jnp.zeros_like(hb_scr)

    for p in range(n_pairs):
        sl = slice(p * PAIR, (p + 1) * PAIR)
        o, hn = _hg_block(qf_ref[0, :, sl], zf_ref[0, :, sl], vf_ref[0, :, sl], lb_ref[0:1, sl], hf_scr[p], False)
        of_ref[0, :, sl] = o
        hf_scr[p] = hn
        o, hn = _hg_block(qb_ref[0, :, sl], zb_ref[0, :, sl], vb_ref[0, :, sl], lb_ref[1:2, sl], hb_scr[p], True)
        ob_ref[0, :, sl] = o
        hb_scr[p] = hn


def _hg_mix(hg, lb, n_ctx):
    bsz, t_len, total = hg.shape
    w = total // 5
    n_pairs = w // PAIR
    c = HG_BLOCK
    nc, ncc = t_len // c, n_ctx // c
    bw = functools.partial(_bwd_block, n_ctx_blocks=ncc, n_blocks=nc)

    def fspec(col):
        return pl.BlockSpec((1, c, w), lambda b, j: (b, j, col))

    def bspec(col):
        return pl.BlockSpec((1, c, w), lambda b, j: (b, bw(j), col))

    kern = functools.partial(_hg_kernel, n_pairs=n_pairs)
    return pl.pallas_call(
        kern,
        grid=(bsz, nc),
        in_specs=[fspec(0), fspec(1), fspec(3), bspec(0), bspec(2), bspec(3),
                  pl.BlockSpec((2, w), lambda b, j: (0, 0))],
        out_specs=[pl.BlockSpec((1, c, w), lambda b, j: (b, j, 0)),
                   pl.BlockSpec((1, c, w), lambda b, j: (b, bw(j), 0))],
        out_shape=[jax.ShapeDtypeStruct((bsz, t_len, w), F32)] * 2,
        scratch_shapes=[pltpu.VMEM((n_pairs, PAIR, PAIR), F32)] * 2,
        compiler_params=_cparams("parallel", "arbitrary"),
    )(hg, hg, hg, hg, hg, hg, lb)


def _rope_rows(x, cos, sin):
    hd2 = HEAD_DIM // 2
    lane = lax.broadcasted_iota(jnp.int32, x.shape, 1) % HEAD_DIM
    swapped = jnp.where(lane < hd2, pltpu.roll(x, PAIR - hd2, 1), pltpu.roll(x, hd2, 1))
    return x * cos + swapped * sin


def _ret_kernel(qf_ref, kf_ref, vf_ref, cf_ref, sf_ref, qb_ref, kb_ref, vb_ref, cb_ref, sb_ref,
                d_ref, gqf_ref, gkf_ref, gqb_ref, gkb_ref, gh_ref, of_ref, ob_ref, hf_scr, hb_scr, *, n_pairs):
    @pl.when(pl.program_id(1) == 0)
    def _():
        hf_scr[...] = jnp.zeros_like(hf_scr)
        hb_scr[...] = jnp.zeros_like(hb_scr)

    kscale = HEAD_DIM ** -0.5
    bmask = _pair_block_mask()
    for p in range(n_pairs):
        sl = slice(p * PAIR, (p + 1) * PAIR)
        q = _rope_rows(qf_ref[0, :, sl].astype(F32), cf_ref[...], sf_ref[...])
        k = _rope_rows(kf_ref[0, :, sl].astype(F32) * kscale, cf_ref[...], sf_ref[...])
        vb16 = vf_ref[0, :, sl].astype(BF16)
        s = _dot_nt(_stack_heads(q).astype(BF16), k.astype(BF16)) * d_ref[p]
        o = _unstack_heads(jnp.dot(s.astype(BF16), vb16, preferred_element_type=F32))
        ht = hf_scr[p]
        o = o + _dot_nt((q * gqf_ref[:, sl]).astype(BF16), ht.astype(BF16))
        upd = _dot_tn(vb16, (k * gkf_ref[:, sl]).astype(BF16))
        hf_scr[p] = ht * gh_ref[p] + jnp.where(bmask, upd, 0.0)
        of_ref[0, :, sl] = o

        q = _rope_rows(qb_ref[0, :, sl].astype(F32), cb_ref[...], sb_ref[...])
        k = _rope_rows(kb_ref[0, :, sl].astype(F32) * kscale, cb_ref[...], sb_ref[...])
        vb16 = vb_ref[0, :, sl].astype(BF16)
        ht = hb_scr[p]
        ob_ref[0, :, sl] = _dot_nt((q * gqb_ref[:, sl]).astype(BF16), ht.astype(BF16))
        upd = _dot_tn(vb16, (k * gkb_ref[:, sl]).astype(BF16))
        hb_scr[p] = ht * gh_ref[p] + jnp.where(bmask, upd, 0.0)


def _ret_constants(n_heads, t_len, n_ctx):
    c = RET_BLOCK
    log_gamma = jnp.log(1.0 - 2.0 ** (-5.0 - jnp.arange(n_heads, dtype=F32)))
    t = jnp.arange(c, dtype=F32)
    dist = jnp.abs(t[:, None] - t[None, :])
    dmat = jnp.exp(log_gamma[:, None, None] * dist[None])
    dmat = dmat.reshape(n_heads // 2, 2 * c, c)
    lg_lane = jnp.repeat(log_gamma, HEAD_DIM)[None, :]
    gqf = jnp.exp(lg_lane * (t[:, None] + 1.0))
    gkf = jnp.exp(lg_lane * (c - 1.0 - t[:, None]))
    gqb = jnp.exp(lg_lane * (c - t[:, None]))
    gkb = jnp.exp(lg_lane * t[:, None])
    gh = jnp.exp(lg_lane * float(c)).reshape(n_heads // 2, 1, PAIR)
    gh = jnp.broadcast_to(gh, (n_heads // 2, PAIR, PAIR))
    n_lat = t_len - n_ctx
    rows = n_lat // GRID_W
    row = jnp.repeat(jnp.arange(rows, dtype=F32), GRID_W)
    col = jnp.broadcast_to(jnp.arange(GRID_W, dtype=F32)[None, :], (rows, GRID_W)).reshape(-1)
    n_freq = HEAD_DIM // 4
    inv_freq = ROPE_BASE ** (-jnp.arange(n_freq, dtype=F32) / n_freq)
    ang = jnp.concatenate([row[:, None] * inv_freq, col[:, None] * inv_freq], axis=-1)
    cos_l, sin_l = jnp.cos(ang), jnp.sin(ang)
    cos_h = jnp.concatenate([cos_l, cos_l], axis=-1)
    sin_h = jnp.concatenate([-sin_l, sin_l], axis=-1)
    cos_t = jnp.concatenate([jnp.ones((n_ctx, HEAD_DIM), F32), cos_h], axis=0)
    sin_t = jnp.concatenate([jnp.zeros((n_ctx, HEAD_DIM), F32), sin_h], axis=0)
    cos_t = jnp.concatenate([cos_t, cos_t], axis=-1)
    sin_t = jnp.concatenate([sin_t, sin_t], axis=-1)
    return dmat, gqf, gkf, gqb, gkb, gh, cos_t, sin_t


def _ret_mix(ret, consts, n_ctx):
    dmat, gqf, gkf, gqb, gkb, gh, cos_t, sin_t = consts
    bsz, t_len, total = ret.shape
    w = total // 4
    n_pairs = w // PAIR
    c = RET_BLOCK
    nb, ncb = t_len // c, n_ctx // c
    bw = functools.partial(_bwd_block, n_ctx_blocks=ncb, n_blocks=nb)

    def fspec(col):
        return pl.BlockSpec((1, c, w), lambda b, j: (b, j, col))

    def bspec(col):
        return pl.BlockSpec((1, c, w), lambda b, j: (b, bw(j), col))

    def const(a):
        nd = a.ndim
        return pl.BlockSpec(a.shape, lambda b, j: (0,) * nd)

    rope_f = pl.BlockSpec((c, PAIR), lambda b, j: (j, 0))
    rope_b = pl.BlockSpec((c, PAIR), lambda b, j: (bw(j), 0))
    kern = functools.partial(_ret_kernel, n_pairs=n_pairs)
    return pl.pallas_call(
        kern,
        grid=(bsz, nb),
        in_specs=[fspec(0), fspec(1), fspec(2), rope_f, rope_f, bspec(0), bspec(1), bspec(2), rope_b, rope_b,
                  const(dmat), const(gqf), const(gkf), const(gqb), const(gkb), const(gh)],
        out_specs=[pl.BlockSpec((1, c, w), lambda b, j: (b, j, 0)),
                   pl.BlockSpec((1, c, w), lambda b, j: (b, bw(j), 0))],
        out_shape=[jax.ShapeDtypeStruct((bsz, t_len, w), F32)] * 2,
        scratch_shapes=[pltpu.VMEM((n_pairs, PAIR, PAIR), F32)] * 2,
        compiler_params=_cparams("parallel", "arbitrary"),
    )(ret, ret, ret, cos_t, sin_t, ret, ret, ret, cos_t, sin_t, dmat, gqf, gkf, gqb, gkb, gh)


def _head_norm_gate(o, g, ones_bd):
    sq = o * o
    hi = sq.astype(BF16)
    lo = (sq - hi.astype(F32)).astype(BF16)
    ss = jnp.dot(hi, ones_bd, preferred_element_type=F32) + jnp.dot(lo, ones_bd, preferred_element_type=F32)
    return o * lax.rsqrt(ss * (1.0 / HEAD_DIM) + EPS) * jax.nn.silu(g)


def _route(lg):
    lane = lax.broadcasted_iota(jnp.int32, lg.shape, 1).astype(F32)
    big = float(1 << 20)
    neg = -jnp.inf
    gmask = lane < N_GROUPS
    lgm = jnp.where(gmask, lg, neg)
    gmax = jnp.max(lgm, axis=1, keepdims=True)
    gsum = jnp.sum(jnp.where(gmask, jnp.exp(lgm - gmax), 0.0), axis=1, keepdims=True)
    gp = 1.0 / gsum
    gi = jnp.min(jnp.where(gmask & (lgm == gmax), lane, big), axis=1, keepdims=True)
    lo_lane = N_GROUPS + EXPERTS_PER_GROUP * gi
    emask = (lane >= lo_lane) & (lane < lo_lane + EXPERTS_PER_GROUP)
    l1 = jnp.max(jnp.where(emask, lg, neg), axis=1, keepdims=True)
    i1 = jnp.min(jnp.where(emask & (lg == l1), lane, big), axis=1, keepdims=True)
    mask2 = emask & (lane != i1)
    l2 = jnp.max(jnp.where(mask2, lg, neg), axis=1, keepdims=True)
    i2 = jnp.min(jnp.where(mask2 & (lg == l2), lane, big), axis=1, keepdims=True)
    e21 = jnp.exp(l2 - l1)
    w1 = gp / (1.0 + e21)
    w2 = gp * e21 / (1.0 + e21)
    e1 = i1 - N_GROUPS
    e2 = i2 - N_GROUPS
    return jnp.where(lane == 0, w1, jnp.where(lane == 1, w2, jnp.where(lane == 2, e1, jnp.where(lane == 3, e2, 0.0))))


def _merge_kernel(x_ref, ys_ref, hof_ref, hob_ref, hgg_ref, rof_ref, rob_ref, rgg_ref, gz_ref,
                  ml_ref, mc_ref, g2_ref, wglu_ref, wbs_ref, wbh_ref, wbr_ref, wout_ref, ones_ref,
                  wr_ref, br_ref, xo_ref, h2_ref, rt_ref, *, tm, tiles_per_b, n_ctx):
    d = x_ref.shape[1]
    is_ctx = _is_ctx_rows(tm, tiles_per_b, n_ctx)
    y = jax.nn.gelu(ys_ref[...], approximate=True)
    y = y * jax.nn.sigmoid(jnp.dot(y.astype(BF16), wglu_ref[...], preferred_element_type=F32))
    t_s5 = jnp.dot(y.astype(BF16), wbs_ref[...], preferred_element_type=F32)
    yh = _head_norm_gate(hof_ref[...] + hob_ref[...], hgg_ref[...], ones_ref[...])
    t_hg = jnp.dot(yh.astype(BF16), wbh_ref[...], preferred_element_type=F32)
    yr = _head_norm_gate(rof_ref[...] + rob_ref[...], rgg_ref[...], ones_ref[...])
    t_ret = jnp.dot(yr.astype(BF16), wbr_ref[...], preferred_element_type=F32)
    merged = (jax.nn.sigmoid(gz_ref[:, 0:d]) * t_s5 + jax.nn.sigmoid(gz_ref[:, d:2 * d]) * t_hg
              + jax.nn.sigmoid(gz_ref[:, 2 * d:3 * d]) * t_ret)
    mix = jnp.dot(merged.astype(BF16), wout_ref[...], preferred_element_type=F32)
    xn = x_ref[...] + _mod_rows(ml_ref, mc_ref, 2, is_ctx) * mix
    xo_ref[...] = xn
    h2 = _rmsnorm_rows(xn, g2_ref[...])
    h2 = h2 * (1.0 + _mod_rows(ml_ref, mc_ref, 4, is_ctx)) + _mod_rows(ml_ref, mc_ref, 3, is_ctx)
    _store_token_tiles(h2_ref, h2)
    h_hi = h2.astype(BF16)
    h_lo = (h2 - h_hi.astype(F32)).astype(BF16)
    p_hi = jnp.dot(h_hi, wr_ref[...], preferred_element_type=F32)
    p_lo = jnp.dot(h_lo, wr_ref[:, 0:LANES], preferred_element_type=F32)
    rt_ref[...] = _route(p_hi[:, 0:LANES] + p_hi[:, LANES:2 * LANES] + p_lo + br_ref[...])


def _merge(x2, ys5, hg_of, hg_ob, hg2, ret_of, ret_ob, ret2, gz, mod_l, mod_c, g2, wts, t_len, n_ctx):
    m, d = x2.shape
    w_s5 = ys5.shape[1]
    w_h = hg_of.shape[1]
    tm = _row_tile(t_len)
    tpb = t_len // tm
    wglu, wbs, wbh, wbr, wout, ones_bd, wr, br = wts

    def rows(width, col=0):
        return pl.BlockSpec((tm, width), lambda i: (i, col))

    def const(a):
        nd = a.ndim
        return pl.BlockSpec(a.shape, lambda i: (0,) * nd)

    kern = functools.partial(_merge_kernel, tm=tm, tiles_per_b=tpb, n_ctx=n_ctx)
    return pl.pallas_call(
        kern,
        grid=(m // tm,),
        in_specs=[rows(d), rows(w_s5), rows(w_h), rows(w_h), rows(w_h, 4), rows(w_h), rows(w_h), rows(w_h, 3),
                  rows(3 * d),
                  pl.BlockSpec((1, N_MOD, d), lambda i: (i // tpb, 0, 0)),
                  pl.BlockSpec((1, N_MOD, d), lambda i: (0, 0, 0)),
                  const(g2), const(wglu), const(wbs), const(wbh), const(wbr), const(wout), const(ones_bd),
                  const(wr), const(br)],
        out_specs=[rows(d), pl.BlockSpec((tm * SUBLANES, LANES), lambda i: (i, 0)), rows(LANES)],
        out_shape=[jax.ShapeDtypeStruct((m, d), F32), jax.ShapeDtypeStruct((m * SUBLANES, LANES), F32),
                   jax.ShapeDtypeStruct((m, LANES), F32)],
        compiler_params=_cparams("parallel"),
    )(x2, ys5, hg_of, hg_ob, hg2, ret_of, ret_ob, ret2, gz, mod_l, mod_c, g2, wglu, wbs, wbh, wbr, wout,
      ones_bd, wr, br)


def _ffn_kernel(lay_ref, be_ref, nu_ref, bs_ref, cnt_ref, src_ref, dst_ref, h2_hbm, wg_ref, wu_ref, wd_ref, y2_hbm,
                wg_scr, wu_scr, wd_scr, xbuf, obuf, gsem, ssem, *, n_spare_rows):
    i = pl.program_id(0)
    nu = nu_ref[0]
    slot = i % 2
    tile = SUBLANES

    def gather_copy(rows_src, rows_dst, sl):
        return pltpu.make_async_copy(h2_hbm.at[rows_src, :], xbuf.at[sl, rows_dst, :], gsem.at[sl])

    def scatter_copy(rows_src, rows_dst, sl):
        return pltpu.make_async_copy(obuf.at[sl, rows_src, :], y2_hbm.at[rows_dst, :], ssem.at[sl])

    def n_groups(blk):
        return lax.shift_right_logical(cnt_ref[blk] + (ROW_GROUP - 1), ROW_GROUP.bit_length() - 1)

    def start_gather(blk, sl):
        base = bs_ref[blk]

        def body(g, carry):
            for j in range(ROW_GROUP):
                r = g * ROW_GROUP + j
                gather_copy(pl.ds(pl.multiple_of(src_ref[base + r], tile), tile),
                            pl.ds(pl.multiple_of(r * tile, tile), tile), sl).start()
            return carry

        lax.fori_loop(0, n_groups(blk), body, 0)

    def wait_gather(blk, sl):
        n = pl.multiple_of(n_groups(blk) * (ROW_GROUP * tile), tile)
        gather_copy(pl.ds(0, n), pl.ds(0, n), sl).wait()

    def start_scatter(blk, sl):
        base = bs_ref[blk]

        def body(g, carry):
            for j in range(ROW_GROUP):
                r = g * ROW_GROUP + j
                scatter_copy(pl.ds(pl.multiple_of(r * tile, tile), tile),
                             pl.ds(pl.multiple_of(dst_ref[base + r], tile), tile), sl).start(priority=1)
            return carry

        lax.fori_loop(0, n_groups(blk), body, 0)

    def wait_scatter(blk, sl):
        n = pl.multiple_of(n_groups(blk) * (ROW_GROUP * tile), tile)
        scatter_copy(pl.ds(0, n), pl.ds(0, n), sl).wait()

    @pl.when(i == 0)
    def _():
        xbuf[...] = jnp.zeros_like(xbuf)
        first_spare = y2_hbm.shape[0] - n_spare_rows
        clear = scatter_copy(pl.ds(0, n_spare_rows), pl.ds(first_spare, n_spare_rows), 0)
        obuf[0, pl.ds(0, n_spare_rows), :] = jnp.zeros((n_spare_rows, LANES), F32)
        clear.start()
        clear.wait()

    @pl.when((i == 0) & (nu > 0))
    def _():
        start_gather(0, 0)

    @pl.when(i + 1 < nu)
    def _():
        start_gather(i + 1, 1 - slot)

    prev = be_ref[jnp.maximum(i - 1, 0)]

    @pl.when((i < nu) & ((i == 0) | (be_ref[i] != prev)))
    def _():
        wg_scr[...] = wg_ref[0, 0].astype(BF16)
        wu_scr[...] = wu_ref[0, 0].astype(BF16)
        wd_scr[...] = wd_ref[0, 0].astype(BF16)

    @pl.when(i < nu)
    def _():
        wait_gather(i, slot)
        xb = _load_token_tiles(xbuf.at[slot]).astype(BF16)
        hid = (jax.nn.silu(jnp.dot(xb, wg_scr[...], preferred_element_type=F32))
               * jnp.dot(xb, wu_scr[...], preferred_element_type=F32))
        y = jnp.dot(hid.astype(BF16), wd_scr[...], preferred_element_type=F32)
        _store_token_tiles(obuf.at[slot], y)
        start_scatter(i, slot)

    @pl.when((i >= 1) & (i < nu + 1))
    def _():
        wait_scatter(i - 1, 1 - slot)

    @pl.when((i == pl.num_programs(0) - 1) & (i < nu))
    def _():
        wait_scatter(i, slot)


def _expert_ffn(layer, blk_e, n_used, blk_start, blk_cnt, src_rows, dst_rows, h2t, wg, wu, wd, n_tok):
    d, f = wg.shape[2], wg.shape[3]
    bm = MOE_BLOCK
    n_blocks = blk_e.shape[0] - 1
    n_spare = N_EXPERTS * (ROW_GROUP - 1)

    def wspec(shape):
        return pl.BlockSpec((1, 1) + shape, lambda i, lay, be, *_: (lay[0], be[i], 0, 0))

    grid_spec = pltpu.PrefetchScalarGridSpec(
        num_scalar_prefetch=7,
        grid=(n_blocks,),
        in_specs=[pl.BlockSpec(memory_space=pl.ANY), wspec((d, f)), wspec((d, f)), wspec((f, d))],
        out_specs=pl.BlockSpec(memory_space=pl.ANY),
        scratch_shapes=[pltpu.VMEM((d, f), BF16), pltpu.VMEM((d, f), BF16), pltpu.VMEM((f, d), BF16),
                        pltpu.VMEM((2, bm * SUBLANES, LANES), F32), pltpu.VMEM((2, bm * SUBLANES, LANES), F32),
                        pltpu.SemaphoreType.DMA((2,)), pltpu.SemaphoreType.DMA((2,))],
    )
    return pl.pallas_call(
        functools.partial(_ffn_kernel, n_spare_rows=n_spare * SUBLANES),
        grid_spec=grid_spec,
        out_shape=jax.ShapeDtypeStruct(((n_tok * TOP_K + n_spare) * SUBLANES, LANES), F32),
        compiler_params=_cparams("arbitrary"),
    )(layer, blk_e, n_used, blk_start, blk_cnt, src_rows, dst_rows, h2t, wg, wu, wd)


def _combine_kernel(x_ref, y2_ref, rt_ref, ml_ref, mc_ref, o_ref, *, tm, tiles_per_b, n_ctx):
    is_ctx = _is_ctx_rows(tm, tiles_per_b, n_ctx)
    rt = rt_ref[...]
    step = TOP_K * SUBLANES
    y0 = jnp.concatenate([y2_ref[pl.ds(k, tm, stride=step), :] for k in range(SUBLANES)], axis=1)
    y1 = jnp.concatenate([y2_ref[pl.ds(SUBLANES + k, tm, stride=step), :] for k in range(SUBLANES)], axis=1)
    y = rt[:, 0:1] * y0 + rt[:, 1:2] * y1
    o_ref[...] = x_ref[...] + _mod_rows(ml_ref, mc_ref, 5, is_ctx) * y


def _combine(x2, y2, route, mod_l, mod_c, t_len, n_ctx):
    m, d = x2.shape
    tm = _row_tile(t_len)
    tpb = t_len // tm
    kern = functools.partial(_combine_kernel, tm=tm, tiles_per_b=tpb, n_ctx=n_ctx)
    return pl.pallas_call(
        kern,
        grid=(m // tm,),
        in_specs=[pl.BlockSpec((tm, d), lambda i: (i, 0)),
                  pl.BlockSpec((tm * TOP_K * SUBLANES, LANES), lambda i: (i, 0)),
                  pl.BlockSpec((tm, LANES), lambda i: (i, 0)),
                  pl.BlockSpec((1, N_MOD, d), lambda i: (i // tpb, 0, 0)),
                  pl.BlockSpec((1, N_MOD, d), lambda i: (0, 0, 0))],
        out_specs=pl.BlockSpec((tm, d), lambda i: (i, 0)),
        out_shape=jax.ShapeDtypeStruct((m, d), F32),
        compiler_params=_cparams("parallel"),
    )(x2, y2, route, mod_l, mod_c)


def _final_norm_kernel(x_ref, g_ref, o_ref):
    o_ref[0] = _rmsnorm_rows(x_ref[0], g_ref[...])


def _final_norm(x3, g, n_ctx):
    bsz, t_len, d = x3.shape
    n_lat = t_len - n_ctx
    tm = n_ctx
    off = n_ctx // tm
    return pl.pallas_call(
        _final_norm_kernel,
        grid=(bsz, n_lat // tm),
        in_specs=[pl.BlockSpec((1, tm, d), lambda b, j: (b, j + off, 0)),
                  pl.BlockSpec((1, d), lambda b, j: (0, 0))],
        out_specs=pl.BlockSpec((1, tm, d), lambda b, j: (b, j, 0)),
        out_shape=jax.ShapeDtypeStruct((bsz, n_lat, d), F32),
        compiler_params=_cparams("parallel", "parallel"),
    )(x3, g.reshape(1, d))


def _dispatch_tables(eid, n_tok):
    a = n_tok * TOP_K
    bm = MOE_BLOCK
    pad = ROW_GROUP - 1
    flat_e = eid.reshape(a)
    experts = jnp.arange(N_EXPERTS, dtype=jnp.int32)
    counts = jnp.sum(flat_e[:, None] == experts[None, :], axis=0, dtype=jnp.int32)
    nblk_e = (counts + bm - 1) // bm
    blk_end = jnp.cumsum(nblk_e)
    blk_first = blk_end - nblk_e
    order = jnp.argsort(jnp.concatenate([flat_e, jnp.repeat(experts, pad)])).astype(jnp.int32)
    seg = counts + pad
    seg_start = jnp.cumsum(seg) - seg
    src_rows = jnp.where(order < a, lax.shift_right_logical(order, 1), 0) * SUBLANES
    dst_rows = order * SUBLANES
    n_blocks = -(-a // bm) + N_EXPERTS
    blk = jnp.arange(n_blocks + 1, dtype=jnp.int32)
    blk_e = jnp.minimum(jnp.sum(blk_end[None, :] <= blk[:, None], axis=1, dtype=jnp.int32), N_EXPERTS - 1)
    within = (blk - blk_first[blk_e]) * bm
    n_used = blk_end[-1].reshape(1)
    blk_start = jnp.clip(seg_start[blk_e] + within, 0, order.shape[0] - 1)
    blk_cnt = jnp.where(blk < n_used[0], jnp.clip(counts[blk_e] - within, 0, bm), 0)
    return src_rows, dst_rows, blk_e, blk_start, blk_cnt, n_used


def _moe(layer, h2t, route, wg, wu, wd):
    n_tok = route.shape[0]
    eid = route[:, 2:4].astype(jnp.int32)
    src_rows, dst_rows, blk_e, blk_start, blk_cnt, n_used = _dispatch_tables(eid, n_tok)
    return _expert_ffn(layer, blk_e, n_used, blk_start, blk_cnt, src_rows, dst_rows, h2t, wg, wu, wd, n_tok)


def kernel(x, c, ctx, c_ctx, w_ada, b_ada, norm1_g, norm2_g, w_in, s5_lam_re, s5_lam_im, s5_log_dt, s5_b_re, s5_b_im, s5_c_re, s5_c_im, s5_d, s5_w_glu, hgrn_lb_raw, w_branch_s5, w_branch_hgrn, w_branch_ret, w_out, moe_w_group, moe_b_group, moe_w_expert, moe_b_expert, moe_w_gate, moe_w_up, moe_w_down, final_norm_g):
    bsz, n_lat, d = x.shape
    n_ctx = ctx.shape[1]
    t_len = n_ctx + n_lat
    depth = w_ada.shape[0]
    m = bsz * t_len
    w_s5 = s5_d.shape[1]
    w_hg = w_branch_hgrn.shape[1]
    w_ret = w_branch_ret.shape[1]
    n_heads = w_ret // HEAD_DIM

    pad_rows = (-(bsz + 1)) % SUBLANES
    cc = jnp.concatenate([c, c_ctx[None, :], jnp.zeros((pad_rows, d), F32)], axis=0)
    mods = _ada_tables(cc, w_ada, b_ada).reshape(depth, bsz + 1 + pad_rows, N_MOD, d)

    p_lb = jax.nn.softmax(hgrn_lb_raw.astype(F32), axis=0)
    lower_bounds = jnp.cumsum(p_lb, axis=0) - p_lb[0:1]
    ret_consts = _ret_constants(n_heads, t_len, n_ctx)
    ones_bd = jnp.kron(jnp.eye(w_hg // HEAD_DIM, dtype=F32), jnp.ones((HEAD_DIM, HEAD_DIM), F32)).astype(BF16)

    xs = jnp.concatenate([ctx, x], axis=1).reshape(m, d)
    off_hg = w_s5
    off_ret = off_hg + 5 * w_hg
    off_gz = off_ret + 4 * w_ret
    w_in_bf = w_in.astype(BF16)
    s5w = jax.vmap(_s5_weights)(s5_lam_re, s5_lam_im, s5_log_dt, s5_b_re, s5_b_im, s5_c_re, s5_c_im, s5_d)
    for l in range(depth):
        mod_l = mods[l, :bsz]
        mod_c = mods[l, bsz:bsz + 1]
        u, hg, ret, gz = _project(xs, norm1_g[l], mod_l, mod_c, w_in_bf, l,
                                  (0, off_hg, off_ret, off_gz, w_in.shape[2]), t_len, n_ctx)

        ys5 = _s5_mix(u.reshape(bsz, t_len, w_s5), *(w[l] for w in s5w), n_ctx).reshape(m, w_s5)
        hg_of, hg_ob = _hg_mix(hg.reshape(bsz, t_len, 5 * w_hg), lower_bounds[l], n_ctx)
        ret_of, ret_ob = _ret_mix(ret.reshape(bsz, t_len, 4 * w_ret), ret_consts, n_ctx)

        wr = jnp.concatenate([moe_w_group[l], moe_w_expert[l],
                              jnp.zeros((d, LANES - N_GROUPS - N_EXPERTS), F32)], axis=1)
        br = jnp.concatenate([moe_b_group[l], moe_b_expert[l],
                              jnp.zeros((LANES - N_GROUPS - N_EXPERTS,), F32)])[None, :]
        wr_hi = wr.astype(BF16)
        wr = jnp.concatenate([wr_hi, (wr - wr_hi.astype(F32)).astype(BF16)], axis=1)
        wts = (s5_w_glu[l].astype(BF16), w_branch_s5[l].astype(BF16), w_branch_hgrn[l].astype(BF16),
               w_branch_ret[l].astype(BF16), w_out[l].astype(BF16), ones_bd, wr, br)
        xs, h2, route = _merge(xs, ys5, hg_of.reshape(m, w_hg), hg_ob.reshape(m, w_hg), hg,
                               ret_of.reshape(m, w_ret), ret_ob.reshape(m, w_ret), ret, gz,
                               mod_l, mod_c, norm2_g[l].reshape(1, d), wts, t_len, n_ctx)
        y2 = _moe(jnp.full((1,), l, jnp.int32), h2, route, moe_w_gate, moe_w_up, moe_w_down)
        xs = _combine(xs, y2, route, mod_l, mod_c, t_len, n_ctx)
    return _final_norm(xs.reshape(bsz, t_len, d), final_norm_g, n_ctx)
```
